```python
import jax, jax.numpy as jnp
from jax import lax
import numpy as np

D_MODEL = 2048
BATCH = 2
SEQ = 8192
DEPTH = 4

N_MIXERS = 2
N_NSA_LAYERS = (DEPTH + N_MIXERS - 1) // N_MIXERS
N_HGRN_LAYERS = DEPTH // N_MIXERS
RMS_EPS = 1e-6
NEG = -1e30
BIG = 1e30

NSA_HEADS = 16
NSA_KV_GROUPS = 4
NSA_HEAD_DIM = 128
NSA_REP = NSA_HEADS // NSA_KV_GROUPS
CMP_BLOCK = 32
CMP_STRIDE = 16
CMP_HIDDEN = NSA_HEAD_DIM
SEL_BLOCK = 64
SEL_TOPK = 16
WINDOW = 512
Q_BLOCK = 128
SEL_Q_CHUNK = 16
ROPE_THETA = 500000.0
ROPE_DIM = NSA_HEAD_DIM // 4
NSA_KVW = NSA_KV_GROUPS * NSA_HEAD_DIM
NSA_IN = NSA_HEADS * NSA_HEAD_DIM + 6 * NSA_KVW + 3 * NSA_HEADS + NSA_HEADS * NSA_HEAD_DIM

HGRN_HEADS = 16
HGRN_F_DIM = 128
HGRN_I_DIM = D_MODEL // HGRN_HEADS
HGRN_CHUNK = 32
HGRN_IN = 2 * HGRN_HEADS * HGRN_F_DIM + 2 * HGRN_HEADS * HGRN_I_DIM

kernel_name = "nsa_hgrn2_interleaved_hybrid"


def rmsnorm(x, w):
    xf = x.astype(jnp.float32)
    y = xf * lax.rsqrt(jnp.mean(xf * xf, axis=-1, keepdims=True) + RMS_EPS)
    return (y * w.astype(jnp.float32)).astype(x.dtype)


def partial_rope(t, pos):
    half = ROPE_DIM // 2
    inv = ROPE_THETA ** (-jnp.arange(half, dtype=jnp.float32) / half)
    ang = pos.astype(jnp.float32)[..., None] * inv
    cos = jnp.cos(ang)[:, :, None, :]
    sin = jnp.sin(ang)[:, :, None, :]
    tr = t[..., :ROPE_DIM].astype(jnp.float32)
    t1, t2 = tr[..., :half], tr[..., half:]
    rot = jnp.concatenate([t1 * cos - t2 * sin, t2 * cos + t1 * sin], axis=-1)
    return jnp.concatenate([rot.astype(t.dtype), t[..., ROPE_DIM:]], axis=-1)


def compress_blocks(t, pos_emb, w1, w2):
    S = t.shape[2]
    n_cmp = (S - CMP_BLOCK) // CMP_STRIDE + 1
    idx = jnp.arange(n_cmp)[:, None] * CMP_STRIDE + jnp.arange(CMP_BLOCK)[None, :]
    blocks = t[:, :, idx, :] + pos_emb
    hid = jax.nn.silu(jnp.einsum('bgnld,ldh->bgnh', blocks, w1))
    return jnp.einsum('bgnh,he->bgne', hid, w2)


def nsa_mixer(h, pos, w_in, ck_pos, ck_w1, ck_w2, cv_pos, cv_w1, cv_w2, w_out):
    B, S, _ = h.shape
    H, G, R, d = NSA_HEADS, NSA_KV_GROUPS, NSA_REP, NSA_HEAD_DIM
    scale = d ** -0.5
    proj = h @ w_in
    cuts = [int(c) for c in np.cumsum([H * d] + [NSA_KVW] * 6 + [3 * H])]
    q, kc, vc, ks, vs, kw, vw, gl, z = jnp.split(proj, cuts, axis=-1)

    q = partial_rope(q.reshape(B, S, H, d), pos)
    qg = q.reshape(B, S, G, R, d).transpose(0, 2, 3, 1, 4)

    def kv_layout(t, rotate):
        t = t.reshape(B, S, G, d)
        if rotate:
            t = partial_rope(t, pos)
        return t.transpose(0, 2, 1, 3)

    k_cmp = compress_blocks(kv_layout(kc, True), ck_pos, ck_w1, ck_w2)
    v_cmp = compress_blocks(kv_layout(vc, False), cv_pos, cv_w1, cv_w2)
    k_sel, v_sel = kv_layout(ks, True), kv_layout(vs, False)
    k_win, v_win = kv_layout(kw, True), kv_layout(vw, False)

    n_cmp = k_cmp.shape[2]
    n_sel = S // SEL_BLOCK
    cmp_start = jnp.arange(n_cmp) * CMP_STRIDE
    cmp_end = cmp_start + CMP_BLOCK - 1
    sel_ids = jnp.arange(n_sel)
    overlap = ((cmp_start[:, None] < (sel_ids[None, :] + 1) * SEL_BLOCK)
               & ((cmp_start + CMP_BLOCK)[:, None] > sel_ids[None, :] * SEL_BLOCK)).astype(jnp.float32)

    kw_pad = jnp.pad(k_win, ((0, 0), (0, 0), (WINDOW, 0), (0, 0)))
    vw_pad = jnp.pad(v_win, ((0, 0), (0, 0), (WINDOW, 0), (0, 0)))
    span = WINDOW + Q_BLOCK

    nqb = S // Q_BLOCK
    q_blocks = qg.reshape(B, G, R, nqb, Q_BLOCK, d).transpose(3, 0, 1, 2, 4, 5)
    starts = jnp.arange(nqb, dtype=jnp.int32) * Q_BLOCK

    def cmp_win_step(inp):
        qb, s0 = inp
        t = s0 + jnp.arange(Q_BLOCK)
        sc = jnp.einsum('bgrqd,bgnd->bgrqn', qb, k_cmp).astype(jnp.float32) * scale
        mc = cmp_end[None, :] <= t[:, None]
        pc = jax.nn.softmax(jnp.where(mc, sc, NEG), axis=-1) * mc
        o_c = jnp.einsum('bgrqn,bgnd->bgrqd', pc.astype(v_cmp.dtype), v_cmp)
        imp = jnp.einsum('bgrqn,nk->bgqk', pc, overlap)
        kb = lax.dynamic_slice_in_dim(kw_pad, s0, span, axis=2)
        vb = lax.dynamic_slice_in_dim(vw_pad, s0, span, axis=2)
        kp = s0 - WINDOW + jnp.arange(span)
        mw = (kp[None, :] <= t[:, None]) & (kp[None, :] > t[:, None] - WINDOW) & (kp[None, :] >= 0)
        sw = jnp.einsum('bgrqd,bgkd->bgrqk', qb, kb).astype(jnp.float32) * scale
        pw = jax.nn.softmax(jnp.where(mw, sw, NEG), axis=-1)
        o_w = jnp.einsum('bgrqk,bgkd->bgrqd', pw.astype(vb.dtype), vb)
        return o_c, o_w, imp

    o_c, o_w, imp = lax.map(cmp_win_step, (q_blocks, starts))

    def from_blocks(o):
        return o.transpose(1, 0, 4, 2, 3, 5).reshape(B, S, H, d)

    o_cmp, o_win = from_blocks(o_c), from_blocks(o_w)
    imp = imp.transpose(1, 2, 0, 3, 4).reshape(B, G, S, n_sel)

    tpos = jnp.arange(S)
    cur = tpos // SEL_BLOCK
    valid = sel_ids[None, :] <= cur[:, None]
    forced = (sel_ids[None, :] == 0) | (sel_ids[None, :] == cur[:, None]) | (sel_ids[None, :] == cur[:, None] - 1)
    score = jnp.where(valid, jnp.where(forced, BIG, imp), NEG)
    n_top = min(SEL_TOPK, n_sel)
    _, sel_idx = lax.top_k(score, n_top)

    k_blk = k_sel.reshape(B, G, n_sel, SEL_BLOCK, d)
    v_blk = v_sel.reshape(B, G, n_sel, SEL_BLOCK, d)
    nsc = S // SEL_Q_CHUNK
    q_chunks = qg.reshape(B, G, R, nsc, SEL_Q_CHUNK, d).transpose(3, 0, 1, 2, 4, 5)
    idx_chunks = sel_idx.reshape(B, G, nsc, SEL_Q_CHUNK, n_top).transpose(2, 0, 1, 3, 4)
    c_starts = jnp.arange(nsc, dtype=jnp.int32) * SEL_Q_CHUNK
    gather = jax.vmap(jax.vmap(lambda tb, ib: tb[ib]))

    def sel_step(inp):
        qc, ic, s0 = inp
        t = s0 + jnp.arange(SEL_Q_CHUNK)
        kg = gather(k_blk, ic)
        vg = gather(v_blk, ic)
        keypos = ic[..., None] * SEL_BLOCK + jnp.arange(SEL_BLOCK)
        m = keypos <= t[None, None, :, None, None]
        s = jnp.einsum('bgrqd,bgqnld->bgrqnl', qc, kg).astype(jnp.float32) * scale
        s = jnp.where(m[:, :, None], s, NEG).reshape(B, G, R, SEL_Q_CHUNK, n_top * SEL_BLOCK)
        p = jax.nn.softmax(s, axis=-1).reshape(B, G, R, SEL_Q_CHUNK, n_top, SEL_BLOCK)
        return jnp.einsum('bgrqnl,bgqnld->bgrqd', p.astype(vg.dtype), vg)

    o_s = lax.map(sel_step, (q_chunks, idx_chunks, c_starts))
    o_sel = from_blocks(o_s)

    g = jax.nn.sigmoid(gl.astype(jnp.float32)).reshape(B, S, 3, H)[..., None]
    o = (g[:, :, 0] * o_cmp.astype(jnp.float32) + g[:, :, 1] * o_sel.astype(jnp.float32)
         + g[:, :, 2] * o_win.astype(jnp.float32))
    o = o.reshape(B, S, H * d) * jax.nn.silu(z.astype(jnp.float32))
    return o.astype(h.dtype) @ w_out


def hgrn2_mixer(h, lb, w_in, gnorm_w, w_out):
    B, S, _ = h.shape
    H, Df, Di, C = HGRN_HEADS, HGRN_F_DIM, HGRN_I_DIM, HGRN_CHUNK
    proj = h @ w_in
    q, f, i, z = jnp.split(proj, [H * Df, 2 * H * Df, 2 * H * Df + H * Di], axis=-1)
    fl = f.astype(jnp.float32)
    lb = lb.astype(jnp.float32)
    log_f = jnp.logaddexp(jnp.log(lb), jnp.log1p(-lb) + jax.nn.log_sigmoid(fl))
    k = (1.0 - lb) * jax.nn.sigmoid(-fl)
    q = jax.nn.silu(q.astype(jnp.float32))
    n_chunks = S // C

    def to_chunks(t, dim):
        return t.reshape(B, n_chunks, C, H, dim).transpose(1, 0, 3, 2, 4)

    xs = (to_chunks(q, Df), to_chunks(k, Df), to_chunks(i.astype(jnp.float32), Di), to_chunks(log_f, Df))
    tri = jnp.tril(jnp.ones((C, C), dtype=bool))

    def step(state, inp):
        qc, kc, ic, gc = inp
        b = jnp.cumsum(gc, axis=2)
        diff = b[:, :, :, None, :] - b[:, :, None, :, :]
        decay = jnp.exp(jnp.where(tri[None, None, :, :, None], diff, -jnp.inf))
        att = jnp.einsum('bhtd,bhtsd,bhsd->bhts', qc, decay, kc)
        o = jnp.einsum('bhts,bhsi->bhti', att, ic) + jnp.einsum('bhtd,bhdi->bhti', qc * jnp.exp(b), state)
        b_last = b[:, :, -1:, :]
        new_state = (jnp.exp(b_last[:, :, 0, :])[..., None] * state
                     + jnp.einsum('bhsd,bhsi->bhdi', kc * jnp.exp(b_last - b), ic))
        return new_state, o

    state0 = jnp.zeros((B, H, Df, Di), jnp.float32)
    _, o = lax.scan(step, state0, xs)
    o = o.transpose(1, 0, 3, 2, 4).reshape(B, S, H, Di)
    o = o * lax.rsqrt(jnp.mean(o * o, axis=-1, keepdims=True) + RMS_EPS) * gnorm_w.astype(jnp.float32)
    o = o * jax.nn.silu(z.astype(jnp.float32).reshape(B, S, H, Di))
    return o.reshape(B, S, H * Di).astype(h.dtype) @ w_out


def setup_inputs(seed: int = 0) -> dict:
    key = jax.random.key(seed)
    ks = jax.random.split(key, 20)
    f32 = jnp.float32
    D, d, l = D_MODEL, NSA_HEAD_DIM, CMP_BLOCK
    offset = jax.random.randint(ks[0], (BATCH,), 0, 4096, dtype=jnp.int32)
    positions = offset[:, None] + jnp.arange(SEQ, dtype=jnp.int32)[None, :]
    return {
        "x": jax.random.normal(ks[1], (BATCH, SEQ, D), f32),
        "positions": positions,
        "norm_w": 1.0 + 0.01 * jax.random.normal(ks[2], (DEPTH, D), f32),
        "final_norm_w": 1.0 + 0.01 * jax.random.normal(ks[3], (D,), f32),
        "nsa_w_in": jax.random.normal(ks[4], (N_NSA_LAYERS, D, NSA_IN), f32) * D ** -0.5,
        "nsa_ck_pos": 0.02 * jax.random.normal(ks[5], (N_NSA_LAYERS, l, d), f32),
        "nsa_ck_w1": jax.random.normal(ks[6], (N_NSA_LAYERS, l, d, CMP_HIDDEN), f32) * (l * d) ** -0.5,
        "nsa_ck_w2": jax.random.normal(ks[7], (N_NSA_LAYERS, CMP_HIDDEN, d), f32) * CMP_HIDDEN ** -0.5,
        "nsa_cv_pos": 0.02 * jax.random.normal(ks[8], (N_NSA_LAYERS, l, d), f32),
        "nsa_cv_w1": jax.random.normal(ks[9], (N_NSA_LAYERS, l, d, CMP_HIDDEN), f32) * (l * d) ** -0.5,
        "nsa_cv_w2": jax.random.normal(ks[10], (N_NSA_LAYERS, CMP_HIDDEN, d), f32) * CMP_HIDDEN ** -0.5,
        "nsa_w_out": jax.random.normal(ks[11], (N_NSA_LAYERS, NSA_HEADS * d, D), f32) * (NSA_HEADS * d) ** -0.5,
        "hgrn_w_in": jax.random.normal(ks[12], (N_HGRN_LAYERS, D, HGRN_IN), f32) * D ** -0.5,
        "hgrn_lb_logits": jax.random.normal(ks[13], (N_HGRN_LAYERS, HGRN_HEADS * HGRN_F_DIM), f32),
        "hgrn_gnorm_w": 1.0 + 0.01 * jax.random.normal(ks[14], (N_HGRN_LAYERS, HGRN_I_DIM), f32),
        "hgrn_w_out": jax.random.normal(ks[15], (N_HGRN_LAYERS, HGRN_HEADS * HGRN_I_DIM, D), f32) * (HGRN_HEADS * HGRN_I_DIM) ** -0.5,
    }


def reference(x, positions, norm_w, final_norm_w, nsa_w_in, nsa_ck_pos, nsa_ck_w1, nsa_ck_w2,
              nsa_cv_pos, nsa_cv_w1, nsa_cv_w2, nsa_w_out, hgrn_w_in, hgrn_lb_logits,
              hgrn_gnorm_w, hgrn_w_out):
    p_lb = jax.nn.softmax(hgrn_lb_logits.astype(jnp.float32), axis=0)
    cum = jnp.cumsum(p_lb, axis=0)
    lower_bounds = cum - cum[0]
    for layer in range(DEPTH):
        hn = rmsnorm(x, norm_w[layer])
        j = layer // N_MIXERS
        if layer % N_MIXERS == 0:
            y = nsa_mixer(hn, positions, nsa_w_in[j], nsa_ck_pos[j], nsa_ck_w1[j], nsa_ck_w2[j],
                          nsa_cv_pos[j], nsa_cv_w1[j], nsa_cv_w2[j], nsa_w_out[j])
        else:
            y = hgrn2_mixer(hn, lower_bounds[j], hgrn_w_in[j], hgrn_gnorm_w[j], hgrn_w_out[j])
        x = x + y.astype(x.dtype)
    return rmsnorm(x, final_norm_w)
```

```python
import functools
import math

import jax
import jax.numpy as jnp
from jax import lax
from jax.experimental import pallas as pl
from jax.experimental.pallas import tpu as pltpu

F32 = jnp.float32
BF16 = jnp.bfloat16

RMS_EPS = 1e-6
NEG = -1e30
BIG = 1e30
BELOW_NEG = -3e38
HEAD_DIM = 128
NSA_HEADS = 16
NSA_KV_GROUPS = 4
NSA_REP = NSA_HEADS // NSA_KV_GROUPS
CMP_BLOCK = 32
CMP_STRIDE = 16
SEL_BLOCK = 64
SEL_TOPK = 16
WINDOW = 512
ROPE_THETA = 500000.0
ROPE_DIM = HEAD_DIM // 4
HGRN_HEADS = 16
HGRN_F_DIM = 128
HGRN_I_DIM = 128

LANES = 128
VMEM_LIMIT_BYTES = 56 * 1024 * 1024

ATTN_SCALE = HEAD_DIM ** -0.5
EXP2_SCALE = ATTN_SCALE * math.log2(math.e)

PROJ_TM = 1024
CMP_TQ = 128
WIN_TQ = 256
SEL_TQ = 256
SEL_TK = 512
OUT_TM = 256
HGRN_CHUNK = 128
HGRN_SUB = 16


def _cparams(*sem):
    return pltpu.CompilerParams(dimension_semantics=sem, vmem_limit_bytes=VMEM_LIMIT_BYTES)


def _dot(a, b):
    return jnp.dot(a, b, preferred_element_type=F32)


def _dot_nt(a, b):
    return lax.dot_general(a, b, (((1,), (1,)), ((), ())), preferred_element_type=F32)


def _silu(v):
    return v * jax.nn.sigmoid(v)


def _rope_table_kernel(pos_ref, inv_ref, cos_ref, sin_ref):
    ang = pos_ref[...] * inv_ref[...]
    lane = lax.broadcasted_iota(jnp.int32, ang.shape, 1)
    c = jnp.cos(ang)
    s = jnp.sin(ang)
    cos_ref[...] = jnp.where(lane < ROPE_DIM, c, 1.0)
    sin_ref[...] = jnp.where(lane < ROPE_DIM // 2, -s, jnp.where(lane < ROPE_DIM, s, 0.0))


def _rope_tables(positions):
    T = positions.size
    tm = min(T, 2048)
    half = ROPE_DIM // 2
    inv = ROPE_THETA ** (-jnp.arange(half, dtype=F32) / half)
    inv_pat = jnp.concatenate([inv, inv, jnp.zeros((LANES - ROPE_DIM,), F32)]).reshape(1, LANES)
    pos = positions.astype(F32).reshape(T, 1)
    return pl.pallas_call(
        _rope_table_kernel,
        grid=(T // tm,),
        in_specs=[pl.BlockSpec((tm, 1), lambda i: (i, 0)),
                  pl.BlockSpec((1, LANES), lambda i: (0, 0))],
        out_specs=[pl.BlockSpec((tm, LANES), lambda i: (i, 0))] * 2,
        out_shape=[jax.ShapeDtypeStruct((T, LANES), F32)] * 2,
        compiler_params=_cparams("parallel"),
        name="rope_tables",
    )(pos, inv_pat)


def _norm_proj_kernel(*refs, heads_per_tile, rope_lo, rope_hi):
    if rope_hi > rope_lo:
        x_ref, nw_ref, w_ref, cos_ref, sin_ref, o_ref, hn_ref = refs
    else:
        x_ref, nw_ref, w_ref, o_ref, hn_ref = refs
    j = pl.program_id(1)

    @pl.when(j == 0)
    def _():
        x = x_ref[...]
        ms = jnp.mean(x * x, axis=-1, keepdims=True)
        hn_ref[...] = (x * lax.rsqrt(ms + RMS_EPS) * nw_ref[...]).astype(BF16)

    acc = _dot(hn_ref[...], w_ref[...])
    if rope_hi <= rope_lo:
        o_ref[...] = acc.astype(o_ref.dtype)
        return

    half = ROPE_DIM // 2
    for h in range(heads_per_tile):
        sl = slice(h * HEAD_DIM, (h + 1) * HEAD_DIM)
        head = j * heads_per_tile + h
        is_rope = jnp.logical_and(head >= rope_lo, head < rope_hi)

        @pl.when(is_rope)
        def _(sl=sl):
            t = acc[:, sl]
            lane = lax.broadcasted_iota(jnp.int32, t.shape, 1)
            sw = jnp.where(lane < half,
                           pltpu.roll(t, HEAD_DIM - half, axis=1),
                           pltpu.roll(t, half, axis=1))
            o_ref[:, sl] = (t * cos_ref[...] + sw * sin_ref[...]).astype(o_ref.dtype)

        @pl.when(jnp.logical_not(is_rope))
        def _(sl=sl):
            o_ref[:, sl] = acc[:, sl].astype(o_ref.dtype)


def _norm_proj(x, norm_w, w, out_dtype, tn, rope=None, tabs=None):
    T, D = x.shape
    N = w.shape[1]
    tm = min(T, PROJ_TM)
    rope_lo, rope_hi = rope if rope is not None else (0, 0)
    in_specs = [pl.BlockSpec((tm, D), lambda i, j: (i, 0)),
                pl.BlockSpec((1, D), lambda i, j: (0, 0)),
                pl.BlockSpec((D, tn), lambda i, j: (0, j))]
    args = [x, norm_w.reshape(1, D), w]
    if rope is not None:
        in_specs += [pl.BlockSpec((tm, LANES), lambda i, j: (i, 0))] * 2
        args += list(tabs)
    return pl.pallas_call(
        functools.partial(_norm_proj_kernel, heads_per_tile=tn // HEAD_DIM,
                          rope_lo=rope_lo, rope_hi=rope_hi),
        grid=(T // tm, N // tn),
        in_specs=in_specs,
        out_specs=pl.BlockSpec((tm, tn), lambda i, j: (i, j)),
        out_shape=jax.ShapeDtypeStruct((T, N), out_dtype),
        scratch_shapes=[pltpu.VMEM((tm, D), BF16)],
        compiler_params=_cparams("parallel", "arbitrary"),
        name="norm_proj",
    )(*args)


def _compress_kernel(kc_ref, vc_ref, kpos_ref, kw1_ref, kw2_ref, vpos_ref, vw1_ref, vw2_ref,
                     kT_ref, v_ref, shift_ref, *, ncp):
    half_blk = CMP_BLOCK // 2

    def comp(t_ref, pos_ref, w1_ref, w2_ref):
        a = jnp.zeros((ncp, HEAD_DIM), F32)
        bm = jnp.zeros((ncp, HEAD_DIM), F32)
        for l in range(half_blk):
            xl = t_ref[pl.ds(l, ncp, stride=CMP_STRIDE), :]
            a = a + _dot((xl + pos_ref[l:l + 1, :]).astype(BF16), w1_ref[l])
            bm = bm + _dot((xl + pos_ref[half_blk + l:half_blk + l + 1, :]).astype(BF16),
                           w1_ref[half_blk + l])
        shift_ref[pl.ds(0, ncp), :] = bm
        shift_ref[pl.ds(ncp, 8), :] = jnp.zeros((8, HEAD_DIM), F32)
        hid = _silu(a + shift_ref[pl.ds(1, ncp), :])
        out = _dot(hid.astype(BF16), w2_ref[...])
        row = lax.broadcasted_iota(jnp.int32, out.shape, 0)
        return jnp.where(row < ncp - 1, out, 0.0)

    kT_ref[0, 0] = comp(kc_ref, kpos_ref, kw1_ref, kw2_ref).T.astype(BF16)
    v_ref[0, 0] = comp(vc_ref, vpos_ref, vw1_ref, vw2_ref).astype(BF16)


def _compress(pf, B, S, kc_blk, vc_blk, ck_pos, ck_w1, ck_w2, cv_pos, cv_w1, cv_w2):
    G = NSA_KV_GROUPS
    ncp = S // CMP_STRIDE
    full2 = lambda b, g: (0, 0)
    full3 = lambda b, g: (0, 0, 0)
    return pl.pallas_call(
        functools.partial(_compress_kernel, ncp=ncp),
        grid=(B, G),
        in_specs=[pl.BlockSpec((S, HEAD_DIM), lambda b, g: (b, kc_blk + g)),
                  pl.BlockSpec((S, HEAD_DIM), lambda b, g: (b, vc_blk + g)),
                  pl.BlockSpec((CMP_BLOCK, HEAD_DIM), full2),
                  pl.BlockSpec((CMP_BLOCK, HEAD_DIM, HEAD_DIM), full3),
                  pl.BlockSpec((HEAD_DIM, HEAD_DIM), full2),
                  pl.BlockSpec((CMP_BLOCK, HEAD_DIM), full2),
                  pl.BlockSpec((CMP_BLOCK, HEAD_DIM, HEAD_DIM), full3),
                  pl.BlockSpec((HEAD_DIM, HEAD_DIM), full2)],
        out_specs=[pl.BlockSpec((1, 1, HEAD_DIM, ncp), lambda b, g: (b, g, 0, 0)),
                   pl.BlockSpec((1, 1, ncp, HEAD_DIM), lambda b, g: (b, g, 0, 0))],
        out_shape=[jax.ShapeDtypeStruct((B, G, HEAD_DIM, ncp), BF16),
                   jax.ShapeDtypeStruct((B, G, ncp, HEAD_DIM), BF16)],
        scratch_shapes=[pltpu.VMEM((ncp + 8, HEAD_DIM), F32)],
        compiler_params=_cparams("parallel", "parallel"),
        name="nsa_compress",
    )(pf, pf, ck_pos, ck_w1.astype(BF16), ck_w2.astype(BF16),
      cv_pos, cv_w1.astype(BF16), cv_w2.astype(BF16))


def _cmp_topk_kernel(q_ref, kT_ref, v_ref, o_ref, bias_ref, *, tq, ncp):
    i = pl.program_id(2)
    s0 = i * tq
    t_col = s0 + lax.broadcasted_iota(jnp.int32, (tq, 1), 0)
    n_row = lax.broadcasted_iota(jnp.int32, (1, ncp), 1)
    mc = jnp.logical_and(n_row * CMP_STRIDE + (CMP_BLOCK - 1) <= t_col, n_row < ncp - 1)
    any_visible = (t_col >= CMP_BLOCK - 1).astype(F32)
    kT = kT_ref[0, 0]
    v = v_ref[0, 0]
    psum = jnp.zeros((tq, ncp), F32)
    for r in range(NSA_REP):
        sl = slice(r * HEAD_DIM, (r + 1) * HEAD_DIM)
        s = jnp.where(mc, _dot(q_ref[:, sl], kT), NEG)
        m = jnp.max(s, axis=-1, keepdims=True)
        e = jnp.exp2((s - m) * EXP2_SCALE)
        l = jnp.sum(e, axis=-1, keepdims=True)
        pc = e * (any_visible / l)
        o_ref[:, sl] = _dot(pc.astype(BF16), v)
        psum = psum + pc

    n_col = lax.broadcasted_iota(jnp.int32, (ncp, LANES), 0) * CMP_STRIDE
    k_row = lax.broadcasted_iota(jnp.int32, (ncp, LANES), 1) * SEL_BLOCK
    ovl = jnp.logical_and(n_col < k_row + SEL_BLOCK, n_col + CMP_BLOCK > k_row).astype(BF16)
    hi = psum.astype(BF16)
    lo = (psum - hi.astype(F32)).astype(BF16)
    imp = _dot(hi, ovl) + _dot(lo, ovl)

    impT = imp.T
    blk = lax.broadcasted_iota(jnp.int32, impT.shape, 0)
    cur = (s0 + lax.broadcasted_iota(jnp.int32, impT.shape, 1)) // SEL_BLOCK
    valid = blk <= cur
    forced = jnp.logical_or(blk == 0, jnp.logical_or(blk == cur, blk == cur - 1))
    sc = jnp.where(valid, jnp.where(forced, BIG, impT), NEG)
    sel = jnp.zeros(impT.shape, F32)
    for _ in range(SEL_TOPK):
        m = jnp.max(sc, axis=0, keepdims=True)
        idx = jnp.min(jnp.where(sc == m, blk, LANES), axis=0, keepdims=True)
        pick = blk == idx
        sel = jnp.where(pick, 1.0, sel)
        sc = jnp.where(pick, BELOW_NEG, sc)
    biasT = jnp.where(sel > 0.0, 0.0, NEG)
    bias_ref[...] = biasT.T.astype(BF16)


def _cmp_topk(pb, kT, vcmp, B, S):
    G = NSA_KV_GROUPS
    T = B * S
    tq = min(S, CMP_TQ)
    nq = S // tq
    ncp = S // CMP_STRIDE
    gw = NSA_REP * HEAD_DIM
    return pl.pallas_call(
        functools.partial(_cmp_topk_kernel, tq=tq, ncp=ncp),
        grid=(B, G, nq),
        in_specs=[pl.BlockSpec((tq, gw), lambda b, g, i: (b * nq + i, g)),
                  pl.BlockSpec((1, 1, HEAD_DIM, ncp), lambda b, g, i: (b, g, 0, 0)),
                  pl.BlockSpec((1, 1, ncp, HEAD_DIM), lambda b, g, i: (b, g, 0, 0))],
        out_specs=[pl.BlockSpec((tq, gw), lambda b, g, i: (b * nq + i, g)),
                   pl.BlockSpec((tq, LANES), lambda b, g, i: (b * nq + i, g))],
        out_shape=[jax.ShapeDtypeStruct((T, NSA_HEADS * HEAD_DIM), F32),
                   jax.ShapeDtypeStruct((T, G * LANES), BF16)],
        compiler_params=_cparams("parallel", "parallel", "parallel"),
        name="nsa_cmp_topk",
    )(pb, kT, vcmp)


def _win_kernel(q_ref, k0_ref, k1_ref, k2_ref, v0_ref, v1_ref, v2_ref, o_ref, *, tq):
    i = pl.program_id(2)
    t_col = i * tq + lax.broadcasted_iota(jnp.int32, (tq, 1), 0)
    kp = (i - 2) * tq + lax.broadcasted_iota(jnp.int32, (1, 3 * tq), 1)
    mw = jnp.logical_and(jnp.logical_and(kp <= t_col, kp > t_col - WINDOW), kp >= 0)
    k = jnp.concatenate([k0_ref[...], k1_ref[...], k2_ref[...]], axis=0)
    v = jnp.concatenate([v0_ref[...], v1_ref[...], v2_ref[...]], axis=0)
    for r in range(NSA_REP):
        sl = slice(r * HEAD_DIM, (r + 1) * HEAD_DIM)
        s = jnp.where(mw, _dot_nt(q_ref[:, sl], k), NEG)
        m = jnp.max(s, axis=-1, keepdims=True)
        e = jnp.exp2((s - m) * EXP2_SCALE)
        l = jnp.sum(e, axis=-1, keepdims=True)
        o_ref[:, sl] = _dot(e.astype(BF16), v) / l


def _window(pb, B, S, k_blk, v_blk):
    G = NSA_KV_GROUPS
    T = B * S
    tq = min(S // 2, WIN_TQ)
    assert WINDOW <= 2 * tq
    nq = S // tq
    gw = NSA_REP * HEAD_DIM

    def kv_spec(col0, back):
        return pl.BlockSpec((tq, HEAD_DIM),
                            lambda b, g, i: (b * nq + jnp.maximum(i - back, 0), col0 + g))

    return pl.pallas_call(
        functools.partial(_win_kernel, tq=tq),
        grid=(B, G, nq),
        in_specs=[pl.BlockSpec((tq, gw), lambda b, g, i: (b * nq + i, g)),
                  kv_spec(k_blk, 2), kv_spec(k_blk, 1), kv_spec(k_blk, 0),
                  kv_spec(v_blk, 2), kv_spec(v_blk, 1), kv_spec(v_blk, 0)],
        out_specs=pl.BlockSpec((tq, gw), lambda b, g, i: (b * nq + i, g)),
        out_shape=jax.ShapeDtypeStruct((T, NSA_HEADS * HEAD_DIM), F32),
        compiler_params=_cparams("parallel", "parallel", "parallel"),
        name="nsa_window",
    )(pb, pb, pb, pb, pb, pb, pb)


def _sel_kernel(q_ref, bias_ref, k_ref, v_ref, o_ref, qa_ref, m_ref, l_ref, acc_ref, *, tq, tk):
    i = pl.program_id(2)
    j = pl.program_id(3)
    last_j = (i * tq + tq - 1) // tk

    @pl.when(j == 0)
    def _():
        for r in range(NSA_REP):
            rows = slice(r * tq, (r + 1) * tq)
            qa_ref[rows, 0:HEAD_DIM] = q_ref[:, r * HEAD_DIM:(r + 1) * HEAD_DIM]
            qa_ref[rows, HEAD_DIM:2 * HEAD_DIM] = bias_ref[...]
        m_ref[...] = jnp.full(m_ref.shape, NEG, F32)
        l_ref[...] = jnp.zeros(l_ref.shape, F32)
        acc_ref[...] = jnp.zeros(acc_ref.shape, F32)

    def step(causal):
        key = j * tk + lax.broadcasted_iota(jnp.int32, (tk, LANES), 0)
        blk = lax.broadcasted_iota(jnp.int32, (tk, LANES), 1)
        onehot = (key // SEL_BLOCK == blk).astype(BF16)
        ka = jnp.concatenate([k_ref[...], onehot], axis=1)
        s = _dot_nt(qa_ref[...], ka)
        if causal:
            row = lax.broadcasted_iota(jnp.int32, (NSA_REP * tq, 1), 0)
            t_col = i * tq + row % tq
            kp = j * tk + lax.broadcasted_iota(jnp.int32, (1, tk), 1)
            s = jnp.where(kp <= t_col, s, NEG)
        m_prev = m_ref[...]
        m_new = jnp.maximum(m_prev, jnp.max(s, axis=-1, keepdims=True))
        alpha = jnp.exp2((m_prev - m_new) * EXP2_SCALE)
        p = jnp.exp2((s - m_new) * EXP2_SCALE)
        l_ref[...] = alpha * l_ref[...] + jnp.sum(p, axis=-1, keepdims=True)
        acc_ref[...] = alpha * acc_ref[...] + _dot(p.astype(BF16), v_ref[...])
        m_ref[...] = m_new

    crosses_diag = j * tk + tk - 1 > i * tq

    @pl.when(jnp.logical_and(j <= last_j, crosses_diag))
    def _():
        step(True)

    @pl.when(jnp.logical_and(j <= last_j, jnp.logical_not(crosses_diag)))
    def _():
        step(False)

    @pl.when(j == pl.num_programs(3) - 1)
    def _():
        for r in range(NSA_REP):
            rows = slice(r * tq, (r + 1) * tq)
            o_ref[:, r * HEAD_DIM:(r + 1) * HEAD_DIM] = acc_ref[rows, :] / l_ref[rows, :]


def _selection(pb, bias, B, S, k_blk, v_blk):
    G = NSA_KV_GROUPS
    T = B * S
    tq = min(S, SEL_TQ)
    tk = min(S, SEL_TK)
    nq = S // tq
    nk = S // tk
    gw = NSA_REP * HEAD_DIM

    def kv_spec(col0):
        return pl.BlockSpec(
            (tk, HEAD_DIM),
            lambda b, g, i, j: (b * nk + jnp.minimum(j, (i * tq + tq - 1) // tk), col0 + g))

    return pl.pallas_call(
        functools.partial(_sel_kernel, tq=tq, tk=tk),
        grid=(B, G, nq, nk),
        in_specs=[pl.BlockSpec((tq, gw), lambda b, g, i, j: (b * nq + i, g)),
                  pl.BlockSpec((tq, LANES), lambda b, g, i, j: (b * nq + i, g)),
                  kv_spec(k_blk), kv_spec(v_blk)],
        out_specs=pl.BlockSpec((tq, gw), lambda b, g, i, j: (b * nq + i, g)),
        out_shape=jax.ShapeDtypeStruct((T, NSA_HEADS * HEAD_DIM), F32),
        scratch_shapes=[pltpu.VMEM((NSA_REP * tq, 2 * HEAD_DIM), BF16),
                        pltpu.VMEM((NSA_REP * tq, 1), F32),
                        pltpu.VMEM((NSA_REP * tq, 1), F32),
                        pltpu.VMEM((NSA_REP * tq, HEAD_DIM), F32)],
        compiler_params=_cparams("parallel", "parallel", "parallel", "arbitrary"),
        name="nsa_selection",
    )(pb, bias, pb, pb)


def _nsa_out_kernel(x_ref, oc_ref, os_ref, ow_ref, z_ref, gl_ref, w_ref, o_ref, og_ref):
    sig = jax.nn.sigmoid(gl_ref[...])
    H = NSA_HEADS
    for h in range(H):
        sl = slice(h * HEAD_DIM, (h + 1) * HEAD_DIM)
        o = (sig[:, h:h + 1] * oc_ref[:, sl] + sig[:, H + h:H + h + 1] * os_ref[:, sl]
             + sig[:, 2 * H + h:2 * H + h + 1] * ow_ref[:, sl])
        og_ref[:, sl] = (o * _silu(z_ref[:, sl])).astype(BF16)
    o_ref[...] = x_ref[...] + _dot(og_ref[...], w_ref[...])


def _nsa_out(x, o_cmp, o_sel, o_win, pf, z_blk, gl_blk, w_out):
    T, D = x.shape
    W = o_cmp.shape[1]
    tm = min(T, OUT_TM)
    row = lambda i: (i, 0)
    return pl.pallas_call(
        _nsa_out_kernel,
        grid=(T // tm,),
        in_specs=[pl.BlockSpec((tm, D), row),
                  pl.BlockSpec((tm, W), row), pl.BlockSpec((tm, W), row), pl.BlockSpec((tm, W), row),
                  pl.BlockSpec((tm, W), lambda i: (i, z_blk)),
                  pl.BlockSpec((tm, LANES), lambda i: (i, gl_blk)),
                  pl.BlockSpec((W, D), lambda i: (0, 0))],
        out_specs=pl.BlockSpec((tm, D), row),
        out_shape=jax.ShapeDtypeStruct((T, D), F32),
        scratch_shapes=[pltpu.VMEM((tm, W), BF16)],
        compiler_params=_cparams("parallel"),
        name="nsa_out",
    )(x, o_cmp, o_sel, o_win, pf, pf, w_out)


def _nsa_layer(x, B, S, tabs, norm_w, w_in, ck_pos, ck_w1, ck_w2, cv_pos, cv_w1, cv_w2, w_out):
    H, G, d = NSA_HEADS, NSA_KV_GROUPS, HEAD_DIM
    kvw = G * d
    c = [0, H * d]
    for _ in range(6):
        c.append(c[-1] + kvw)
    c.append(c[-1] + 3 * H)
    c.append(c[-1] + H * d)
    seg = lambda n: w_in[:, c[n]:c[n + 1]]
    wq, wkc, wvc, wks, wvs, wkw, wvw, wgl, wz = [seg(n) for n in range(9)]
    w_b = jnp.concatenate([wq, wks, wkw, wvs, wvw], axis=1).astype(BF16)
    w_f = jnp.concatenate([wz, wkc, wvc, wgl, jnp.zeros((w_in.shape[0], LANES - 3 * H), w_in.dtype)],
                          axis=1).astype(BF16)
    pb = _norm_proj(x, norm_w, w_b, BF16, 1024, rope=(0, H + 2 * G), tabs=tabs)
    pf = _norm_proj(x, norm_w, w_f, F32, 640, rope=(H, H + G), tabs=tabs)
    ks_blk, kw_blk, vs_blk, vw_blk = H, H + G, H + 2 * G, H + 3 * G
    kc_blk, vc_blk, gl_blk = H, H + G, H + 2 * G

    kT, vcmp = _compress(pf, B, S, kc_blk, vc_blk, ck_pos, ck_w1, ck_w2, cv_pos, cv_w1, cv_w2)
    o_cmp, bias = _cmp_topk(pb, kT, vcmp, B, S)
    o_win = _window(pb, B, S, kw_blk, vw_blk)
    o_sel = _selection(pb, bias, B, S, ks_blk, vs_blk)
    return _nsa_out(x, o_cmp, o_sel, o_win, pf, 0, gl_blk, w_out.astype(BF16))


def _hgrn_kernel(q_ref, f_ref, i_ref, z_ref, lbl_ref, gw_ref, o_ref,
                 st_ref, b_ref, qs_ref, k_ref, oc_ref, *, C, sub, layer):
    n = pl.program_id(2)

    @pl.when(n == 0)
    def _():
        st_ref[...] = jnp.zeros(st_ref.shape, F32)

    lg = lbl_ref[...]
    e = jnp.exp(lg - jnp.max(lg, axis=0, keepdims=True))
    p = e / jnp.sum(e, axis=0, keepdims=True)
    lb = jnp.sum(p[0:layer + 1, :], axis=0, keepdims=True) - p[0:1, :]

    fl = f_ref[...]
    log_sig = jnp.minimum(fl, 0.0) - jnp.log1p(jnp.exp(-jnp.abs(fl)))
    la = jnp.log(lb)
    lc = jnp.log1p(-lb) + log_sig
    g = jnp.maximum(la, lc) + jnp.log1p(jnp.exp(-jnp.abs(la - lc)))
    k_ref[...] = (1.0 - lb) * jax.nn.sigmoid(-fl)
    qs_ref[...] = _silu(q_ref[...])

    r0 = lax.broadcasted_iota(jnp.int32, (C, C), 0)
    r1 = lax.broadcasted_iota(jnp.int32, (C, C), 1)
    tri = (r0 >= r1).astype(BF16)
    g1 = g.astype(BF16)
    rem = g - g1.astype(F32)
    g2 = rem.astype(BF16)
    g3 = (rem - g2.astype(F32)).astype(BF16)
    b = _dot(tri, g1) + _dot(tri, g2) + _dot(tri, g3)
    b_ref[...] = b

    iv = i_ref[...]
    ib = iv.astype(BF16)
    stT = st_ref[...]
    oc_ref[...] = _dot_nt((qs_ref[...] * jnp.exp(b)).astype(BF16), stT.astype(BF16))

    trow = lax.broadcasted_iota(jnp.int32, (sub, HEAD_DIM), 0)
    for blk in range(C // sub):
        lo = blk * sub
        rows = pl.ds(lo, sub)
        b_blk = b_ref[rows, :]
        q_blk = qs_ref[rows, :]
        o_blk = oc_ref[rows, :]
        if blk > 0:
            bref = b_ref[pl.ds(lo - 1, 1), :]
            qt = (q_blk * jnp.exp(b_blk - bref)).astype(BF16)
            kt = (k_ref[pl.ds(0, lo), :] * jnp.exp(bref - b_ref[pl.ds(0, lo), :])).astype(BF16)
            att = _dot_nt(qt, kt)
            o_blk = o_blk + _dot(att.astype(BF16), ib[0:lo, :])
        for s in range(sub):
            bs = b_ref[pl.ds(lo + s, 1), :]
            w = jnp.where(trow >= s, jnp.exp(b_blk - bs), 0.0)
            col = jnp.sum(q_blk * w * k_ref[pl.ds(lo + s, 1), :], axis=-1, keepdims=True)
            o_blk = o_blk + col * i_ref[pl.ds(lo + s, 1), :]
        oc_ref[rows, :] = o_blk

    b_last = b_ref[pl.ds(C - 1, 1), :]
    kd = (k_ref[...] * jnp.exp(b_last - b)).astype(BF16)
    st_ref[...] = stT * jnp.exp(b_last) + _dot(iv.T.astype(BF16), kd)

    o = oc_ref[...]
    o = o * lax.rsqrt(jnp.mean(o * o, axis=-1, keepdims=True) + RMS_EPS) * gw_ref[...]
    o_ref[...] = (o * _silu(z_ref[...])).astype(BF16)


def _hgrn_recurrence(ph, lb_logits, gnorm_w, B, S, layer):
    H = HGRN_HEADS
    T = B * S
    C = min(S, HGRN_CHUNK)
    nc = S // C
    NL = lb_logits.shape[0]

    def col(c0):
        return pl.BlockSpec((C, HEAD_DIM), lambda b, h, n: (b * nc + n, c0 + h))

    return pl.pallas_call(
        functools.partial(_hgrn_kernel, C=C, sub=HGRN_SUB, layer=layer),
        grid=(B, H, nc),
        in_specs=[col(0), col(H), col(2 * H), col(3 * H),
                  pl.BlockSpec((NL, HEAD_DIM), lambda b, h, n: (0, h)),
                  pl.BlockSpec((1, HEAD_DIM), lambda b, h, n: (0, 0))],
        out_specs=pl.BlockSpec((C, HEAD_DIM), lambda b, h, n: (b * nc + n, h)),
        out_shape=jax.ShapeDtypeStruct((T, H * HGRN_I_DIM), BF16),
        scratch_shapes=[pltpu.VMEM((HGRN_I_DIM, HGRN_F_DIM), F32),
                        pltpu.VMEM((C, HEAD_DIM), F32),
                        pltpu.VMEM((C, HEAD_DIM), F32),
                        pltpu.VMEM((C, HEAD_DIM), F32),
                        pltpu.VMEM((C, HEAD_DIM), F32)],
        compiler_params=_cparams("parallel", "parallel", "arbitrary"),
        name="hgrn_recurrence",
    )(ph, ph, ph, ph, lb_logits, gnorm_w.reshape(1, HEAD_DIM))


def _hgrn_out_kernel(x_ref, og_ref, w_ref, fw_ref, o_ref, *, final_norm):
    y = x_ref[...] + _dot(og_ref[...], w_ref[...])
    if final_norm:
        ms = jnp.mean(y * y, axis=-1, keepdims=True)
        y = y * lax.rsqrt(ms + RMS_EPS) * fw_ref[...]
    o_ref[...] = y


def _hgrn_out(x, og, w_out, final_w, final_norm):
    T, D = x.shape
    W = og.shape[1]
    tm = min(T, 2 * OUT_TM)
    row = lambda i: (i, 0)
    return pl.pallas_call(
        functools.partial(_hgrn_out_kernel, final_norm=final_norm),
        grid=(T // tm,),
        in_specs=[pl.BlockSpec((tm, D), row), pl.BlockSpec((tm, W), row),
                  pl.BlockSpec((W, D), lambda i: (0, 0)),
                  pl.BlockSpec((1, D), lambda i: (0, 0))],
        out_specs=pl.BlockSpec((tm, D), row),
        out_shape=jax.ShapeDtypeStruct((T, D), F32),
        compiler_params=_cparams("parallel"),
        name="hgrn_out",
    )(x, og, w_out, final_w.reshape(1, D))


def _hgrn_layer(x, B, S, norm_w, w_in, lb_logits, gnorm_w, w_out, layer, final_w, final_norm):
    ph = _norm_proj(x, norm_w, w_in.astype(BF16), F32, 1024)
    og = _hgrn_recurrence(ph, lb_logits, gnorm_w, B, S, layer)
    return _hgrn_out(x, og, w_out.astype(BF16), final_w, final_norm)


def kernel(x, positions, norm_w, final_norm_w, nsa_w_in, nsa_ck_pos, nsa_ck_w1, nsa_ck_w2,
           nsa_cv_pos, nsa_cv_w1, nsa_cv_w2, nsa_w_out, hgrn_w_in, hgrn_lb_logits,
           hgrn_gnorm_w, hgrn_w_out):
    B, S, D = x.shape
    depth = norm_w.shape[0]
    xf = x.reshape(B * S, D)
    tabs = _rope_tables(positions)
    for layer in range(depth):
        j = layer // 2
        if layer % 2 == 0:
            xf = _nsa_layer(xf, B, S, tabs, norm_w[layer], nsa_w_in[j], nsa_ck_pos[j], nsa_ck_w1[j],
                            nsa_ck_w2[j], nsa_cv_pos[j], nsa_cv_w1[j], nsa_cv_w2[j], nsa_w_out[j])
        else:
            xf = _hgrn_layer(xf, B, S, norm_w[layer], hgrn_w_in[j], hgrn_lb_logits, hgrn_gnorm_w[j],
                             hgrn_w_out[j], j, final_norm_w, layer == depth - 1)
    return xf.reshape(B, S, D)
```

```python
import functools
import math

import jax
import jax.numpy as jnp
from jax import lax
from jax.experimental import pallas as pl
from jax.experimental.pallas import tpu as pltpu

F32 = jnp.float32
BF16 = jnp.bfloat16

RMS_EPS = 1e-6
NEG = -1e30
BIG = 1e30
BELOW_NEG = -3e38
HEAD_DIM = 128
NSA_HEADS = 16
NSA_KV_GROUPS = 4
NSA_REP = NSA_HEADS // NSA_KV_GROUPS
CMP_BLOCK = 32
CMP_STRIDE = 16
SEL_BLOCK = 64
SEL_TOPK = 16
WINDOW = 512
ROPE_THETA = 500000.0
ROPE_DIM = HEAD_DIM // 4
HGRN_HEADS = 16
HGRN_F_DIM = 128
HGRN_I_DIM = 128

LANES = 128
VMEM_LIMIT_BYTES = 56 * 1024 * 1024

ATTN_SCALE = HEAD_DIM ** -0.5
EXP2_SCALE = ATTN_SCALE * math.log2(math.e)

PROJ_TM = 1024
CMP_TQ = 128
WIN_TQ = 256
SEL_TQ = 256
SEL_TK = 1024
OUT_TM = 256
HGRN_CHUNK = 128
HGRN_SUB = 16
HGRN_HEADS_PER_STEP = 4


def _cparams(*sem):
    return pltpu.CompilerParams(dimension_semantics=sem, vmem_limit_bytes=VMEM_LIMIT_BYTES)


def _dot(a, b):
    return jnp.dot(a, b, preferred_element_type=F32)


def _dot_nt(a, b):
    return lax.dot_general(a, b, (((1,), (1,)), ((), ())), preferred_element_type=F32)


def _silu(v):
    return v * jax.nn.sigmoid(v)


def _rope_table_kernel(pos_ref, inv_ref, cos_ref, sin_ref):
    ang = pos_ref[...] * inv_ref[...]
    lane = lax.broadcasted_iota(jnp.int32, ang.shape, 1)
    c = jnp.cos(ang)
    s = jnp.sin(ang)
    cos_ref[...] = jnp.where(lane < ROPE_DIM, c, 1.0)
    sin_ref[...] = jnp.where(lane < ROPE_DIM // 2, -s, jnp.where(lane < ROPE_DIM, s, 0.0))


def _rope_tables(positions):
    T = positions.size
    tm = min(T, 2048)
    half = ROPE_DIM // 2
    inv = ROPE_THETA ** (-jnp.arange(half, dtype=F32) / half)
    inv_pat = jnp.concatenate([inv, inv, jnp.zeros((LANES - ROPE_DIM,), F32)]).reshape(1, LANES)
    pos = positions.astype(F32).reshape(T, 1)
    return pl.pallas_call(
        _rope_table_kernel,
        grid=(T // tm,),
        in_specs=[pl.BlockSpec((tm, 1), lambda i: (i, 0)),
                  pl.BlockSpec((1, LANES), lambda i: (0, 0))],
        out_specs=[pl.BlockSpec((tm, LANES), lambda i: (i, 0))] * 2,
        out_shape=[jax.ShapeDtypeStruct((T, LANES), F32)] * 2,
        compiler_params=_cparams("parallel"),
        name="rope_tables",
    )(pos, inv_pat)


def _norm_proj_kernel(*refs, heads_per_tile, rope_lo, rope_hi, q_heads):
    if rope_hi > rope_lo:
        x_ref, nw_ref, w_ref, cos_ref, sin_ref, o_ref, hn_ref = refs
    else:
        x_ref, nw_ref, w_ref, o_ref, hn_ref = refs
    j = pl.program_id(1)

    @pl.when(j == 0)
    def _():
        x = x_ref[...]
        ms = jnp.mean(x * x, axis=-1, keepdims=True)
        hn_ref[...] = (x * lax.rsqrt(ms + RMS_EPS) * nw_ref[...]).astype(BF16)

    acc = _dot(hn_ref[...], w_ref[...])
    if rope_hi <= rope_lo:
        o_ref[...] = acc.astype(o_ref.dtype)
        return

    half = ROPE_DIM // 2
    for h in range(heads_per_tile):
        sl = slice(h * HEAD_DIM, (h + 1) * HEAD_DIM)
        head = j * heads_per_tile + h
        is_rope = jnp.logical_and(head >= rope_lo, head < rope_hi)

        @pl.when(is_rope)
        def _(sl=sl):
            t = acc[:, sl]
            lane = lax.broadcasted_iota(jnp.int32, t.shape, 1)
            sw = jnp.where(lane < half,
                           pltpu.roll(t, HEAD_DIM - half, axis=1),
                           pltpu.roll(t, half, axis=1))
            factor = jnp.where(head < q_heads, EXP2_SCALE, 1.0).astype(F32)
            o_ref[:, sl] = ((t * cos_ref[...] + sw * sin_ref[...]) * factor).astype(o_ref.dtype)

        @pl.when(jnp.logical_not(is_rope))
        def _(sl=sl):
            o_ref[:, sl] = acc[:, sl].astype(o_ref.dtype)


def _norm_proj(x, norm_w, w, out_dtype, tn, rope=None, tabs=None, q_heads=0):
    T, D = x.shape
    N = w.shape[1]
    tm = min(T, PROJ_TM)
    rope_lo, rope_hi = rope if rope is not None else (0, 0)
    in_specs = [pl.BlockSpec((tm, D), lambda i, j: (i, 0)),
                pl.BlockSpec((1, D), lambda i, j: (0, 0)),
                pl.BlockSpec((D, tn), lambda i, j: (0, j))]
    args = [x, norm_w.reshape(1, D), w]
    if rope is not None:
        in_specs += [pl.BlockSpec((tm, LANES), lambda i, j: (i, 0))] * 2
        args += list(tabs)
    return pl.pallas_call(
        functools.partial(_norm_proj_kernel, heads_per_tile=tn // HEAD_DIM,
                          rope_lo=rope_lo, rope_hi=rope_hi, q_heads=q_heads),
        grid=(T // tm, N // tn),
        in_specs=in_specs,
        out_specs=pl.BlockSpec((tm, tn), lambda i, j: (i, j)),
        out_shape=jax.ShapeDtypeStruct((T, N), out_dtype),
        scratch_shapes=[pltpu.VMEM((tm, D), BF16)],
        compiler_params=_cparams("parallel", "arbitrary"),
        name="norm_proj",
    )(*args)


def _compress_kernel(kc_ref, vc_ref, kpos_ref, kw1_ref, kw2_ref, vpos_ref, vw1_ref, vw2_ref,
                     kT_ref, v_ref, shift_ref, *, ncp):
    half_blk = CMP_BLOCK // 2

    def comp(t_ref, pos_ref, w1_ref, w2_ref):
        a = jnp.zeros((ncp, HEAD_DIM), F32)
        bm = jnp.zeros((ncp, HEAD_DIM), F32)
        for l in range(half_blk):
            xl = t_ref[pl.ds(l, ncp, stride=CMP_STRIDE), :]
            a = a + _dot((xl + pos_ref[l:l + 1, :]).astype(BF16), w1_ref[l])
            bm = bm + _dot((xl + pos_ref[half_blk + l:half_blk + l + 1, :]).astype(BF16),
                           w1_ref[half_blk + l])
        shift_ref[pl.ds(0, ncp), :] = bm
        shift_ref[pl.ds(ncp, 8), :] = jnp.zeros((8, HEAD_DIM), F32)
        hid = _silu(a + shift_ref[pl.ds(1, ncp), :])
        out = _dot(hid.astype(BF16), w2_ref[...])
        row = lax.broadcasted_iota(jnp.int32, out.shape, 0)
        return jnp.where(row < ncp - 1, out, 0.0)

    kT_ref[0, 0] = comp(kc_ref, kpos_ref, kw1_ref, kw2_ref).T.astype(BF16)
    v_ref[0, 0] = comp(vc_ref, vpos_ref, vw1_ref, vw2_ref).astype(BF16)


def _compress(pf, B, S, kc_blk, vc_blk, ck_pos, ck_w1, ck_w2, cv_pos, cv_w1, cv_w2):
    G = NSA_KV_GROUPS
    ncp = S // CMP_STRIDE
    full2 = lambda b, g: (0, 0)
    full3 = lambda b, g: (0, 0, 0)
    return pl.pallas_call(
        functools.partial(_compress_kernel, ncp=ncp),
        grid=(B, G),
        in_specs=[pl.BlockSpec((S, HEAD_DIM), lambda b, g: (b, kc_blk + g)),
                  pl.BlockSpec((S, HEAD_DIM), lambda b, g: (b, vc_blk + g)),
                  pl.BlockSpec((CMP_BLOCK, HEAD_DIM), full2),
                  pl.BlockSpec((CMP_BLOCK, HEAD_DIM, HEAD_DIM), full3),
                  pl.BlockSpec((HEAD_DIM, HEAD_DIM), full2),
                  pl.BlockSpec((CMP_BLOCK, HEAD_DIM), full2),
                  pl.BlockSpec((CMP_BLOCK, HEAD_DIM, HEAD_DIM), full3),
                  pl.BlockSpec((HEAD_DIM, HEAD_DIM), full2)],
        out_specs=[pl.BlockSpec((1, 1, HEAD_DIM, ncp), lambda b, g: (b, g, 0, 0)),
                   pl.BlockSpec((1, 1, ncp, HEAD_DIM), lambda b, g: (b, g, 0, 0))],
        out_shape=[jax.ShapeDtypeStruct((B, G, HEAD_DIM, ncp), BF16),
                   jax.ShapeDtypeStruct((B, G, ncp, HEAD_DIM), BF16)],
        scratch_shapes=[pltpu.VMEM((ncp + 8, HEAD_DIM), F32)],
        compiler_params=_cparams("parallel", "parallel"),
        name="nsa_compress",
    )(pf, pf, ck_pos, ck_w1.astype(BF16), ck_w2.astype(BF16),
      cv_pos, cv_w1.astype(BF16), cv_w2.astype(BF16))


def _cmp_topk_kernel(q_ref, kT_ref, v_ref, o_ref, bias_ref, *, tq, ncp):
    i = pl.program_id(2)
    s0 = i * tq
    t_col = s0 + lax.broadcasted_iota(jnp.int32, (tq, 1), 0)
    n_row = lax.broadcasted_iota(jnp.int32, (1, ncp), 1)
    mc = jnp.logical_and(n_row * CMP_STRIDE + (CMP_BLOCK - 1) <= t_col, n_row < ncp - 1)
    any_visible = (t_col >= CMP_BLOCK - 1).astype(F32)
    kT = kT_ref[0, 0]
    v = v_ref[0, 0]
    psum = jnp.zeros((tq, ncp), F32)
    for r in range(NSA_REP):
        sl = slice(r * HEAD_DIM, (r + 1) * HEAD_DIM)
        s = jnp.where(mc, _dot(q_ref[:, sl], kT), NEG)
        m = jnp.max(s, axis=-1, keepdims=True)
        e = jnp.exp2(s - m)
        l = jnp.sum(e, axis=-1, keepdims=True)
        pc = e * (any_visible / l)
        o_ref[:, sl] = _dot(pc.astype(BF16), v)
        psum = psum + pc

    n_col = lax.broadcasted_iota(jnp.int32, (ncp, LANES), 0) * CMP_STRIDE
    k_row = lax.broadcasted_iota(jnp.int32, (ncp, LANES), 1) * SEL_BLOCK
    ovl = jnp.logical_and(n_col < k_row + SEL_BLOCK, n_col + CMP_BLOCK > k_row).astype(BF16)
    hi = psum.astype(BF16)
    lo = (psum - hi.astype(F32)).astype(BF16)
    imp = _dot(hi, ovl) + _dot(lo, ovl)

    impT = imp.T
    blk = lax.broadcasted_iota(jnp.int32, impT.shape, 0)
    cur = (s0 + lax.broadcasted_iota(jnp.int32, impT.shape, 1)) // SEL_BLOCK
    valid = blk <= cur
    forced = jnp.logical_or(blk == 0, jnp.logical_or(blk == cur, blk == cur - 1))
    sc = jnp.where(valid, jnp.where(forced, BIG, impT), NEG)
    sel = jnp.zeros(impT.shape, F32)
    for _ in range(SEL_TOPK):
        m = jnp.max(sc, axis=0, keepdims=True)
        idx = jnp.min(jnp.where(sc == m, blk, LANES), axis=0, keepdims=True)
        pick = blk == idx
        sel = jnp.where(pick, 1.0, sel)
        sc = jnp.where(pick, BELOW_NEG, sc)
    biasT = jnp.where(sel > 0.0, 0.0, NEG)
    bias_ref[...] = biasT.T.astype(BF16)


def _cmp_topk(pb, kT, vcmp, B, S):
    G = NSA_KV_GROUPS
    T = B * S
    tq = min(S, CMP_TQ)
    nq = S // tq
    ncp = S // CMP_STRIDE
    gw = NSA_REP * HEAD_DIM
    return pl.pallas_call(
        functools.partial(_cmp_topk_kernel, tq=tq, ncp=ncp),
        grid=(B, G, nq),
        in_specs=[pl.BlockSpec((tq, gw), lambda b, g, i: (b * nq + i, g)),
                  pl.BlockSpec((1, 1, HEAD_DIM, ncp), lambda b, g, i: (b, g, 0, 0)),
                  pl.BlockSpec((1, 1, ncp, HEAD_DIM), lambda b, g, i: (b, g, 0, 0))],
        out_specs=[pl.BlockSpec((tq, gw), lambda b, g, i: (b * nq + i, g)),
                   pl.BlockSpec((tq, LANES), lambda b, g, i: (b * nq + i, g))],
        out_shape=[jax.ShapeDtypeStruct((T, NSA_HEADS * HEAD_DIM), F32),
                   jax.ShapeDtypeStruct((T, G * LANES), BF16)],
        compiler_params=_cparams("parallel", "parallel", "parallel"),
        name="nsa_cmp_topk",
    )(pb, kT, vcmp)


def _win_kernel(q_ref, k0_ref, k1_ref, k2_ref, v0_ref, v1_ref, v2_ref, o_ref, *, tq):
    i = pl.program_id(2)
    t_col = i * tq + lax.broadcasted_iota(jnp.int32, (tq, 1), 0)
    kp = (i - 2) * tq + lax.broadcasted_iota(jnp.int32, (1, 3 * tq), 1)
    mw = jnp.logical_and(jnp.logical_and(kp <= t_col, kp > t_col - WINDOW), kp >= 0)
    k = jnp.concatenate([k0_ref[...], k1_ref[...], k2_ref[...]], axis=0)
    v = jnp.concatenate([v0_ref[...], v1_ref[...], v2_ref[...]], axis=0)
    for r in range(NSA_REP):
        sl = slice(r * HEAD_DIM, (r + 1) * HEAD_DIM)
        s = jnp.where(mw, _dot_nt(q_ref[:, sl], k), NEG)
        m = jnp.max(s, axis=-1, keepdims=True)
        e = jnp.exp2(s - m)
        l = jnp.sum(e, axis=-1, keepdims=True)
        o_ref[:, sl] = _dot(e.astype(BF16), v) / l


def _window(pb, B, S, k_blk, v_blk):
    G = NSA_KV_GROUPS
    T = B * S
    tq = min(S // 2, WIN_TQ)
    assert WINDOW <= 2 * tq
    nq = S // tq
    gw = NSA_REP * HEAD_DIM

    def kv_spec(col0, back):
        return pl.BlockSpec((tq, HEAD_DIM),
                            lambda b, g, i: (b * nq + jnp.maximum(i - back, 0), col0 + g))

    return pl.pallas_call(
        functools.partial(_win_kernel, tq=tq),
        grid=(B, G, nq),
        in_specs=[pl.BlockSpec((tq, gw), lambda b, g, i: (b * nq + i, g)),
                  kv_spec(k_blk, 2), kv_spec(k_blk, 1), kv_spec(k_blk, 0),
                  kv_spec(v_blk, 2), kv_spec(v_blk, 1), kv_spec(v_blk, 0)],
        out_specs=pl.BlockSpec((tq, gw), lambda b, g, i: (b * nq + i, g)),
        out_shape=jax.ShapeDtypeStruct((T, NSA_HEADS * HEAD_DIM), F32),
        compiler_params=_cparams("parallel", "parallel", "parallel"),
        name="nsa_window",
    )(pb, pb, pb, pb, pb, pb, pb)


def _sel_kernel(qi_ref, kj_ref, q_ref, bias_ref, k_ref, v_ref, o_ref, qa_ref, m_ref, acc_ref, *, tq, tk):
    n = pl.program_id(2)
    i = qi_ref[n]
    j = kj_ref[n]
    last_j = (i * tq + tq - 1) // tk

    @pl.when(j == 0)
    def _():
        for r in range(NSA_REP):
            rows = slice(r * tq, (r + 1) * tq)
            qa_ref[rows, 0:HEAD_DIM] = q_ref[:, r * HEAD_DIM:(r + 1) * HEAD_DIM]
            qa_ref[rows, HEAD_DIM:2 * HEAD_DIM] = bias_ref[...]
        m_ref[...] = jnp.full(m_ref.shape, NEG, F32)
        acc_ref[...] = jnp.zeros(acc_ref.shape, F32)

    def step(causal):
        key = j * tk + lax.broadcasted_iota(jnp.int32, (tk, LANES), 0)
        blk = lax.broadcasted_iota(jnp.int32, (tk, LANES), 1)
        onehot = (key // SEL_BLOCK == blk).astype(BF16)
        ka = jnp.concatenate([k_ref[...], onehot], axis=1)
        va = jnp.concatenate([v_ref[...], jnp.ones((tk, LANES), BF16)], axis=1)
        if causal:
            t_col = i * tq + lax.broadcasted_iota(jnp.int32, (tq, 1), 0)
            kp = j * tk + lax.broadcasted_iota(jnp.int32, (1, tk), 1)
            visible = kp <= t_col
        for r in range(NSA_REP):
            rows = slice(r * tq, (r + 1) * tq)
            s = _dot_nt(qa_ref[rows, :], ka)
            if causal:
                s = jnp.where(visible, s, NEG)
            m_prev = m_ref[rows, :]
            m_new = jnp.maximum(m_prev, jnp.max(s, axis=-1, keepdims=True))
            alpha = jnp.exp2(m_prev - m_new)
            p = jnp.exp2(s - jnp.tile(m_new, (1, tk // LANES)))
            acc_ref[rows, :] = jnp.tile(alpha, (1, 2)) * acc_ref[rows, :] + _dot(p.astype(BF16), va)
            m_ref[rows, :] = m_new

    @pl.when(j < last_j)
    def _():
        step(False)

    @pl.when(j == last_j)
    def _():
        step(True)
        for r in range(NSA_REP):
            rows = slice(r * tq, (r + 1) * tq)
            o_ref[:, r * HEAD_DIM:(r + 1) * HEAD_DIM] = (acc_ref[rows, 0:HEAD_DIM]
                                                         / acc_ref[rows, HEAD_DIM:2 * HEAD_DIM])


def _selection(pb, bias, B, S, k_blk, v_blk):
    G = NSA_KV_GROUPS
    T = B * S
    tq = min(S, SEL_TQ)
    tk = min(S, SEL_TK)
    assert tk % tq == 0
    nq = S // tq
    nk = S // tk
    gw = NSA_REP * HEAD_DIM
    pairs = [(i, j) for i in range(nq) for j in range((i * tq + tq - 1) // tk + 1)]
    qi = jnp.asarray([p[0] for p in pairs], jnp.int32)
    kj = jnp.asarray([p[1] for p in pairs], jnp.int32)

    def q_map(b, g, n, qi_ref, kj_ref):
        return (b * nq + qi_ref[n], g)

    def kv_spec(col0):
        return pl.BlockSpec((tk, HEAD_DIM),
                            lambda b, g, n, qi_ref, kj_ref: (b * nk + kj_ref[n], col0 + g))

    grid_spec = pltpu.PrefetchScalarGridSpec(
        num_scalar_prefetch=2,
        grid=(B, G, len(pairs)),
        in_specs=[pl.BlockSpec((tq, gw), q_map), pl.BlockSpec((tq, LANES), q_map),
                  kv_spec(k_blk), kv_spec(v_blk)],
        out_specs=pl.BlockSpec((tq, gw), q_map),
        scratch_shapes=[pltpu.VMEM((NSA_REP * tq, 2 * HEAD_DIM), BF16),
                        pltpu.VMEM((NSA_REP * tq, LANES), F32),
                        pltpu.VMEM((NSA_REP * tq, 2 * HEAD_DIM), F32)])
    return pl.pallas_call(
        functools.partial(_sel_kernel, tq=tq, tk=tk),
        grid_spec=grid_spec,
        out_shape=jax.ShapeDtypeStruct((T, NSA_HEADS * HEAD_DIM), F32),
        compiler_params=_cparams("parallel", "parallel", "arbitrary"),
        name="nsa_selection",
    )(qi, kj, pb, bias, pb, pb)


def _nsa_out_kernel(x_ref, oc_ref, os_ref, ow_ref, z_ref, gl_ref, w_ref, o_ref, og_ref):
    sig = jax.nn.sigmoid(gl_ref[...])
    H = NSA_HEADS
    for h in range(H):
        sl = slice(h * HEAD_DIM, (h + 1) * HEAD_DIM)
        o = (sig[:, h:h + 1] * oc_ref[:, sl] + sig[:, H + h:H + h + 1] * os_ref[:, sl]
             + sig[:, 2 * H + h:2 * H + h + 1] * ow_ref[:, sl])
        og_ref[:, sl] = (o * _silu(z_ref[:, sl])).astype(BF16)
    o_ref[...] = x_ref[...] + _dot(og_ref[...], w_ref[...])


def _nsa_out(x, o_cmp, o_sel, o_win, pf, z_blk, gl_blk, w_out):
    T, D = x.shape
    W = o_cmp.shape[1]
    tm = min(T, OUT_TM)
    row = lambda i: (i, 0)
    return pl.pallas_call(
        _nsa_out_kernel,
        grid=(T // tm,),
        in_specs=[pl.BlockSpec((tm, D), row),
                  pl.BlockSpec((tm, W), row), pl.BlockSpec((tm, W), row), pl.BlockSpec((tm, W), row),
                  pl.BlockSpec((tm, W), lambda i: (i, z_blk)),
                  pl.BlockSpec((tm, LANES), lambda i: (i, gl_blk)),
                  pl.BlockSpec((W, D), lambda i: (0, 0))],
        out_specs=pl.BlockSpec((tm, D), row),
        out_shape=jax.ShapeDtypeStruct((T, D), F32),
        scratch_shapes=[pltpu.VMEM((tm, W), BF16)],
        compiler_params=_cparams("parallel"),
        name="nsa_out",
    )(x, o_cmp, o_sel, o_win, pf, pf, w_out)


def _nsa_layer(x, B, S, tabs, norm_w, w_in, ck_pos, ck_w1, ck_w2, cv_pos, cv_w1, cv_w2, w_out):
    H, G, d = NSA_HEADS, NSA_KV_GROUPS, HEAD_DIM
    kvw = G * d
    c = [0, H * d]
    for _ in range(6):
        c.append(c[-1] + kvw)
    c.append(c[-1] + 3 * H)
    c.append(c[-1] + H * d)
    seg = lambda n: w_in[:, c[n]:c[n + 1]]
    wq, wkc, wvc, wks, wvs, wkw, wvw, wgl, wz = [seg(n) for n in range(9)]
    w_b = jnp.concatenate([wq, wks, wkw, wvs, wvw], axis=1).astype(BF16)
    w_f = jnp.concatenate([wz, wkc, wvc, wgl, jnp.zeros((w_in.shape[0], LANES - 3 * H), w_in.dtype)],
                          axis=1).astype(BF16)
    pb = _norm_proj(x, norm_w, w_b, BF16, 1024, rope=(0, H + 2 * G), tabs=tabs, q_heads=H)
    pf = _norm_proj(x, norm_w, w_f, F32, 640, rope=(H, H + G), tabs=tabs)
    ks_blk, kw_blk, vs_blk, vw_blk = H, H + G, H + 2 * G, H + 3 * G
    kc_blk, vc_blk, gl_blk = H, H + G, H + 2 * G

    kT, vcmp = _compress(pf, B, S, kc_blk, vc_blk, ck_pos, ck_w1, ck_w2, cv_pos, cv_w1, cv_w2)
    o_cmp, bias = _cmp_topk(pb, kT, vcmp, B, S)
    o_win = _window(pb, B, S, kw_blk, vw_blk)
    o_sel = _selection(pb, bias, B, S, ks_blk, vs_blk)
    return _nsa_out(x, o_cmp, o_sel, o_win, pf, 0, gl_blk, w_out.astype(BF16))


def _hgrn_kernel(q_ref, f_ref, i_ref, z_ref, lbl_ref, gw_ref, o_ref,
                 st_ref, b_ref, k_ref, *, C, sub, hb, layer):
    n = pl.program_id(2)

    @pl.when(n == 0)
    def _():
        st_ref[...] = jnp.zeros(st_ref.shape, F32)

    r0 = lax.broadcasted_iota(jnp.int32, (C, C), 0)
    r1 = lax.broadcasted_iota(jnp.int32, (C, C), 1)
    tri = (r0 >= r1).astype(BF16)
    nblk = C // sub
    trow = lax.broadcasted_iota(jnp.int32, (nblk, sub, HEAD_DIM), 1)

    for h in range(hb):
        cs = slice(h * HEAD_DIM, (h + 1) * HEAD_DIM)
        lg = lbl_ref[:, cs]
        e = jnp.exp(lg - jnp.max(lg, axis=0, keepdims=True))
        p = e / jnp.sum(e, axis=0, keepdims=True)
        lb = jnp.sum(p[0:layer + 1, :], axis=0, keepdims=True) - p[0:1, :]

        fl = f_ref[:, cs]
        log_sig = jnp.minimum(fl, 0.0) - jnp.log1p(jnp.exp(-jnp.abs(fl)))
        la = jnp.log(lb)
        lc = jnp.log1p(-lb) + log_sig
        g = jnp.maximum(la, lc) + jnp.log1p(jnp.exp(-jnp.abs(la - lc)))
        k = (1.0 - lb) * jax.nn.sigmoid(-fl)
        qs = _silu(q_ref[:, cs])
        iv = i_ref[:, cs]
        ib = iv.astype(BF16)

        g1 = g.astype(BF16)
        rem = g - g1.astype(F32)
        g2 = rem.astype(BF16)
        g3 = (rem - g2.astype(F32)).astype(BF16)
        b = _dot(tri, g1) + _dot(tri, g2) + _dot(tri, g3)
        b_ref[h] = b
        k_ref[h] = k

        stT = st_ref[h]
        o_inter = _dot_nt((qs * jnp.exp(b)).astype(BF16), stT.astype(BF16))

        b3 = b.reshape(nblk, sub, HEAD_DIM)
        q3 = qs.reshape(nblk, sub, HEAD_DIM)
        k3 = k.reshape(nblk, sub, HEAD_DIM)
        i3 = iv.reshape(nblk, sub, HEAD_DIM)
        o3 = jnp.zeros((nblk, sub, HEAD_DIM), F32)
        for s in range(sub):
            w = jnp.where(trow >= s, jnp.exp(b3 - b3[:, s:s + 1, :]), 0.0)
            col = jnp.sum(q3 * w * k3[:, s:s + 1, :], axis=-1, keepdims=True)
            o3 = o3 + col * i3[:, s:s + 1, :]

        parts = [jnp.zeros((sub, HEAD_DIM), F32)]
        for blk in range(1, nblk):
            lo = blk * sub
            bref = b_ref[h, pl.ds(lo - 1, 1), :]
            qt = (qs[lo:lo + sub, :] * jnp.exp(b[lo:lo + sub, :] - bref)).astype(BF16)
            kt = (k_ref[h, pl.ds(0, lo), :] * jnp.exp(bref - b_ref[h, pl.ds(0, lo), :])).astype(BF16)
            att = _dot_nt(qt, kt)
            parts.append(_dot(att.astype(BF16), ib[0:lo, :]))
        o = o_inter + o3.reshape(C, HEAD_DIM) + jnp.concatenate(parts, axis=0)

        b_last = b_ref[h, pl.ds(C - 1, 1), :]
        kd = (k * jnp.exp(b_last - b)).astype(BF16)
        st_ref[h] = stT * jnp.exp(b_last) + _dot(iv.T.astype(BF16), kd)

        o = o * lax.rsqrt(jnp.mean(o * o, axis=-1, keepdims=True) + RMS_EPS) * gw_ref[...]
        o_ref[:, cs] = (o * _silu(z_ref[:, cs])).astype(BF16)


def _hgrn_recurrence(ph, lb_logits, gnorm_w, B, S, layer):
    H = HGRN_HEADS
    T = B * S
    C = min(S, HGRN_CHUNK)
    nc = S // C
    NL = lb_logits.shape[0]
    hb = HGRN_HEADS_PER_STEP
    hg = H // hb
    w = hb * HEAD_DIM

    def col(c0):
        return pl.BlockSpec((C, w), lambda b, h, n: (b * nc + n, c0 + h))

    return pl.pallas_call(
        functools.partial(_hgrn_kernel, C=C, sub=HGRN_SUB, hb=hb, layer=layer),
        grid=(B, hg, nc),
        in_specs=[col(0), col(hg), col(2 * hg), col(3 * hg),
                  pl.BlockSpec((NL, w), lambda b, h, n: (0, h)),
                  pl.BlockSpec((1, HEAD_DIM), lambda b, h, n: (0, 0))],
        out_specs=pl.BlockSpec((C, w), lambda b, h, n: (b * nc + n, h)),
        out_shape=jax.ShapeDtypeStruct((T, H * HGRN_I_DIM), BF16),
        scratch_shapes=[pltpu.VMEM((hb, HGRN_I_DIM, HGRN_F_DIM), F32),
                        pltpu.VMEM((hb, C, HEAD_DIM), F32),
                        pltpu.VMEM((hb, C, HEAD_DIM), F32)],
        compiler_params=_cparams("parallel", "parallel", "arbitrary"),
        name="hgrn_recurrence",
    )(ph, ph, ph, ph, lb_logits, gnorm_w.reshape(1, HEAD_DIM))


def _hgrn_out_kernel(x_ref, og_ref, w_ref, fw_ref, o_ref, *, final_norm):
    y = x_ref[...] + _dot(og_ref[...], w_ref[...])
    if final_norm:
        ms = jnp.mean(y * y, axis=-1, keepdims=True)
        y = y * lax.rsqrt(ms + RMS_EPS) * fw_ref[...]
    o_ref[...] = y


def _hgrn_out(x, og, w_out, final_w, final_norm):
    T, D = x.shape
    W = og.shape[1]
    tm = min(T, 2 * OUT_TM)
    row = lambda i: (i, 0)
    return pl.pallas_call(
        functools.partial(_hgrn_out_kernel, final_norm=final_norm),
        grid=(T // tm,),
        in_specs=[pl.BlockSpec((tm, D), row), pl.BlockSpec((tm, W), row),
                  pl.BlockSpec((W, D), lambda i: (0, 0)),
                  pl.BlockSpec((1, D), lambda i: (0, 0))],
        out_specs=pl.BlockSpec((tm, D), row),
        out_shape=jax.ShapeDtypeStruct((T, D), F32),
        compiler_params=_cparams("parallel"),
        name="hgrn_out",
    )(x, og, w_out, final_w.reshape(1, D))


def _hgrn_layer(x, B, S, norm_w, w_in, lb_logits, gnorm_w, w_out, layer, final_w, final_norm):
    ph = _norm_proj(x, norm_w, w_in.astype(BF16), F32, 1024)
    og = _hgrn_recurrence(ph, lb_logits, gnorm_w, B, S, layer)
    return _hgrn_out(x, og, w_out.astype(BF16), final_w, final_norm)


def kernel(x, positions, norm_w, final_norm_w, nsa_w_in, nsa_ck_pos, nsa_ck_w1, nsa_ck_w2,
           nsa_cv_pos, nsa_cv_w1, nsa_cv_w2, nsa_w_out, hgrn_w_in, hgrn_lb_logits,
           hgrn_gnorm_w, hgrn_w_out):
    B, S, D = x.shape
    depth = norm_w.shape[0]
    xf = x.reshape(B * S, D)
    tabs = _rope_tables(positions)
    for layer in range(depth):
        j = layer // 2
        if layer % 2 == 0:
            xf = _nsa_layer(xf, B, S, tabs, norm_w[layer], nsa_w_in[j], nsa_ck_pos[j], nsa_ck_w1[j],
                            nsa_ck_w2[j], nsa_cv_pos[j], nsa_cv_w1[j], nsa_cv_w2[j], nsa_w_out[j])
        else:
            xf = _hgrn_layer(xf, B, S, norm_w[layer], hgrn_w_in[j], hgrn_lb_logits, hgrn_gnorm_w[j],
                             hgrn_w_out[j], j, final_norm_w, layer == depth - 1)
    return xf.reshape(B, S, D)
```

```python
import functools
import math

import jax
import jax.numpy as jnp
from jax import lax
from jax.experimental import pallas as pl
from jax.experimental.pallas import tpu as pltpu

F32 = jnp.float32
BF16 = jnp.bfloat16

RMS_EPS = 1e-6
NEG = -1e30
BIG = 1e30
BELOW_NEG = -3e38
HEAD_DIM = 128
NSA_HEADS = 16
NSA_KV_GROUPS = 4
NSA_REP = NSA_HEADS // NSA_KV_GROUPS
CMP_BLOCK = 32
CMP_STRIDE = 16
SEL_BLOCK = 64
SEL_TOPK = 16
WINDOW = 512
ROPE_THETA = 500000.0
ROPE_DIM = HEAD_DIM // 4
HGRN_HEADS = 16
HGRN_F_DIM = 128
HGRN_I_DIM = 128

LANES = 128
VMEM_LIMIT_BYTES = 56 * 1024 * 1024

ATTN_SCALE = HEAD_DIM ** -0.5
EXP2_SCALE = ATTN_SCALE * math.log2(math.e)

PROJ_TM = 1024
CMP_TQ = 256
WIN_TQ = 256
SEL_TQ = 512
SEL_TK = 1024
OUT_TM = 256
HGRN_CHUNK = 128
HGRN_SUB = 16
HGRN_HEADS_PER_STEP = 4


def _cparams(*sem):
    return pltpu.CompilerParams(dimension_semantics=sem, vmem_limit_bytes=VMEM_LIMIT_BYTES)


def _dot(a, b):
    return jnp.dot(a, b, preferred_element_type=F32)


def _dot_nt(a, b):
    return lax.dot_general(a, b, (((1,), (1,)), ((), ())), preferred_element_type=F32)


def _silu(v):
    return v * jax.nn.sigmoid(v)


def _rope_table_kernel(pos_ref, inv_ref, cos_ref, sin_ref):
    ang = pos_ref[...] * inv_ref[...]
    lane = lax.broadcasted_iota(jnp.int32, ang.shape, 1)
    c = jnp.cos(ang)
    s = jnp.sin(ang)
    cos_ref[...] = jnp.where(lane < ROPE_DIM, c, 1.0)
    sin_ref[...] = jnp.where(lane < ROPE_DIM // 2, -s, jnp.where(lane < ROPE_DIM, s, 0.0))


def _rope_tables(positions):
    T = positions.size
    tm = min(T, 2048)
    half = ROPE_DIM // 2
    inv = ROPE_THETA ** (-jnp.arange(half, dtype=F32) / half)
    inv_pat = jnp.concatenate([inv, inv, jnp.zeros((LANES - ROPE_DIM,), F32)]).reshape(1, LANES)
    pos = positions.astype(F32).reshape(T, 1)
    return pl.pallas_call(
        _rope_table_kernel,
        grid=(T // tm,),
        in_specs=[pl.BlockSpec((tm, 1), lambda i: (i, 0)),
                  pl.BlockSpec((1, LANES), lambda i: (0, 0))],
        out_specs=[pl.BlockSpec((tm, LANES), lambda i: (i, 0))] * 2,
        out_shape=[jax.ShapeDtypeStruct((T, LANES), F32)] * 2,
        compiler_params=_cparams("parallel"),
        name="rope_tables",
    )(pos, inv_pat)


def _norm_proj_kernel(*refs, heads_per_tile, rope_lo, rope_hi, q_heads):
    if rope_hi > rope_lo:
        x_ref, nw_ref, w_ref, cos_ref, sin_ref, o_ref, hn_ref = refs
    else:
        x_ref, nw_ref, w_ref, o_ref, hn_ref = refs
    j = pl.program_id(1)

    @pl.when(j == 0)
    def _():
        x = x_ref[...]
        ms = jnp.mean(x * x, axis=-1, keepdims=True)
        hn_ref[...] = (x * lax.rsqrt(ms + RMS_EPS) * nw_ref[...]).astype(BF16)

    acc = _dot(hn_ref[...], w_ref[...])
    if rope_hi <= rope_lo:
        o_ref[...] = acc.astype(o_ref.dtype)
        return

    half = ROPE_DIM // 2
    for h in range(heads_per_tile):
        sl = slice(h * HEAD_DIM, (h + 1) * HEAD_DIM)
        head = j * heads_per_tile + h
        is_rope = jnp.logical_and(head >= rope_lo, head < rope_hi)

        @pl.when(is_rope)
        def _(sl=sl):
            t = acc[:, sl]
            lane = lax.broadcasted_iota(jnp.int32, t.shape, 1)
            sw = jnp.where(lane < half,
                           pltpu.roll(t, HEAD_DIM - half, axis=1),
                           pltpu.roll(t, half, axis=1))
            factor = jnp.where(head < q_heads, EXP2_SCALE, 1.0).astype(F32)
            o_ref[:, sl] = ((t * cos_ref[...] + sw * sin_ref[...]) * factor).astype(o_ref.dtype)

        @pl.when(jnp.logical_not(is_rope))
        def _(sl=sl):
            o_ref[:, sl] = acc[:, sl].astype(o_ref.dtype)


def _norm_proj(x, norm_w, w, out_dtype, tn, rope=None, tabs=None, q_heads=0):
    T, D = x.shape
    N = w.shape[1]
    tm = min(T, PROJ_TM)
    rope_lo, rope_hi = rope if rope is not None else (0, 0)
    in_specs = [pl.BlockSpec((tm, D), lambda i, j: (i, 0)),
                pl.BlockSpec((1, D), lambda i, j: (0, 0)),
                pl.BlockSpec((D, tn), lambda i, j: (0, j))]
    args = [x, norm_w.reshape(1, D), w]
    if rope is not None:
        in_specs += [pl.BlockSpec((tm, LANES), lambda i, j: (i, 0))] * 2
        args += list(tabs)
    return pl.pallas_call(
        functools.partial(_norm_proj_kernel, heads_per_tile=tn // HEAD_DIM,
                          rope_lo=rope_lo, rope_hi=rope_hi, q_heads=q_heads),
        grid=(T // tm, N // tn),
        in_specs=in_specs,
        out_specs=pl.BlockSpec((tm, tn), lambda i, j: (i, j)),
        out_shape=jax.ShapeDtypeStruct((T, N), out_dtype),
        scratch_shapes=[pltpu.VMEM((tm, D), BF16)],
        compiler_params=_cparams("parallel", "arbitrary"),
        name="norm_proj",
    )(*args)


def _compress_kernel(kc_ref, vc_ref, kpos_ref, kw1_ref, kw2_ref, vpos_ref, vw1_ref, vw2_ref,
                     kT_ref, v_ref, shift_ref, *, ncp):
    half_blk = CMP_BLOCK // 2

    def comp(t_ref, pos_ref, w1_ref, w2_ref):
        a = jnp.zeros((ncp, HEAD_DIM), F32)
        bm = jnp.zeros((ncp, HEAD_DIM), F32)
        for l in range(half_blk):
            xl = t_ref[pl.ds(l, ncp, stride=CMP_STRIDE), :]
            a = a + _dot((xl + pos_ref[l:l + 1, :]).astype(BF16), w1_ref[l])
            bm = bm + _dot((xl + pos_ref[half_blk + l:half_blk + l + 1, :]).astype(BF16),
                           w1_ref[half_blk + l])
        shift_ref[pl.ds(0, ncp), :] = bm
        shift_ref[pl.ds(ncp, 8), :] = jnp.zeros((8, HEAD_DIM), F32)
        hid = _silu(a + shift_ref[pl.ds(1, ncp), :])
        out = _dot(hid.astype(BF16), w2_ref[...])
        row = lax.broadcasted_iota(jnp.int32, out.shape, 0)
        return jnp.where(row < ncp - 1, out, 0.0)

    kT_ref[0, 0] = comp(kc_ref, kpos_ref, kw1_ref, kw2_ref).T.astype(BF16)
    v_ref[0, 0] = comp(vc_ref, vpos_ref, vw1_ref, vw2_ref).astype(BF16)


def _compress(pf, B, S, kc_blk, vc_blk, ck_pos, ck_w1, ck_w2, cv_pos, cv_w1, cv_w2):
    G = NSA_KV_GROUPS
    ncp = S // CMP_STRIDE
    full2 = lambda b, g: (0, 0)
    full3 = lambda b, g: (0, 0, 0)
    return pl.pallas_call(
        functools.partial(_compress_kernel, ncp=ncp),
        grid=(B, G),
        in_specs=[pl.BlockSpec((S, HEAD_DIM), lambda b, g: (b, kc_blk + g)),
                  pl.BlockSpec((S, HEAD_DIM), lambda b, g: (b, vc_blk + g)),
                  pl.BlockSpec((CMP_BLOCK, HEAD_DIM), full2),
                  pl.BlockSpec((CMP_BLOCK, HEAD_DIM, HEAD_DIM), full3),
                  pl.BlockSpec((HEAD_DIM, HEAD_DIM), full2),
                  pl.BlockSpec((CMP_BLOCK, HEAD_DIM), full2),
                  pl.BlockSpec((CMP_BLOCK, HEAD_DIM, HEAD_DIM), full3),
                  pl.BlockSpec((HEAD_DIM, HEAD_DIM), full2)],
        out_specs=[pl.BlockSpec((1, 1, HEAD_DIM, ncp), lambda b, g: (b, g, 0, 0)),
                   pl.BlockSpec((1, 1, ncp, HEAD_DIM), lambda b, g: (b, g, 0, 0))],
        out_shape=[jax.ShapeDtypeStruct((B, G, HEAD_DIM, ncp), BF16),
                   jax.ShapeDtypeStruct((B, G, ncp, HEAD_DIM), BF16)],
        scratch_shapes=[pltpu.VMEM((ncp + 8, HEAD_DIM), F32)],
        compiler_params=_cparams("parallel", "parallel"),
        name="nsa_compress",
    )(pf, pf, ck_pos, ck_w1.astype(BF16), ck_w2.astype(BF16),
      cv_pos, cv_w1.astype(BF16), cv_w2.astype(BF16))


def _cmp_topk_kernel(q_ref, kT_ref, v_ref, o_ref, bias_ref, imp_ref, *, tq, ncp):
    i = pl.program_id(2)
    s0 = i * tq
    t_col = s0 + lax.broadcasted_iota(jnp.int32, (tq, 1), 0)
    any_visible = (t_col >= CMP_BLOCK - 1).astype(F32)

    def attend(nb):
        nc = nb * LANES
        n_row = lax.broadcasted_iota(jnp.int32, (1, nc), 1)
        mc = jnp.logical_and(n_row * CMP_STRIDE + (CMP_BLOCK - 1) <= t_col, n_row < ncp - 1)
        kT = kT_ref[0, 0, :, 0:nc]
        v = v_ref[0, 0, 0:nc, :]
        psum = jnp.zeros((tq, nc), F32)
        for r in range(NSA_REP):
            sl = slice(r * HEAD_DIM, (r + 1) * HEAD_DIM)
            s = jnp.where(mc, _dot(q_ref[:, sl], kT), NEG)
            m = jnp.max(s, axis=-1, keepdims=True)
            e = jnp.exp2(s - m)
            l = jnp.sum(e, axis=-1, keepdims=True)
            pc = e * (any_visible / l)
            o_ref[:, sl] = _dot(pc.astype(BF16), v)
            psum = psum + pc
        n_col = lax.broadcasted_iota(jnp.int32, (nc, LANES), 0) * CMP_STRIDE
        k_row = lax.broadcasted_iota(jnp.int32, (nc, LANES), 1) * SEL_BLOCK
        ovl = jnp.logical_and(n_col < k_row + SEL_BLOCK, n_col + CMP_BLOCK > k_row).astype(BF16)
        hi = psum.astype(BF16)
        lo = (psum - hi.astype(F32)).astype(BF16)
        imp_ref[...] = _dot(hi, ovl) + _dot(lo, ovl)

    nb_total = ncp // LANES
    last_visible = (s0 + tq - CMP_BLOCK) // CMP_STRIDE
    nb_needed = jnp.minimum(last_visible // LANES + 1, nb_total)
    for nb in range(1, nb_total + 1):
        pl.when(nb_needed == nb)(functools.partial(attend, nb))

    impT = imp_ref[...].T
    blk = lax.broadcasted_iota(jnp.int32, impT.shape, 0)
    cur = (s0 + lax.broadcasted_iota(jnp.int32, impT.shape, 1)) // SEL_BLOCK
    valid = blk <= cur
    forced = jnp.logical_or(blk == 0, jnp.logical_or(blk == cur, blk == cur - 1))
    sc = jnp.where(valid, jnp.where(forced, BIG, impT), NEG)
    for _ in range(SEL_TOPK):
        m = jnp.max(sc, axis=0, keepdims=True)
        idx = jnp.min(jnp.where(sc == m, blk, LANES), axis=0, keepdims=True)
        sc = jnp.where(blk == idx, BELOW_NEG, sc)
    biasT = jnp.where(sc == BELOW_NEG, 0.0, NEG)
    bias_ref[...] = biasT.T.astype(BF16)


def _cmp_topk(pb, kT, vcmp, B, S):
    G = NSA_KV_GROUPS
    T = B * S
    tq = min(S, CMP_TQ)
    nq = S // tq
    ncp = S // CMP_STRIDE
    gw = NSA_REP * HEAD_DIM
    return pl.pallas_call(
        functools.partial(_cmp_topk_kernel, tq=tq, ncp=ncp),
        grid=(B, G, nq),
        in_specs=[pl.BlockSpec((tq, gw), lambda b, g, i: (b * nq + i, g)),
                  pl.BlockSpec((1, 1, HEAD_DIM, ncp), lambda b, g, i: (b, g, 0, 0)),
                  pl.BlockSpec((1, 1, ncp, HEAD_DIM), lambda b, g, i: (b, g, 0, 0))],
        out_specs=[pl.BlockSpec((tq, gw), lambda b, g, i: (b * nq + i, g)),
                   pl.BlockSpec((tq, LANES), lambda b, g, i: (b * nq + i, g))],
        out_shape=[jax.ShapeDtypeStruct((T, NSA_HEADS * HEAD_DIM), F32),
                   jax.ShapeDtypeStruct((T, G * LANES), BF16)],
        scratch_shapes=[pltpu.VMEM((tq, LANES), F32)],
        compiler_params=_cparams("parallel", "parallel", "parallel"),
        name="nsa_cmp_topk",
    )(pb, kT, vcmp)


def _win_kernel(q_ref, k0_ref, k1_ref, k2_ref, v0_ref, v1_ref, v2_ref, o_ref, *, tq):
    i = pl.program_id(2)
    t_col = i * tq + lax.broadcasted_iota(jnp.int32, (tq, 1), 0)
    kp = (i - 2) * tq + lax.broadcasted_iota(jnp.int32, (1, 3 * tq), 1)
    mw = jnp.logical_and(jnp.logical_and(kp <= t_col, kp > t_col - WINDOW), kp >= 0)
    k = jnp.concatenate([k0_ref[...], k1_ref[...], k2_ref[...]], axis=0)
    v = jnp.concatenate([v0_ref[...], v1_ref[...], v2_ref[...]], axis=0)
    for r in range(NSA_REP):
        sl = slice(r * HEAD_DIM, (r + 1) * HEAD_DIM)
        s = jnp.where(mw, _dot_nt(q_ref[:, sl], k), NEG)
        m = jnp.max(s, axis=-1, keepdims=True)
        e = jnp.exp2(s - m)
        l = jnp.sum(e, axis=-1, keepdims=True)
        o_ref[:, sl] = _dot(e.astype(BF16), v) / l


def _window(pb, B, S, k_blk, v_blk):
    G = NSA_KV_GROUPS
    T = B * S
    tq = min(S // 2, WIN_TQ)
    assert WINDOW <= 2 * tq
    nq = S // tq
    gw = NSA_REP * HEAD_DIM

    def kv_spec(col0, back):
        return pl.BlockSpec((tq, HEAD_DIM),
                            lambda b, g, i: (b * nq + jnp.maximum(i - back, 0), col0 + g))

    return pl.pallas_call(
        functools.partial(_win_kernel, tq=tq),
        grid=(B, G, nq),
        in_specs=[pl.BlockSpec((tq, gw), lambda b, g, i: (b * nq + i, g)),
                  kv_spec(k_blk, 2), kv_spec(k_blk, 1), kv_spec(k_blk, 0),
                  kv_spec(v_blk, 2), kv_spec(v_blk, 1), kv_spec(v_blk, 0)],
        out_specs=pl.BlockSpec((tq, gw), lambda b, g, i: (b * nq + i, g)),
        out_shape=jax.ShapeDtypeStruct((T, NSA_HEADS * HEAD_DIM), F32),
        compiler_params=_cparams("parallel", "parallel", "parallel"),
        name="nsa_window",
    )(pb, pb, pb, pb, pb, pb, pb)


def _sel_kernel(qi_ref, kj_ref, q_ref, bias_ref, k_ref, v_ref, o_ref, qa_ref, m_ref, acc_ref, *, tq, tk):
    n = pl.program_id(2)
    i = qi_ref[n]
    j = kj_ref[n]
    last_j = (i * tq + tq - 1) // tk

    @pl.when(j == 0)
    def _():
        for r in range(NSA_REP):
            rows = slice(r * tq, (r + 1) * tq)
            qa_ref[rows, 0:HEAD_DIM] = q_ref[:, r * HEAD_DIM:(r + 1) * HEAD_DIM]
            qa_ref[rows, HEAD_DIM:2 * HEAD_DIM] = bias_ref[...]
        m_ref[...] = jnp.full(m_ref.shape, NEG, F32)
        acc_ref[...] = jnp.zeros(acc_ref.shape, F32)

    def step(col0, ncols, diagonal):
        key = j * tk + col0 + lax.broadcasted_iota(jnp.int32, (ncols, LANES), 0)
        blk = lax.broadcasted_iota(jnp.int32, (ncols, LANES), 1)
        onehot = (key // SEL_BLOCK == blk).astype(BF16)
        ka = jnp.concatenate([k_ref[col0:col0 + ncols, :], onehot], axis=1)
        va = jnp.concatenate([v_ref[col0:col0 + ncols, :], jnp.ones((ncols, LANES), BF16)], axis=1)
        if diagonal:
            visible = (lax.broadcasted_iota(jnp.int32, (tq, tq), 1)
                       <= lax.broadcasted_iota(jnp.int32, (tq, tq), 0))
        for r in range(NSA_REP):
            rows = slice(r * tq, (r + 1) * tq)
            s = _dot_nt(qa_ref[rows, :], ka)
            if diagonal:
                s = jnp.where(visible, s, NEG)
            m_prev = m_ref[rows, :]
            m_new = jnp.maximum(m_prev, jnp.max(s, axis=-1, keepdims=True))
            alpha = jnp.exp2(m_prev - m_new)
            p = jnp.exp2(s - jnp.tile(m_new, (1, ncols // LANES)))
            acc_ref[rows, :] = jnp.tile(alpha, (1, 2)) * acc_ref[rows, :] + _dot(p.astype(BF16), va)
            m_ref[rows, :] = m_new

    @pl.when(j < last_j)
    def _():
        step(0, tk, False)

    for sub in range(tk // tq):
        @pl.when(jnp.logical_and(j == last_j, i % (tk // tq) == sub))
        def _(sub=sub):
            if sub > 0:
                step(0, sub * tq, False)
            step(sub * tq, tq, True)

    @pl.when(j == last_j)
    def _():
        for r in range(NSA_REP):
            rows = slice(r * tq, (r + 1) * tq)
            o_ref[:, r * HEAD_DIM:(r + 1) * HEAD_DIM] = (acc_ref[rows, 0:HEAD_DIM]
                                                         / acc_ref[rows, HEAD_DIM:2 * HEAD_DIM])


def _selection(pb, bias, B, S, k_blk, v_blk):
    G = NSA_KV_GROUPS
    T = B * S
    tq = min(S, SEL_TQ)
    tk = min(S, SEL_TK)
    assert tk % tq == 0
    nq = S // tq
    nk = S // tk
    gw = NSA_REP * HEAD_DIM
    pairs = [(i, j) for i in range(nq) for j in range((i * tq + tq - 1) // tk + 1)]
    qi = jnp.asarray([p[0] for p in pairs], jnp.int32)
    kj = jnp.asarray([p[1] for p in pairs], jnp.int32)

    def q_map(b, g, n, qi_ref, kj_ref):
        return (b * nq + qi_ref[n], g)

    def kv_spec(col0):
        return pl.BlockSpec((tk, HEAD_DIM),
                            lambda b, g, n, qi_ref, kj_ref: (b * nk + kj_ref[n], col0 + g))

    grid_spec = pltpu.PrefetchScalarGridSpec(
        num_scalar_prefetch=2,
        grid=(B, G, len(pairs)),
        in_specs=[pl.BlockSpec((tq, gw), q_map), pl.BlockSpec((tq, LANES), q_map),
                  kv_spec(k_blk), kv_spec(v_blk)],
        out_specs=pl.BlockSpec((tq, gw), q_map),
        scratch_shapes=[pltpu.VMEM((NSA_REP * tq, 2 * HEAD_DIM), BF16),
                        pltpu.VMEM((NSA_REP * tq, LANES), F32),
                        pltpu.VMEM((NSA_REP * tq, 2 * HEAD_DIM), F32)])
    return pl.pallas_call(
        functools.partial(_sel_kernel, tq=tq, tk=tk),
        grid_spec=grid_spec,
        out_shape=jax.ShapeDtypeStruct((T, NSA_HEADS * HEAD_DIM), F32),
        compiler_params=_cparams("parallel", "parallel", "arbitrary"),
        name="nsa_selection",
    )(qi, kj, pb, bias, pb, pb)


def _nsa_out_kernel(x_ref, oc_ref, os_ref, ow_ref, z_ref, gl_ref, w_ref, o_ref, og_ref):
    sig = jax.nn.sigmoid(gl_ref[...])
    H = NSA_HEADS
    for h in range(H):
        sl = slice(h * HEAD_DIM, (h + 1) * HEAD_DIM)
        o = (sig[:, h:h + 1] * oc_ref[:, sl] + sig[:, H + h:H + h + 1] * os_ref[:, sl]
             + sig[:, 2 * H + h:2 * H + h + 1] * ow_ref[:, sl])
        og_ref[:, sl] = (o * _silu(z_ref[:, sl])).astype(BF16)
    o_ref[...] = x_ref[...] + _dot(og_ref[...], w_ref[...])


def _nsa_out(x, o_cmp, o_sel, o_win, pf, z_blk, gl_blk, w_out):
    T, D = x.shape
    W = o_cmp.shape[1]
    tm = min(T, OUT_TM)
    row = lambda i: (i, 0)
    return pl.pallas_call(
        _nsa_out_kernel,
        grid=(T // tm,),
        in_specs=[pl.BlockSpec((tm, D), row),
                  pl.BlockSpec((tm, W), row), pl.BlockSpec((tm, W), row), pl.BlockSpec((tm, W), row),
                  pl.BlockSpec((tm, W), lambda i: (i, z_blk)),
                  pl.BlockSpec((tm, LANES), lambda i: (i, gl_blk)),
                  pl.BlockSpec((W, D), lambda i: (0, 0))],
        out_specs=pl.BlockSpec((tm, D), row),
        out_shape=jax.ShapeDtypeStruct((T, D), F32),
        scratch_shapes=[pltpu.VMEM((tm, W), BF16)],
        compiler_params=_cparams("parallel"),
        name="nsa_out",
    )(x, o_cmp, o_sel, o_win, pf, pf, w_out)


def _nsa_layer(x, B, S, tabs, norm_w, w_in, ck_pos, ck_w1, ck_w2, cv_pos, cv_w1, cv_w2, w_out):
    H, G, d = NSA_HEADS, NSA_KV_GROUPS, HEAD_DIM
    kvw = G * d
    c = [0, H * d]
    for _ in range(6):
        c.append(c[-1] + kvw)
    c.append(c[-1] + 3 * H)
    c.append(c[-1] + H * d)
    seg = lambda n: w_in[:, c[n]:c[n + 1]]
    wq, wkc, wvc, wks, wvs, wkw, wvw, wgl, wz = [seg(n) for n in range(9)]
    w_b = jnp.concatenate([wq, wks, wkw, wvs, wvw], axis=1).astype(BF16)
    w_f = jnp.concatenate([wz, wkc, wvc, wgl, jnp.zeros((w_in.shape[0], LANES - 3 * H), w_in.dtype)],
                          axis=1).astype(BF16)
    pb = _norm_proj(x, norm_w, w_b, BF16, 1024, rope=(0, H + 2 * G), tabs=tabs, q_heads=H)
    pf = _norm_proj(x, norm_w, w_f, F32, 640, rope=(H, H + G), tabs=tabs)
    ks_blk, kw_blk, vs_blk, vw_blk = H, H + G, H + 2 * G, H + 3 * G
    kc_blk, vc_blk, gl_blk = H, H + G, H + 2 * G

    kT, vcmp = _compress(pf, B, S, kc_blk, vc_blk, ck_pos, ck_w1, ck_w2, cv_pos, cv_w1, cv_w2)
    o_cmp, bias = _cmp_topk(pb, kT, vcmp, B, S)
    o_win = _window(pb, B, S, kw_blk, vw_blk)
    o_sel = _selection(pb, bias, B, S, ks_blk, vs_blk)
    return _nsa_out(x, o_cmp, o_sel, o_win, pf, 0, gl_blk, w_out.astype(BF16))


def _hgrn_kernel(q_ref, f_ref, i_ref, z_ref, lbl_ref, gw_ref, o_ref,
                 st_ref, *, C, sub, hb, layer):
    n = pl.program_id(2)

    @pl.when(n == 0)
    def _():
        st_ref[...] = jnp.zeros(st_ref.shape, F32)

    r0 = lax.broadcasted_iota(jnp.int32, (C, C), 0)
    r1 = lax.broadcasted_iota(jnp.int32, (C, C), 1)
    tri = (r0 >= r1).astype(BF16)
    dsub = sub // 2
    drow = lax.broadcasted_iota(jnp.int32, (C // dsub, dsub, HEAD_DIM), 1)
    row_id = lax.broadcasted_iota(jnp.int32, (C, HEAD_DIM), 0)

    for h in range(hb):
        cs = slice(h * HEAD_DIM, (h + 1) * HEAD_DIM)
        lg = lbl_ref[:, cs]
        e = jnp.exp(lg - jnp.max(lg, axis=0, keepdims=True))
        p = e / jnp.sum(e, axis=0, keepdims=True)
        lb = jnp.sum(p[0:layer + 1, :], axis=0, keepdims=True) - p[0:1, :]

        fl = f_ref[:, cs]
        t = jnp.exp(-jnp.abs(fl))
        r = 1.0 / (1.0 + t)
        pos = fl >= 0.0
        sig_p = jnp.where(pos, r, t * r)
        sig_n = jnp.where(pos, t * r, r)
        k = (1.0 - lb) * sig_n
        lc = jnp.minimum(fl, 0.0) - jnp.log(1.0 + t) + jnp.log1p(-lb)
        g = jnp.maximum(jnp.log(lb + (1.0 - lb) * sig_p), lc)
        qs = _silu(q_ref[:, cs])
        iv = i_ref[:, cs]
        ib = iv.astype(BF16)

        g1 = g.astype(BF16)
        rem = g - g1.astype(F32)
        g2 = rem.astype(BF16)
        g3 = (rem - g2.astype(F32)).astype(BF16)
        b = _dot(tri, g1) + _dot(tri, g2) + _dot(tri, g3)

        stT = st_ref[h]
        o_inter = _dot_nt((qs * jnp.exp(b)).astype(BF16), stT.astype(BF16))

        nd = C // dsub
        b3 = b.reshape(nd, dsub, HEAD_DIM)
        q3 = qs.reshape(nd, dsub, HEAD_DIM)
        k3 = k.reshape(nd, dsub, HEAD_DIM)
        i3 = iv.reshape(nd, dsub, HEAD_DIM)
        o3 = jnp.zeros((nd, dsub, HEAD_DIM), F32)
        for s in range(dsub):
            w = jnp.where(drow >= s, jnp.exp(b3 - b3[:, s:s + 1, :]), 0.0)
            col = jnp.sum(q3 * w * k3[:, s:s + 1, :], axis=-1, keepdims=True)
            o3 = o3 + col * i3[:, s:s + 1, :]

        att = None
        bsz = 2 * dsub
        while bsz <= C:
            bl = b.reshape(C // bsz, bsz, HEAD_DIM)
            bmid = jnp.broadcast_to(bl[:, bsz // 2 - 1:bsz // 2, :], bl.shape).reshape(C, HEAD_DIM)
            second = (row_id % bsz) >= bsz // 2
            qk = jnp.where(second, qs, k) * jnp.exp(-jnp.abs(b - bmid))
            qh = jnp.where(second, qk, 0.0).astype(BF16)
            kh = jnp.where(second, 0.0, qk).astype(BF16)
            a = _dot_nt(qh, kh)
            if bsz < C:
                a = jnp.where(r0 // bsz == r1 // bsz, a, 0.0)
            att = a if att is None else att + a
            bsz *= 2
        o = o_inter + o3.reshape(C, HEAD_DIM) + _dot(att.astype(BF16), ib)

        b_last = b[C - 1:C, :]
        kd = (k * jnp.exp(b_last - b)).astype(BF16)
        st_ref[h] = stT * jnp.exp(b_last) + _dot(iv.T.astype(BF16), kd)

        o = o * lax.rsqrt(jnp.mean(o * o, axis=-1, keepdims=True) + RMS_EPS) * gw_ref[...]
        o_ref[:, cs] = (o * _silu(z_ref[:, cs])).astype(BF16)


def _hgrn_recurrence(ph, lb_logits, gnorm_w, B, S, layer):
    H = HGRN_HEADS
    T = B * S
    C = min(S, HGRN_CHUNK)
    nc = S // C
    NL = lb_logits.shape[0]
    hb = HGRN_HEADS_PER_STEP
    hg = H // hb
    w = hb * HEAD_DIM

    def col(c0):
        return pl.BlockSpec((C, w), lambda b, h, n: (b * nc + n, c0 + h))

    return pl.pallas_call(
        functools.partial(_hgrn_kernel, C=C, sub=HGRN_SUB, hb=hb, layer=layer),
        grid=(B, hg, nc),
        in_specs=[col(0), col(hg), col(2 * hg), col(3 * hg),
                  pl.BlockSpec((NL, w), lambda b, h, n: (0, h)),
                  pl.BlockSpec((1, HEAD_DIM), lambda b, h, n: (0, 0))],
        out_specs=pl.BlockSpec((C, w), lambda b, h, n: (b * nc + n, h)),
        out_shape=jax.ShapeDtypeStruct((T, H * HGRN_I_DIM), BF16),
        scratch_shapes=[pltpu.VMEM((hb, HGRN_I_DIM, HGRN_F_DIM), F32)],
        compiler_params=_cparams("parallel", "parallel", "arbitrary"),
        name="hgrn_recurrence",
    )(ph, ph, ph, ph, lb_logits, gnorm_w.reshape(1, HEAD_DIM))


def _hgrn_out_kernel(x_ref, og_ref, w_ref, fw_ref, o_ref, *, final_norm):
    y = x_ref[...] + _dot(og_ref[...], w_ref[...])
    if final_norm:
        ms = jnp.mean(y * y, axis=-1, keepdims=True)
        y = y * lax.rsqrt(ms + RMS_EPS) * fw_ref[...]
    o_ref[...] = y


def _hgrn_out(x, og, w_out, final_w, final_norm):
    T, D = x.shape
    W = og.shape[1]
    tm = min(T, 2 * OUT_TM)
    row = lambda i: (i, 0)
    return pl.pallas_call(
        functools.partial(_hgrn_out_kernel, final_norm=final_norm),
        grid=(T // tm,),
        in_specs=[pl.BlockSpec((tm, D), row), pl.BlockSpec((tm, W), row),
                  pl.BlockSpec((W, D), lambda i: (0, 0)),
                  pl.BlockSpec((1, D), lambda i: (0, 0))],
        out_specs=pl.BlockSpec((tm, D), row),
        out_shape=jax.ShapeDtypeStruct((T, D), F32),
        compiler_params=_cparams("parallel"),
        name="hgrn_out",
    )(x, og, w_out, final_w.reshape(1, D))


def _hgrn_layer(x, B, S, norm_w, w_in, lb_logits, gnorm_w, w_out, layer, final_w, final_norm):
    ph = _norm_proj(x, norm_w, w_in.astype(BF16), F32, 1024)
    og = _hgrn_recurrence(ph, lb_logits, gnorm_w, B, S, layer)
    return _hgrn_out(x, og, w_out.astype(BF16), final_w, final_norm)


def kernel(x, positions, norm_w, final_norm_w, nsa_w_in, nsa_ck_pos, nsa_ck_w1, nsa_ck_w2,
           nsa_cv_pos, nsa_cv_w1, nsa_cv_w2, nsa_w_out, hgrn_w_in, hgrn_lb_logits,
           hgrn_gnorm_w, hgrn_w_out):
    B, S, D = x.shape
    depth = norm_w.shape[0]
    xf = x.reshape(B * S, D)
    tabs = _rope_tables(positions)
    for layer in range(depth):
        j = layer // 2
        if layer % 2 == 0:
            xf = _nsa_layer(xf, B, S, tabs, norm_w[layer], nsa_w_in[j], nsa_ck_pos[j], nsa_ck_w1[j],
                            nsa_ck_w2[j], nsa_cv_pos[j], nsa_cv_w1[j], nsa_cv_w2[j], nsa_w_out[j])
        else:
            xf = _hgrn_layer(xf, B, S, norm_w[layer], hgrn_w_in[j], hgrn_lb_logits, hgrn_gnorm_w[j],
                             hgrn_w_out[j], j, final_norm_w, layer == depth - 1)
    return xf.reshape(B, S, D)
```

```python
import functools
import math

import jax
import jax.numpy as jnp
from jax import lax
from jax.experimental import pallas as pl
from jax.experimental.pallas import tpu as pltpu

F32 = jnp.float32
BF16 = jnp.bfloat16

RMS_EPS = 1e-6
NEG = -1e30
BIG = 1e30
BELOW_NEG = -3e38
HEAD_DIM = 128
NSA_HEADS = 16
NSA_KV_GROUPS = 4
NSA_REP = NSA_HEADS // NSA_KV_GROUPS
CMP_BLOCK = 32
CMP_STRIDE = 16
SEL_BLOCK = 64
SEL_TOPK = 16
WINDOW = 512
ROPE_THETA = 500000.0
ROPE_DIM = HEAD_DIM // 4
HGRN_HEADS = 16
HGRN_F_DIM = 128
HGRN_I_DIM = 128

LANES = 128
VMEM_LIMIT_BYTES = 56 * 1024 * 1024

ATTN_SCALE = HEAD_DIM ** -0.5
EXP2_SCALE = ATTN_SCALE * math.log2(math.e)

PROJ_TM = 1024
PROJ_CHUNK_HEADS = 2
CMP_TQ = 256
WIN_TQ = 256
SEL_TQ = 512
SEL_TK = 1024
OUT_TM = 256
HGRN_CHUNK = 128
HGRN_SUB = 16
HGRN_HEADS_PER_STEP = 4


def _cparams(*sem):
    return pltpu.CompilerParams(dimension_semantics=sem, vmem_limit_bytes=VMEM_LIMIT_BYTES)


def _dot(a, b):
    return jnp.dot(a, b, preferred_element_type=F32)


def _dot_nt(a, b):
    return lax.dot_general(a, b, (((1,), (1,)), ((), ())), preferred_element_type=F32)


def _silu(v):
    return v * jax.nn.sigmoid(v)


_HALF = ROPE_DIM // 2
_PARTNER_SHIFT = LANES // 2
ROPE_PERM = (list(range(_HALF))
             + list(range(ROPE_DIM, ROPE_DIM + _PARTNER_SHIFT - _HALF))
             + list(range(_HALF, ROPE_DIM))
             + list(range(ROPE_DIM + _PARTNER_SHIFT - _HALF, HEAD_DIM)))


def _permute_head_dims(w, axis):
    axis = axis % w.ndim
    n = w.shape[axis] // HEAD_DIM
    shape = w.shape[:axis] + (n, HEAD_DIM) + w.shape[axis + 1:]
    wr = w.reshape(shape)
    cuts = [(0, _HALF), (ROPE_DIM, ROPE_DIM + _PARTNER_SHIFT - _HALF), (_HALF, ROPE_DIM),
            (ROPE_DIM + _PARTNER_SHIFT - _HALF, HEAD_DIM)]
    parts = [lax.slice_in_dim(wr, lo, hi, axis=axis + 1) for lo, hi in cuts]
    return jnp.concatenate(parts, axis=axis + 1).reshape(w.shape)


def _rope_table_kernel(pos_ref, inv_ref, cos_ref, sin_ref):
    ang = pos_ref[...] * inv_ref[...]
    lane = lax.broadcasted_iota(jnp.int32, ang.shape, 1)
    s = jnp.sin(ang)
    cos_ref[...] = jnp.cos(ang)
    sin_ref[...] = jnp.where(lane < _HALF, -s, s)


def _rope_tables(positions):
    T = positions.size
    tm = min(T, 2048)
    inv = ROPE_THETA ** (-jnp.arange(_HALF, dtype=F32) / _HALF)
    inv_pat = jnp.concatenate([inv, jnp.zeros((_PARTNER_SHIFT - _HALF,), F32), inv,
                               jnp.zeros((LANES - _PARTNER_SHIFT - _HALF,), F32)]).reshape(1, LANES)
    pos = positions.astype(F32).reshape(T, 1)
    return pl.pallas_call(
        _rope_table_kernel,
        grid=(T // tm,),
        in_specs=[pl.BlockSpec((tm, 1), lambda i: (i, 0)),
                  pl.BlockSpec((1, LANES), lambda i: (0, 0))],
        out_specs=[pl.BlockSpec((tm, LANES), lambda i: (i, 0))] * 2,
        out_shape=[jax.ShapeDtypeStruct((T, LANES), F32)] * 2,
        compiler_params=_cparams("parallel"),
        name="rope_tables",
    )(pos, inv_pat)


def _norm_proj_kernel(*refs, n_tiles, heads_per_tile, rope_lo, rope_hi, q_heads):
    if rope_hi > rope_lo:
        x_ref, nw_ref, w_ref, cos_ref, sin_ref, o_ref, hn_ref = refs
    else:
        x_ref, nw_ref, w_ref, o_ref, hn_ref = refs
    j = pl.program_id(1)

    @pl.when(j == 0)
    def _():
        x = x_ref[...]
        ms = jnp.mean(x * x, axis=-1, keepdims=True)
        hn_ref[...] = (x * lax.rsqrt(ms + RMS_EPS) * nw_ref[...]).astype(BF16)

    def tile_body(kinds):
        h = 0
        while h < heads_per_tile:
            nh = min(PROJ_CHUNK_HEADS, heads_per_tile - h)
            acc = _dot(hn_ref[...], w_ref[:, h * HEAD_DIM:(h + nh) * HEAD_DIM])
            for hh in range(nh):
                sl = slice((h + hh) * HEAD_DIM, (h + hh + 1) * HEAD_DIM)
                t = acc[:, hh * HEAD_DIM:(hh + 1) * HEAD_DIM]
                kind = kinds[h + hh]
                if kind != 'plain':
                    t = t * cos_ref[...] + pltpu.roll(t, _PARTNER_SHIFT, axis=1) * sin_ref[...]
                if kind == 'rope_q':
                    t = t * EXP2_SCALE
                o_ref[:, sl] = t.astype(o_ref.dtype)
            h += nh

    def kind_of(head):
        if rope_lo <= head < rope_hi:
            return 'rope_q' if head < q_heads else 'rope'
        return 'plain'

    patterns = {}
    for jj in range(n_tiles):
        kinds = tuple(kind_of(jj * heads_per_tile + h) for h in range(heads_per_tile))
        patterns.setdefault(kinds, []).append(jj)
    for kinds, tiles in patterns.items():
        cond = functools.reduce(jnp.logical_or, [j == jj for jj in tiles])
        pl.when(cond)(functools.partial(tile_body, kinds))


def _norm_proj(x, norm_w, w, out_dtype, tn, rope=None, tabs=None, q_heads=0):
    T, D = x.shape
    N = w.shape[1]
    tm = min(T, PROJ_TM)
    rope_lo, rope_hi = rope if rope is not None else (0, 0)
    in_specs = [pl.BlockSpec((tm, D), lambda i, j: (i, 0)),
                pl.BlockSpec((1, D), lambda i, j: (0, 0)),
                pl.BlockSpec((D, tn), lambda i, j: (0, j))]
    args = [x, norm_w.reshape(1, D), w]
    if rope is not None:
        in_specs += [pl.BlockSpec((tm, LANES), lambda i, j: (i, 0))] * 2
        args += list(tabs)
    return pl.pallas_call(
        functools.partial(_norm_proj_kernel, n_tiles=N // tn, heads_per_tile=tn // HEAD_DIM,
                          rope_lo=rope_lo, rope_hi=rope_hi, q_heads=q_heads),
        grid=(T // tm, N // tn),
        in_specs=in_specs,
        out_specs=pl.BlockSpec((tm, tn), lambda i, j: (i, j)),
        out_shape=jax.ShapeDtypeStruct((T, N), out_dtype),
        scratch_shapes=[pltpu.VMEM((tm, D), BF16)],
        compiler_params=_cparams("parallel", "arbitrary"),
        name="norm_proj",
    )(*args)


def _compress_kernel(kc_ref, vc_ref, kpos_ref, kw1_ref, kw2_ref, vpos_ref, vw1_ref, vw2_ref,
                     kT_ref, v_ref, shift_ref, *, ncp):
    half_blk = CMP_BLOCK // 2

    def comp(t_ref, pos_ref, w1_ref, w2_ref):
        a = jnp.zeros((ncp, HEAD_DIM), F32)
        bm = jnp.zeros((ncp, HEAD_DIM), F32)
        for l in range(half_blk):
            xl = t_ref[pl.ds(l, ncp, stride=CMP_STRIDE), :]
            a = a + _dot((xl + pos_ref[l:l + 1, :]).astype(BF16), w1_ref[l])
            bm = bm + _dot((xl + pos_ref[half_blk + l:half_blk + l + 1, :]).astype(BF16),
                           w1_ref[half_blk + l])
        shift_ref[pl.ds(0, ncp), :] = bm
        shift_ref[pl.ds(ncp, 8), :] = jnp.zeros((8, HEAD_DIM), F32)
        hid = _silu(a + shift_ref[pl.ds(1, ncp), :])
        out = _dot(hid.astype(BF16), w2_ref[...])
        row = lax.broadcasted_iota(jnp.int32, out.shape, 0)
        return jnp.where(row < ncp - 1, out, 0.0)

    kT_ref[0, 0] = comp(kc_ref, kpos_ref, kw1_ref, kw2_ref).T.astype(BF16)
    v_ref[0, 0] = comp(vc_ref, vpos_ref, vw1_ref, vw2_ref).astype(BF16)


def _compress(pf, B, S, kc_blk, vc_blk, ck_pos, ck_w1, ck_w2, cv_pos, cv_w1, cv_w2):
    G = NSA_KV_GROUPS
    ncp = S // CMP_STRIDE
    full2 = lambda b, g: (0, 0)
    full3 = lambda b, g: (0, 0, 0)
    return pl.pallas_call(
        functools.partial(_compress_kernel, ncp=ncp),
        grid=(B, G),
        in_specs=[pl.BlockSpec((S, HEAD_DIM), lambda b, g: (b, kc_blk + g)),
                  pl.BlockSpec((S, HEAD_DIM), lambda b, g: (b, vc_blk + g)),
                  pl.BlockSpec((CMP_BLOCK, HEAD_DIM), full2),
                  pl.BlockSpec((CMP_BLOCK, HEAD_DIM, HEAD_DIM), full3),
                  pl.BlockSpec((HEAD_DIM, HEAD_DIM), full2),
                  pl.BlockSpec((CMP_BLOCK, HEAD_DIM), full2),
                  pl.BlockSpec((CMP_BLOCK, HEAD_DIM, HEAD_DIM), full3),
                  pl.BlockSpec((HEAD_DIM, HEAD_DIM), full2)],
        out_specs=[pl.BlockSpec((1, 1, HEAD_DIM, ncp), lambda b, g: (b, g, 0, 0)),
                   pl.BlockSpec((1, 1, ncp, HEAD_DIM), lambda b, g: (b, g, 0, 0))],
        out_shape=[jax.ShapeDtypeStruct((B, G, HEAD_DIM, ncp), BF16),
                   jax.ShapeDtypeStruct((B, G, ncp, HEAD_DIM), BF16)],
        scratch_shapes=[pltpu.VMEM((ncp + 8, HEAD_DIM), F32)],
        compiler_params=_cparams("parallel", "parallel"),
        name="nsa_compress",
    )(pf, pf, ck_pos, ck_w1.astype(BF16), ck_w2.astype(BF16),
      cv_pos, cv_w1.astype(BF16), cv_w2.astype(BF16))


def _cmp_topk_kernel(q_ref, kT_ref, v_ref, o_ref, bias_ref, imp_ref, *, tq, ncp):
    i = pl.program_id(2)
    s0 = i * tq
    t_col = s0 + lax.broadcasted_iota(jnp.int32, (tq, 1), 0)
    any_visible = (t_col >= CMP_BLOCK - 1).astype(F32)

    def attend(nb):
        nc = nb * LANES
        n_row = lax.broadcasted_iota(jnp.int32, (1, nc), 1)
        mc = jnp.logical_and(n_row * CMP_STRIDE + (CMP_BLOCK - 1) <= t_col, n_row < ncp - 1)
        kT = kT_ref[0, 0, :, 0:nc]
        v = v_ref[0, 0, 0:nc, :]
        psum = jnp.zeros((tq, nc), F32)
        for r in range(NSA_REP):
            sl = slice(r * HEAD_DIM, (r + 1) * HEAD_DIM)
            s = jnp.where(mc, _dot(q_ref[:, sl], kT), NEG)
            m = jnp.max(s, axis=-1, keepdims=True)
            e = jnp.exp2(s - m)
            l = jnp.sum(e, axis=-1, keepdims=True)
            pc = e * (any_visible / l)
            o_ref[:, sl] = _dot(pc.astype(BF16), v)
            psum = psum + pc
        n_col = lax.broadcasted_iota(jnp.int32, (nc, LANES), 0) * CMP_STRIDE
        k_row = lax.broadcasted_iota(jnp.int32, (nc, LANES), 1) * SEL_BLOCK
        ovl = jnp.logical_and(n_col < k_row + SEL_BLOCK, n_col + CMP_BLOCK > k_row).astype(BF16)
        hi = psum.astype(BF16)
        lo = (psum - hi.astype(F32)).astype(BF16)
        imp_ref[...] = _dot(hi, ovl) + _dot(lo, ovl)

    nb_total = ncp // LANES
    last_visible = (s0 + tq - CMP_BLOCK) // CMP_STRIDE
    nb_needed = jnp.minimum(last_visible // LANES + 1, nb_total)
    for nb in range(1, nb_total + 1):
        pl.when(nb_needed == nb)(functools.partial(attend, nb))

    impT = imp_ref[...].T
    blk = lax.broadcasted_iota(jnp.int32, impT.shape, 0)
    cur = (s0 + lax.broadcasted_iota(jnp.int32, impT.shape, 1)) // SEL_BLOCK
    valid = blk <= cur
    forced = jnp.logical_or(blk == 0, jnp.logical_or(blk == cur, blk == cur - 1))
    sc = jnp.where(valid, jnp.where(forced, BIG, impT), NEG)
    for _ in range(SEL_TOPK):
        m = jnp.max(sc, axis=0, keepdims=True)
        idx = jnp.min(jnp.where(sc == m, blk, LANES), axis=0, keepdims=True)
        sc = jnp.where(blk == idx, BELOW_NEG, sc)
    biasT = jnp.where(sc == BELOW_NEG, 0.0, NEG)
    bias_ref[...] = biasT.T.astype(BF16)


def _cmp_topk(pb, kT, vcmp, B, S):
    G = NSA_KV_GROUPS
    T = B * S
    tq = min(S, CMP_TQ)
    nq = S // tq
    ncp = S // CMP_STRIDE
    gw = NSA_REP * HEAD_DIM
    return pl.pallas_call(
        functools.partial(_cmp_topk_kernel, tq=tq, ncp=ncp),
        grid=(B, G, nq),
        in_specs=[pl.BlockSpec((tq, gw), lambda b, g, i: (b * nq + i, g)),
                  pl.BlockSpec((1, 1, HEAD_DIM, ncp), lambda b, g, i: (b, g, 0, 0)),
                  pl.BlockSpec((1, 1, ncp, HEAD_DIM), lambda b, g, i: (b, g, 0, 0))],
        out_specs=[pl.BlockSpec((tq, gw), lambda b, g, i: (b * nq + i, g)),
                   pl.BlockSpec((tq, LANES), lambda b, g, i: (b * nq + i, g))],
        out_shape=[jax.ShapeDtypeStruct((T, NSA_HEADS * HEAD_DIM), F32),
                   jax.ShapeDtypeStruct((T, G * LANES), BF16)],
        scratch_shapes=[pltpu.VMEM((tq, LANES), F32)],
        compiler_params=_cparams("parallel", "parallel", "parallel"),
        name="nsa_cmp_topk",
    )(pb, kT, vcmp)


def _win_kernel(q_ref, k0_ref, k1_ref, k2_ref, v0_ref, v1_ref, v2_ref, o_ref, *, tq):
    i = pl.program_id(2)
    row = lax.broadcasted_iota(jnp.int32, (tq, tq), 0)
    col = lax.broadcasted_iota(jnp.int32, (tq, tq), 1)
    ones = jnp.ones((tq, LANES), BF16)

    def body(at_start):
        masks = [col > row, None, col <= row]
        if at_start:
            masks[0] = jnp.logical_and(masks[0], i >= 2)
            masks[1] = jnp.broadcast_to(i >= 1, (tq, tq))
        ks = [k0_ref[...], k1_ref[...], k2_ref[...]]
        vs = [jnp.concatenate([v_ref[...], ones], axis=1) for v_ref in (v0_ref, v1_ref, v2_ref)]
        for r in range(NSA_REP):
            sl = slice(r * HEAD_DIM, (r + 1) * HEAD_DIM)
            q = q_ref[:, sl]
            ss = []
            for k, mask in zip(ks, masks):
                s = _dot_nt(q, k)
                ss.append(s if mask is None else jnp.where(mask, s, NEG))
            m = jnp.max(jnp.maximum(jnp.maximum(ss[0], ss[1]), ss[2]), axis=-1, keepdims=True)
            acc = None
            for s, va in zip(ss, vs):
                part = _dot(jnp.exp2(s - m).astype(BF16), va)
                acc = part if acc is None else acc + part
            o_ref[:, sl] = acc[:, 0:HEAD_DIM] / acc[:, HEAD_DIM:2 * HEAD_DIM]

    pl.when(i >= 2)(functools.partial(body, False))
    pl.when(i < 2)(functools.partial(body, True))


def _window(pb, B, S, k_blk, v_blk):
    G = NSA_KV_GROUPS
    T = B * S
    tq = min(S // 2, WIN_TQ)
    assert WINDOW == 2 * tq
    nq = S // tq
    gw = NSA_REP * HEAD_DIM

    def kv_spec(col0, back):
        return pl.BlockSpec((tq, HEAD_DIM),
                            lambda b, g, i: (b * nq + jnp.maximum(i - back, 0), col0 + g))

    return pl.pallas_call(
        functools.partial(_win_kernel, tq=tq),
        grid=(B, G, nq),
        in_specs=[pl.BlockSpec((tq, gw), lambda b, g, i: (b * nq + i, g)),
                  kv_spec(k_blk, 2), kv_spec(k_blk, 1), kv_spec(k_blk, 0),
                  kv_spec(v_blk, 2), kv_spec(v_blk, 1), kv_spec(v_blk, 0)],
        out_specs=pl.BlockSpec((tq, gw), lambda b, g, i: (b * nq + i, g)),
        out_shape=jax.ShapeDtypeStruct((T, NSA_HEADS * HEAD_DIM), F32),
        compiler_params=_cparams("parallel", "parallel", "parallel"),
        name="nsa_window",
    )(pb, pb, pb, pb, pb, pb, pb)


def _sel_kernel(qi_ref, kj_ref, q_ref, bias_ref, k_ref, v_ref, o_ref, qa_ref, m_ref, acc_ref, *, tq, tk):
    n = pl.program_id(2)
    i = qi_ref[n]
    j = kj_ref[n]
    last_j = (i * tq + tq - 1) // tk

    @pl.when(j == 0)
    def _():
        for r in range(NSA_REP):
            rows = slice(r * tq, (r + 1) * tq)
            qa_ref[rows, 0:HEAD_DIM] = q_ref[:, r * HEAD_DIM:(r + 1) * HEAD_DIM]
            qa_ref[rows, HEAD_DIM:2 * HEAD_DIM] = bias_ref[...]
        m_ref[...] = jnp.full(m_ref.shape, NEG, F32)
        acc_ref[...] = jnp.zeros(acc_ref.shape, F32)

    def step(col0, ncols, diagonal):
        key = j * tk + col0 + lax.broadcasted_iota(jnp.int32, (ncols, LANES), 0)
        blk = lax.broadcasted_iota(jnp.int32, (ncols, LANES), 1)
        onehot = (key // SEL_BLOCK == blk).astype(BF16)
        ka = jnp.concatenate([k_ref[col0:col0 + ncols, :], onehot], axis=1)
        va = jnp.concatenate([v_ref[col0:col0 + ncols, :], jnp.ones((ncols, LANES), BF16)], axis=1)
        if diagonal:
            visible = (lax.broadcasted_iota(jnp.int32, (tq, tq), 1)
                       <= lax.broadcasted_iota(jnp.int32, (tq, tq), 0))
        for r in range(NSA_REP):
            rows = slice(r * tq, (r + 1) * tq)
            s = _dot_nt(qa_ref[rows, :], ka)
            if diagonal:
                s = jnp.where(visible, s, NEG)
            m_prev = m_ref[rows, :]
            m_new = jnp.maximum(m_prev, jnp.max(s, axis=-1, keepdims=True))
            alpha = jnp.exp2(m_prev - m_new)
            p = jnp.exp2(s - jnp.tile(m_new, (1, ncols // LANES)))
            acc_ref[rows, :] = jnp.tile(alpha, (1, 2)) * acc_ref[rows, :] + _dot(p.astype(BF16), va)
            m_ref[rows, :] = m_new

    @pl.when(j < last_j)
    def _():
        step(0, tk, False)

    for sub in range(tk // tq):
        @pl.when(jnp.logical_and(j == last_j, i % (tk // tq) == sub))
        def _(sub=sub):
            if sub > 0:
                step(0, sub * tq, False)
            step(sub * tq, tq, True)

    @pl.when(j == last_j)
    def _():
        for r in range(NSA_REP):
            rows = slice(r * tq, (r + 1) * tq)
            o_ref[:, r * HEAD_DIM:(r + 1) * HEAD_DIM] = (acc_ref[rows, 0:HEAD_DIM]
                                                         / acc_ref[rows, HEAD_DIM:2 * HEAD_DIM])


def _selection(pb, bias, B, S, k_blk, v_blk):
    G = NSA_KV_GROUPS
    T = B * S
    tq = min(S, SEL_TQ)
    tk = min(S, SEL_TK)
    assert tk % tq == 0
    nq = S // tq
    nk = S // tk
    gw = NSA_REP * HEAD_DIM
    pairs = [(i, j) for i in range(nq) for j in range((i * tq + tq - 1) // tk + 1)]
    qi = jnp.asarray([p[0] for p in pairs], jnp.int32)
    kj = jnp.asarray([p[1] for p in pairs], jnp.int32)

    def q_map(b, g, n, qi_ref, kj_ref):
        return (b * nq + qi_ref[n], g)

    def kv_spec(col0):
        return pl.BlockSpec((tk, HEAD_DIM),
                            lambda b, g, n, qi_ref, kj_ref: (b * nk + kj_ref[n], col0 + g))

    grid_spec = pltpu.PrefetchScalarGridSpec(
        num_scalar_prefetch=2,
        grid=(B, G, len(pairs)),
        in_specs=[pl.BlockSpec((tq, gw), q_map), pl.BlockSpec((tq, LANES), q_map),
                  kv_spec(k_blk), kv_spec(v_blk)],
        out_specs=pl.BlockSpec((tq, gw), q_map),
        scratch_shapes=[pltpu.VMEM((NSA_REP * tq, 2 * HEAD_DIM), BF16),
                        pltpu.VMEM((NSA_REP * tq, LANES), F32),
                        pltpu.VMEM((NSA_REP * tq, 2 * HEAD_DIM), F32)])
    return pl.pallas_call(
        functools.partial(_sel_kernel, tq=tq, tk=tk),
        grid_spec=grid_spec,
        out_shape=jax.ShapeDtypeStruct((T, NSA_HEADS * HEAD_DIM), F32),
        compiler_params=_cparams("parallel", "parallel", "arbitrary"),
        name="nsa_selection",
    )(qi, kj, pb, bias, pb, pb)


def _nsa_out_kernel(x_ref, oc_ref, os_ref, ow_ref, z_ref, gl_ref, w_ref, o_ref):
    sig = jax.nn.sigmoid(gl_ref[...])
    H = NSA_HEADS
    acc = x_ref[...]
    for hp in range(H // 2):
        parts = []
        for h in (2 * hp, 2 * hp + 1):
            sl = slice(h * HEAD_DIM, (h + 1) * HEAD_DIM)
            o = (sig[:, h:h + 1] * oc_ref[:, sl] + sig[:, H + h:H + h + 1] * os_ref[:, sl]
                 + sig[:, 2 * H + h:2 * H + h + 1] * ow_ref[:, sl])
            parts.append((o * _silu(z_ref[:, sl])).astype(BF16))
        rows = slice(2 * hp * HEAD_DIM, (2 * hp + 2) * HEAD_DIM)
        acc = acc + _dot(jnp.concatenate(parts, axis=1), w_ref[rows, :])
    o_ref[...] = acc


def _nsa_out(x, o_cmp, o_sel, o_win, pf, z_blk, gl_blk, w_out):
    T, D = x.shape
    W = o_cmp.shape[1]
    tm = min(T, OUT_TM)
    row = lambda i: (i, 0)
    return pl.pallas_call(
        _nsa_out_kernel,
        grid=(T // tm,),
        in_specs=[pl.BlockSpec((tm, D), row),
                  pl.BlockSpec((tm, W), row), pl.BlockSpec((tm, W), row), pl.BlockSpec((tm, W), row),
                  pl.BlockSpec((tm, W), lambda i: (i, z_blk)),
                  pl.BlockSpec((tm, LANES), lambda i: (i, gl_blk)),
                  pl.BlockSpec((W, D), lambda i: (0, 0))],
        out_specs=pl.BlockSpec((tm, D), row),
        out_shape=jax.ShapeDtypeStruct((T, D), F32),
        compiler_params=_cparams("parallel"),
        name="nsa_out",
    )(x, o_cmp, o_sel, o_win, pf, pf, w_out)


def _nsa_layer(x, B, S, tabs, norm_w, w_in, ck_pos, ck_w1, ck_w2, cv_pos, cv_w1, cv_w2, w_out):
    H, G, d = NSA_HEADS, NSA_KV_GROUPS, HEAD_DIM
    kvw = G * d
    c = [0, H * d]
    for _ in range(6):
        c.append(c[-1] + kvw)
    c.append(c[-1] + 3 * H)
    c.append(c[-1] + H * d)
    seg = lambda n: w_in[:, c[n]:c[n + 1]]
    wq, wkc, wvc, wks, wvs, wkw, wvw, wgl, wz = [seg(n) for n in range(9)]
    wq, wkc, wks, wkw = [_permute_head_dims(t, 1) for t in (wq, wkc, wks, wkw)]
    ck_pos = _permute_head_dims(ck_pos, 1)
    ck_w1 = _permute_head_dims(ck_w1, 1)
    ck_w2 = _permute_head_dims(ck_w2, 1)
    w_b = jnp.concatenate([wq, wks, wkw, wvs, wvw], axis=1).astype(BF16)
    w_f = jnp.concatenate([wz, wkc, wvc, wgl, jnp.zeros((w_in.shape[0], LANES - 3 * H), w_in.dtype)],
                          axis=1).astype(BF16)
    pb = _norm_proj(x, norm_w, w_b, BF16, 1024, rope=(0, H + 2 * G), tabs=tabs, q_heads=H)
    pf = _norm_proj(x, norm_w, w_f, F32, 640, rope=(H, H + G), tabs=tabs)
    ks_blk, kw_blk, vs_blk, vw_blk = H, H + G, H + 2 * G, H + 3 * G
    kc_blk, vc_blk, gl_blk = H, H + G, H + 2 * G

    kT, vcmp = _compress(pf, B, S, kc_blk, vc_blk, ck_pos, ck_w1, ck_w2, cv_pos, cv_w1, cv_w2)
    o_cmp, bias = _cmp_topk(pb, kT, vcmp, B, S)
    o_win = _window(pb, B, S, kw_blk, vw_blk)
    o_sel = _selection(pb, bias, B, S, ks_blk, vs_blk)
    return _nsa_out(x, o_cmp, o_sel, o_win, pf, 0, gl_blk, w_out.astype(BF16))


def _hgrn_kernel(q_ref, f_ref, i_ref, z_ref, lbl_ref, gw_ref, o_ref,
                 st_ref, *, C, sub, hb, layer):
    n = pl.program_id(2)

    @pl.when(n == 0)
    def _():
        st_ref[...] = jnp.zeros(st_ref.shape, F32)

    r0 = lax.broadcasted_iota(jnp.int32, (C, C), 0)
    r1 = lax.broadcasted_iota(jnp.int32, (C, C), 1)
    tri = (r0 >= r1).astype(BF16)
    dsub = sub // 2
    drow = lax.broadcasted_iota(jnp.int32, (C // dsub, dsub, HEAD_DIM), 1)
    row_id = lax.broadcasted_iota(jnp.int32, (C, HEAD_DIM), 0)

    for h in range(hb):
        cs = slice(h * HEAD_DIM, (h + 1) * HEAD_DIM)
        lg = lbl_ref[:, cs]
        e = jnp.exp(lg - jnp.max(lg, axis=0, keepdims=True))
        p = e / jnp.sum(e, axis=0, keepdims=True)
        lb = jnp.sum(p[0:layer + 1, :], axis=0, keepdims=True) - p[0:1, :]

        fl = f_ref[:, cs]
        t = jnp.exp(-jnp.abs(fl))
        r = 1.0 / (1.0 + t)
        pos = fl >= 0.0
        sig_p = jnp.where(pos, r, t * r)
        sig_n = jnp.where(pos, t * r, r)
        k = (1.0 - lb) * sig_n
        lc = jnp.minimum(fl, 0.0) - jnp.log(1.0 + t) + jnp.log1p(-lb)
        g = jnp.maximum(jnp.log(lb + (1.0 - lb) * sig_p), lc)
        qs = _silu(q_ref[:, cs])
        iv = i_ref[:, cs]
        ib = iv.astype(BF16)

        g1 = g.astype(BF16)
        rem = g - g1.astype(F32)
        g2 = rem.astype(BF16)
        g3 = (rem - g2.astype(F32)).astype(BF16)
        b = _dot(tri, g1) + _dot(tri, g2) + _dot(tri, g3)

        stT = st_ref[h]
        o_inter = _dot_nt((qs * jnp.exp(b)).astype(BF16), stT.astype(BF16))

        nd = C // dsub
        b3 = b.reshape(nd, dsub, HEAD_DIM)
        q3 = qs.reshape(nd, dsub, HEAD_DIM)
        k3 = k.reshape(nd, dsub, HEAD_DIM)
        i3 = iv.reshape(nd, dsub, HEAD_DIM)
        o3 = jnp.zeros((nd, dsub, HEAD_DIM), F32)
        for s in range(dsub):
            w = jnp.where(drow >= s, jnp.exp(b3 - b3[:, s:s + 1, :]), 0.0)
            col = jnp.sum(q3 * w * k3[:, s:s + 1, :], axis=-1, keepdims=True)
            o3 = o3 + col * i3[:, s:s + 1, :]

        att = None
        bsz = 2 * dsub
        while bsz <= C:
            bl = b.reshape(C // bsz, bsz, HEAD_DIM)
            bmid = jnp.broadcast_to(bl[:, bsz // 2 - 1:bsz // 2, :], bl.shape).reshape(C, HEAD_DIM)
            second = (row_id % bsz) >= bsz // 2
            qk = jnp.where(second, qs, k) * jnp.exp(-jnp.abs(b - bmid))
            qh = jnp.where(second, qk, 0.0).astype(BF16)
            kh = jnp.where(second, 0.0, qk).astype(BF16)
            a = _dot_nt(qh, kh)
            if bsz < C:
                a = jnp.where(r0 // bsz == r1 // bsz, a, 0.0)
            att = a if att is None else att + a
            bsz *= 2
        o = o_inter + o3.reshape(C, HEAD_DIM) + _dot(att.astype(BF16), ib)

        b_last = b[C - 1:C, :]
        kd = (k * jnp.exp(b_last - b)).astype(BF16)
        st_ref[h] = stT * jnp.exp(b_last) + _dot(iv.T.astype(BF16), kd)

        o = o * lax.rsqrt(jnp.mean(o * o, axis=-1, keepdims=True) + RMS_EPS) * gw_ref[...]
        o_ref[:, cs] = (o * _silu(z_ref[:, cs])).astype(BF16)


def _hgrn_recurrence(ph, lb_logits, gnorm_w, B, S, layer):
    H = HGRN_HEADS
    T = B * S
    C = min(S, HGRN_CHUNK)
    nc = S // C
    NL = lb_logits.shape[0]
    hb = HGRN_HEADS_PER_STEP
    hg = H // hb
    w = hb * HEAD_DIM

    def col(c0):
        return pl.BlockSpec((C, w), lambda b, h, n: (b * nc + n, c0 + h))

    return pl.pallas_call(
        functools.partial(_hgrn_kernel, C=C, sub=HGRN_SUB, hb=hb, layer=layer),
        grid=(B, hg, nc),
        in_specs=[col(0), col(hg), col(2 * hg), col(3 * hg),
                  pl.BlockSpec((NL, w), lambda b, h, n: (0, h)),
                  pl.BlockSpec((1, HEAD_DIM), lambda b, h, n: (0, 0))],
        out_specs=pl.BlockSpec((C, w), lambda b, h, n: (b * nc + n, h)),
        out_shape=jax.ShapeDtypeStruct((T, H * HGRN_I_DIM), BF16),
        scratch_shapes=[pltpu.VMEM((hb, HGRN_I_DIM, HGRN_F_DIM), F32)],
        compiler_params=_cparams("parallel", "parallel", "arbitrary"),
        name="hgrn_recurrence",
    )(ph, ph, ph, ph, lb_logits, gnorm_w.reshape(1, HEAD_DIM))


def _hgrn_out_kernel(x_ref, og_ref, w_ref, fw_ref, o_ref, *, final_norm):
    y = x_ref[...] + _dot(og_ref[...], w_ref[...])
    if final_norm:
        ms = jnp.mean(y * y, axis=-1, keepdims=True)
        y = y * lax.rsqrt(ms + RMS_EPS) * fw_ref[...]
    o_ref[...] = y


def _hgrn_out(x, og, w_out, final_w, final_norm):
    T, D = x.shape
    W = og.shape[1]
    tm = min(T, 2 * OUT_TM)
    row = lambda i: (i, 0)
    return pl.pallas_call(
        functools.partial(_hgrn_out_kernel, final_norm=final_norm),
        grid=(T // tm,),
        in_specs=[pl.BlockSpec((tm, D), row), pl.BlockSpec((tm, W), row),
                  pl.BlockSpec((W, D), lambda i: (0, 0)),
                  pl.BlockSpec((1, D), lambda i: (0, 0))],
        out_specs=pl.BlockSpec((tm, D), row),
        out_shape=jax.ShapeDtypeStruct((T, D), F32),
        compiler_params=_cparams("parallel"),
        name="hgrn_out",
    )(x, og, w_out, final_w.reshape(1, D))


def _hgrn_layer(x, B, S, norm_w, w_in, lb_logits, gnorm_w, w_out, layer, final_w, final_norm):
    ph = _norm_proj(x, norm_w, w_in.astype(BF16), F32, 1024)
    og = _hgrn_recurrence(ph, lb_logits, gnorm_w, B, S, layer)
    return _hgrn_out(x, og, w_out.astype(BF16), final_w, final_norm)


def kernel(x, positions, norm_w, final_norm_w, nsa_w_in, nsa_ck_pos, nsa_ck_w1, nsa_ck_w2,
           nsa_cv_pos, nsa_cv_w1, nsa_cv_w2, nsa_w_out, hgrn_w_in, hgrn_lb_logits,
           hgrn_gnorm_w, hgrn_w_out):
    B, S, D = x.shape
    depth = norm_w.shape[0]
    xf = x.reshape(B * S, D)
    tabs = _rope_tables(positions)
    for layer in range(depth):
        j = layer // 2
        if layer % 2 == 0:
            xf = _nsa_layer(xf, B, S, tabs, norm_w[layer], nsa_w_in[j], nsa_ck_pos[j], nsa_ck_w1[j],
                            nsa_ck_w2[j], nsa_cv_pos[j], nsa_cv_w1[j], nsa_cv_w2[j], nsa_w_out[j])
        else:
            xf = _hgrn_layer(xf, B, S, norm_w[layer], hgrn_w_in[j], hgrn_lb_logits, hgrn_gnorm_w[j],
                             hgrn_w_out[j], j, final_norm_w, layer == depth - 1)
    return xf.reshape(B, S, D)
```

```python
import functools
import math

import jax
import jax.numpy as jnp
from jax import lax
from jax.experimental import pallas as pl
from jax.experimental.pallas import tpu as pltpu

F32 = jnp.float32
BF16 = jnp.bfloat16

RMS_EPS = 1e-6
NEG = -1e30
BELOW_NEG = -3e38
HEAD_DIM = 128
NSA_HEADS = 16
NSA_KV_GROUPS = 4
NSA_REP = NSA_HEADS // NSA_KV_GROUPS
CMP_BLOCK = 32
CMP_STRIDE = 16
SEL_BLOCK = 64
SEL_TOPK = 16
WINDOW = 512
ROPE_THETA = 500000.0
ROPE_DIM = HEAD_DIM // 4
HGRN_HEADS = 16
HGRN_F_DIM = 128
HGRN_I_DIM = 128

LANES = 128
VMEM_LIMIT_BYTES = 56 * 1024 * 1024

ATTN_SCALE = HEAD_DIM ** -0.5
EXP2_SCALE = ATTN_SCALE * math.log2(math.e)

PROJ_TM = 1024
PROJ_CHUNK_HEADS = 2
CMP_TQ = 256
WIN_TQ = 256
SEL_TQ = 512
SEL_TK = 1024
OUT_TM = 256
HGRN_CHUNK = 128
HGRN_SUB = 16
HGRN_HEADS_PER_STEP = 4


def _cparams(*sem):
    return pltpu.CompilerParams(dimension_semantics=sem, vmem_limit_bytes=VMEM_LIMIT_BYTES)


def _dot(a, b):
    return jnp.dot(a, b, preferred_element_type=F32)


def _dot_nt(a, b):
    return lax.dot_general(a, b, (((1,), (1,)), ((), ())), preferred_element_type=F32)


def _silu(v):
    return v * jax.nn.sigmoid(v)


_HALF = ROPE_DIM // 2
_PARTNER_SHIFT = LANES // 2
ROPE_PERM = (list(range(_HALF))
             + list(range(ROPE_DIM, ROPE_DIM + _PARTNER_SHIFT - _HALF))
             + list(range(_HALF, ROPE_DIM))
             + list(range(ROPE_DIM + _PARTNER_SHIFT - _HALF, HEAD_DIM)))


def _permute_head_dims(w, axis):
    axis = axis % w.ndim
    n = w.shape[axis] // HEAD_DIM
    shape = w.shape[:axis] + (n, HEAD_DIM) + w.shape[axis + 1:]
    wr = w.reshape(shape)
    cuts = [(0, _HALF), (ROPE_DIM, ROPE_DIM + _PARTNER_SHIFT - _HALF), (_HALF, ROPE_DIM),
            (ROPE_DIM + _PARTNER_SHIFT - _HALF, HEAD_DIM)]
    parts = [lax.slice_in_dim(wr, lo, hi, axis=axis + 1) for lo, hi in cuts]
    return jnp.concatenate(parts, axis=axis + 1).reshape(w.shape)


def _rope_table_kernel(pos_ref, inv_ref, cos_ref, sin_ref):
    ang = pos_ref[...] * inv_ref[...]
    lane = lax.broadcasted_iota(jnp.int32, ang.shape, 1)
    s = jnp.sin(ang)
    cos_ref[...] = jnp.cos(ang)
    sin_ref[...] = jnp.where(lane < _HALF, -s, s)


def _rope_tables(positions):
    T = positions.size
    tm = min(T, 2048)
    inv = ROPE_THETA ** (-jnp.arange(_HALF, dtype=F32) / _HALF)
    inv_pat = jnp.concatenate([inv, jnp.zeros((_PARTNER_SHIFT - _HALF,), F32), inv,
                               jnp.zeros((LANES - _PARTNER_SHIFT - _HALF,), F32)]).reshape(1, LANES)
    pos = positions.astype(F32).reshape(T, 1)
    return pl.pallas_call(
        _rope_table_kernel,
        grid=(T // tm,),
        in_specs=[pl.BlockSpec((tm, 1), lambda i: (i, 0)),
                  pl.BlockSpec((1, LANES), lambda i: (0, 0))],
        out_specs=[pl.BlockSpec((tm, LANES), lambda i: (i, 0))] * 2,
        out_shape=[jax.ShapeDtypeStruct((T, LANES), F32)] * 2,
        compiler_params=_cparams("parallel"),
        name="rope_tables",
    )(pos, inv_pat)


def _norm_proj_kernel(*refs, n_tiles, heads_per_tile, rope_lo, rope_hi, q_heads):
    if rope_hi > rope_lo:
        x_ref, nw_ref, w_ref, cos_ref, sin_ref, o_ref, hn_ref = refs
    else:
        x_ref, nw_ref, w_ref, o_ref, hn_ref = refs
    j = pl.program_id(1)

    @pl.when(j == 0)
    def _():
        x = x_ref[...]
        ms = jnp.mean(x * x, axis=-1, keepdims=True)
        hn_ref[...] = (x * lax.rsqrt(ms + RMS_EPS) * nw_ref[...]).astype(BF16)

    def tile_body(kinds):
        h = 0
        while h < heads_per_tile:
            nh = min(PROJ_CHUNK_HEADS, heads_per_tile - h)
            acc = _dot(hn_ref[...], w_ref[:, h * HEAD_DIM:(h + nh) * HEAD_DIM])
            for hh in range(nh):
                sl = slice((h + hh) * HEAD_DIM, (h + hh + 1) * HEAD_DIM)
                t = acc[:, hh * HEAD_DIM:(hh + 1) * HEAD_DIM]
                kind = kinds[h + hh]
                if kind != 'plain':
                    t = t * cos_ref[...] + pltpu.roll(t, _PARTNER_SHIFT, axis=1) * sin_ref[...]
                if kind == 'rope_q':
                    t = t * EXP2_SCALE
                o_ref[:, sl] = t.astype(o_ref.dtype)
            h += nh

    def kind_of(head):
        if rope_lo <= head < rope_hi:
            return 'rope_q' if head < q_heads else 'rope'
        return 'plain'

    patterns = {}
    for jj in range(n_tiles):
        kinds = tuple(kind_of(jj * heads_per_tile + h) for h in range(heads_per_tile))
        patterns.setdefault(kinds, []).append(jj)
    for kinds, tiles in patterns.items():
        cond = functools.reduce(jnp.logical_or, [j == jj for jj in tiles])
        pl.when(cond)(functools.partial(tile_body, kinds))


def _norm_proj(x, norm_w, w, out_dtype, tn, rope=None, tabs=None, q_heads=0):
    T, D = x.shape
    N = w.shape[1]
    tm = min(T, PROJ_TM)
    rope_lo, rope_hi = rope if rope is not None else (0, 0)
    in_specs = [pl.BlockSpec((tm, D), lambda i, j: (i, 0)),
                pl.BlockSpec((1, D), lambda i, j: (0, 0)),
                pl.BlockSpec((D, tn), lambda i, j: (0, j))]
    args = [x, norm_w.reshape(1, D), w]
    if rope is not None:
        in_specs += [pl.BlockSpec((tm, LANES), lambda i, j: (i, 0))] * 2
        args += list(tabs)
    return pl.pallas_call(
        functools.partial(_norm_proj_kernel, n_tiles=N // tn, heads_per_tile=tn // HEAD_DIM,
                          rope_lo=rope_lo, rope_hi=rope_hi, q_heads=q_heads),
        grid=(T // tm, N // tn),
        in_specs=in_specs,
        out_specs=pl.BlockSpec((tm, tn), lambda i, j: (i, j)),
        out_shape=jax.ShapeDtypeStruct((T, N), out_dtype),
        scratch_shapes=[pltpu.VMEM((tm, D), BF16)],
        compiler_params=_cparams("parallel", "arbitrary"),
        name="norm_proj",
    )(*args)


def _compress_kernel(kc_ref, vc_ref, kpos_ref, kw1_ref, kw2_ref, vpos_ref, vw1_ref, vw2_ref,
                     kT_ref, v_ref, shift_ref, *, ncp):
    half_blk = CMP_BLOCK // 2

    def comp(t_ref, pos_ref, w1_ref, w2_ref):
        a = jnp.zeros((ncp, HEAD_DIM), F32)
        bm = jnp.zeros((ncp, HEAD_DIM), F32)
        for l in range(half_blk):
            xl = t_ref[pl.ds(l, ncp, stride=CMP_STRIDE), :]
            a = a + _dot((xl + pos_ref[l:l + 1, :]).astype(BF16), w1_ref[l])
            bm = bm + _dot((xl + pos_ref[half_blk + l:half_blk + l + 1, :]).astype(BF16),
                           w1_ref[half_blk + l])
        shift_ref[pl.ds(0, ncp), :] = bm
        shift_ref[pl.ds(ncp, 8), :] = jnp.zeros((8, HEAD_DIM), F32)
        hid = _silu(a + shift_ref[pl.ds(1, ncp), :])
        out = _dot(hid.astype(BF16), w2_ref[...])
        row = lax.broadcasted_iota(jnp.int32, out.shape, 0)
        return jnp.where(row < ncp - 1, out, 0.0)

    kT_ref[0, 0] = comp(kc_ref, kpos_ref, kw1_ref, kw2_ref).T.astype(BF16)
    v_ref[0, 0] = comp(vc_ref, vpos_ref, vw1_ref, vw2_ref).astype(BF16)


def _compress(pf, B, S, kc_blk, vc_blk, ck_pos, ck_w1, ck_w2, cv_pos, cv_w1, cv_w2):
    G = NSA_KV_GROUPS
    ncp = S // CMP_STRIDE
    full2 = lambda b, g: (0, 0)
    full3 = lambda b, g: (0, 0, 0)
    return pl.pallas_call(
        functools.partial(_compress_kernel, ncp=ncp),
        grid=(B, G),
        in_specs=[pl.BlockSpec((S, HEAD_DIM), lambda b, g: (b, kc_blk + g)),
                  pl.BlockSpec((S, HEAD_DIM), lambda b, g: (b, vc_blk + g)),
                  pl.BlockSpec((CMP_BLOCK, HEAD_DIM), full2),
                  pl.BlockSpec((CMP_BLOCK, HEAD_DIM, HEAD_DIM), full3),
                  pl.BlockSpec((HEAD_DIM, HEAD_DIM), full2),
                  pl.BlockSpec((CMP_BLOCK, HEAD_DIM), full2),
                  pl.BlockSpec((CMP_BLOCK, HEAD_DIM, HEAD_DIM), full3),
                  pl.BlockSpec((HEAD_DIM, HEAD_DIM), full2)],
        out_specs=[pl.BlockSpec((1, 1, HEAD_DIM, ncp), lambda b, g: (b, g, 0, 0)),
                   pl.BlockSpec((1, 1, ncp, HEAD_DIM), lambda b, g: (b, g, 0, 0))],
        out_shape=[jax.ShapeDtypeStruct((B, G, HEAD_DIM, ncp), BF16),
                   jax.ShapeDtypeStruct((B, G, ncp, HEAD_DIM), BF16)],
        scratch_shapes=[pltpu.VMEM((ncp + 8, HEAD_DIM), F32)],
        compiler_params=_cparams("parallel", "parallel"),
        name="nsa_compress",
    )(pf, pf, ck_pos, ck_w1.astype(BF16), ck_w2.astype(BF16),
      cv_pos, cv_w1.astype(BF16), cv_w2.astype(BF16))


def _cmp_topk_kernel(q_ref, kT_ref, v_ref, o_ref, bias_ref, imp_ref, *, tq, ncp):
    i = pl.program_id(2)
    s0 = i * tq
    t_col = s0 + lax.broadcasted_iota(jnp.int32, (tq, 1), 0)
    any_visible = (t_col >= CMP_BLOCK - 1).astype(F32)

    def attend(nb):
        nc = nb * LANES
        n_row = lax.broadcasted_iota(jnp.int32, (1, nc), 1)
        mc = jnp.logical_and(n_row * CMP_STRIDE + (CMP_BLOCK - 1) <= t_col, n_row < ncp - 1)
        kT = kT_ref[0, 0, :, 0:nc]
        v = v_ref[0, 0, 0:nc, :]
        psum = jnp.zeros((tq, nc), F32)
        for r in range(NSA_REP):
            sl = slice(r * HEAD_DIM, (r + 1) * HEAD_DIM)
            s = jnp.where(mc, _dot(q_ref[:, sl], kT), NEG)
            m = jnp.max(s, axis=-1, keepdims=True)
            e = jnp.exp2(s - m)
            l = jnp.sum(e, axis=-1, keepdims=True)
            pc = e * (any_visible / l)
            o_ref[:, sl] = _dot(pc.astype(BF16), v)
            psum = psum + pc
        n_col = lax.broadcasted_iota(jnp.int32, (nc, LANES), 0) * CMP_STRIDE
        k_row = lax.broadcasted_iota(jnp.int32, (nc, LANES), 1) * SEL_BLOCK
        ovl = jnp.logical_and(n_col < k_row + SEL_BLOCK, n_col + CMP_BLOCK > k_row).astype(BF16)
        hi = psum.astype(BF16)
        lo = (psum - hi.astype(F32)).astype(BF16)
        imp_ref[...] = _dot(hi, ovl) + _dot(lo, ovl)

    nb_total = ncp // LANES
    last_visible = (s0 + tq - CMP_BLOCK) // CMP_STRIDE
    nb_needed = jnp.minimum(last_visible // LANES + 1, nb_total)
    for nb in range(1, nb_total + 1):
        pl.when(nb_needed == nb)(functools.partial(attend, nb))

    blk = lax.broadcasted_iota(jnp.int32, (LANES, LANES), 0)
    lane = lax.broadcasted_iota(jnp.int32, (LANES, LANES), 1)
    groups = []
    for c in range(0, tq, LANES):
        impT = imp_ref[c:c + LANES, :].T
        cur = (s0 + c + lane) // SEL_BLOCK
        forced = jnp.logical_or(blk == 0, jnp.logical_or(blk == cur, blk == cur - 1))
        groups.append(jnp.where(forced, BELOW_NEG, jnp.where(blk <= cur, impT, NEG)))
    for _ in range(SEL_TOPK - 3):
        nxt = []
        for sc in groups:
            m = jnp.max(sc, axis=0, keepdims=True)
            idx = jnp.min(jnp.where(sc == m, blk, LANES), axis=0, keepdims=True)
            nxt.append(jnp.where(blk == idx, BELOW_NEG, sc))
        groups = nxt
    for g, sc in enumerate(groups):
        biasT = jnp.where(sc == BELOW_NEG, 0.0, NEG)
        bias_ref[g * LANES:(g + 1) * LANES, :] = biasT.T.astype(BF16)


def _cmp_topk(pb, kT, vcmp, B, S):
    G = NSA_KV_GROUPS
    T = B * S
    tq = min(S, CMP_TQ)
    nq = S // tq
    ncp = S // CMP_STRIDE
    gw = NSA_REP * HEAD_DIM
    return pl.pallas_call(
        functools.partial(_cmp_topk_kernel, tq=tq, ncp=ncp),
        grid=(B, G, nq),
        in_specs=[pl.BlockSpec((tq, gw), lambda b, g, i: (b * nq + i, g)),
                  pl.BlockSpec((1, 1, HEAD_DIM, ncp), lambda b, g, i: (b, g, 0, 0)),
                  pl.BlockSpec((1, 1, ncp, HEAD_DIM), lambda b, g, i: (b, g, 0, 0))],
        out_specs=[pl.BlockSpec((tq, gw), lambda b, g, i: (b * nq + i, g)),
                   pl.BlockSpec((tq, LANES), lambda b, g, i: (b * nq + i, g))],
        out_shape=[jax.ShapeDtypeStruct((T, NSA_HEADS * HEAD_DIM), F32),
                   jax.ShapeDtypeStruct((T, G * LANES), BF16)],
        scratch_shapes=[pltpu.VMEM((tq, LANES), F32)],
        compiler_params=_cparams("parallel", "parallel", "parallel"),
        name="nsa_cmp_topk",
    )(pb, kT, vcmp)


def _win_kernel(q_ref, k0_ref, k1_ref, k2_ref, v0_ref, v1_ref, v2_ref, o_ref, *, tq):
    i = pl.program_id(2)
    row = lax.broadcasted_iota(jnp.int32, (tq, tq), 0)
    col = lax.broadcasted_iota(jnp.int32, (tq, tq), 1)
    ones = jnp.ones((tq, LANES), BF16)

    def body(at_start):
        masks = [col > row, None, col <= row]
        if at_start:
            masks[0] = jnp.logical_and(masks[0], i >= 2)
            masks[1] = jnp.broadcast_to(i >= 1, (tq, tq))
        ks = [k0_ref[...], k1_ref[...], k2_ref[...]]
        vs = [jnp.concatenate([v_ref[...], ones], axis=1) for v_ref in (v0_ref, v1_ref, v2_ref)]
        for r in range(NSA_REP):
            sl = slice(r * HEAD_DIM, (r + 1) * HEAD_DIM)
            q = q_ref[:, sl]
            ss = []
            for k, mask in zip(ks, masks):
                s = _dot_nt(q, k)
                ss.append(s if mask is None else jnp.where(mask, s, NEG))
            m = jnp.max(jnp.maximum(jnp.maximum(ss[0], ss[1]), ss[2]), axis=-1, keepdims=True)
            acc = None
            for s, va in zip(ss, vs):
                part = _dot(jnp.exp2(s - m).astype(BF16), va)
                acc = part if acc is None else acc + part
            o_ref[:, sl] = acc[:, 0:HEAD_DIM] / acc[:, HEAD_DIM:2 * HEAD_DIM]

    pl.when(i >= 2)(functools.partial(body, False))
    pl.when(i < 2)(functools.partial(body, True))


def _window(pb, B, S, k_blk, v_blk):
    G = NSA_KV_GROUPS
    T = B * S
    tq = min(S // 2, WIN_TQ)
    assert WINDOW == 2 * tq
    nq = S // tq
    gw = NSA_REP * HEAD_DIM

    def kv_spec(col0, back):
        return pl.BlockSpec((tq, HEAD_DIM),
                            lambda b, g, i: (b * nq + jnp.maximum(i - back, 0), col0 + g))

    return pl.pallas_call(
        functools.partial(_win_kernel, tq=tq),
        grid=(B, G, nq),
        in_specs=[pl.BlockSpec((tq, gw), lambda b, g, i: (b * nq + i, g)),
                  kv_spec(k_blk, 2), kv_spec(k_blk, 1), kv_spec(k_blk, 0),
                  kv_spec(v_blk, 2), kv_spec(v_blk, 1), kv_spec(v_blk, 0)],
        out_specs=pl.BlockSpec((tq, gw), lambda b, g, i: (b * nq + i, g)),
        out_shape=jax.ShapeDtypeStruct((T, NSA_HEADS * HEAD_DIM), F32),
        compiler_params=_cparams("parallel", "parallel", "parallel"),
        name="nsa_window",
    )(pb, pb, pb, pb, pb, pb, pb)


def _sel_kernel(qi_ref, kj_ref, q_ref, bias_ref, k_ref, v_ref, o_ref, qa_ref, m_ref, acc_ref, *, tq, tk):
    n = pl.program_id(2)
    i = qi_ref[n]
    j = kj_ref[n]
    last_j = (i * tq + tq - 1) // tk

    @pl.when(j == 0)
    def _():
        for r in range(NSA_REP):
            rows = slice(r * tq, (r + 1) * tq)
            qa_ref[rows, 0:HEAD_DIM] = q_ref[:, r * HEAD_DIM:(r + 1) * HEAD_DIM]
            qa_ref[rows, HEAD_DIM:2 * HEAD_DIM] = bias_ref[...]
        m_ref[...] = jnp.full(m_ref.shape, NEG, F32)
        acc_ref[...] = jnp.zeros(acc_ref.shape, F32)

    def step(col0, ncols, diagonal):
        key = j * tk + col0 + lax.broadcasted_iota(jnp.int32, (ncols, LANES), 0)
        blk = lax.broadcasted_iota(jnp.int32, (ncols, LANES), 1)
        onehot = (key // SEL_BLOCK == blk).astype(BF16)
        ka = jnp.concatenate([k_ref[col0:col0 + ncols, :], onehot], axis=1)
        va = jnp.concatenate([v_ref[col0:col0 + ncols, :], jnp.ones((ncols, LANES), BF16)], axis=1)
        if diagonal:
            visible = (lax.broadcasted_iota(jnp.int32, (tq, tq), 1)
                       <= lax.broadcasted_iota(jnp.int32, (tq, tq), 0))
        def scores(r):
            s = _dot_nt(qa_ref[r * tq:(r + 1) * tq, :], ka)
            return jnp.where(visible, s, NEG) if diagonal else s

        s_next = scores(0)
        for r in range(NSA_REP):
            rows = slice(r * tq, (r + 1) * tq)
            s = s_next
            if r + 1 < NSA_REP:
                s_next = scores(r + 1)
            m_prev = m_ref[rows, :]
            m_new = jnp.maximum(m_prev, jnp.max(s, axis=-1, keepdims=True))
            alpha = jnp.exp2(m_prev - m_new)
            p = jnp.exp2(s - jnp.tile(m_new, (1, ncols // LANES)))
            acc_ref[rows, :] = jnp.tile(alpha, (1, 2)) * acc_ref[rows, :] + _dot(p.astype(BF16), va)
            m_ref[rows, :] = m_new

    @pl.when(j < last_j)
    def _():
        step(0, tk, False)

    for sub in range(tk // tq):
        @pl.when(jnp.logical_and(j == last_j, i % (tk // tq) == sub))
        def _(sub=sub):
            if sub > 0:
                step(0, sub * tq, False)
            step(sub * tq, tq, True)

    @pl.when(j == last_j)
    def _():
        for r in range(NSA_REP):
            rows = slice(r * tq, (r + 1) * tq)
            o_ref[:, r * HEAD_DIM:(r + 1) * HEAD_DIM] = (acc_ref[rows, 0:HEAD_DIM]
                                                         / acc_ref[rows, HEAD_DIM:2 * HEAD_DIM])


def _selection(pb, bias, B, S, k_blk, v_blk):
    G = NSA_KV_GROUPS
    T = B * S
    tq = min(S, SEL_TQ)
    tk = min(S, SEL_TK)
    assert tk % tq == 0
    nq = S // tq
    nk = S // tk
    gw = NSA_REP * HEAD_DIM
    pairs = [(i, j) for i in range(nq) for j in range((i * tq + tq - 1) // tk + 1)]
    qi = jnp.asarray([p[0] for p in pairs], jnp.int32)
    kj = jnp.asarray([p[1] for p in pairs], jnp.int32)

    def q_map(b, g, n, qi_ref, kj_ref):
        return (b * nq + qi_ref[n], g)

    def kv_spec(col0):
        return pl.BlockSpec((tk, HEAD_DIM),
                            lambda b, g, n, qi_ref, kj_ref: (b * nk + kj_ref[n], col0 + g))

    grid_spec = pltpu.PrefetchScalarGridSpec(
        num_scalar_prefetch=2,
        grid=(B, G, len(pairs)),
        in_specs=[pl.BlockSpec((tq, gw), q_map), pl.BlockSpec((tq, LANES), q_map),
                  kv_spec(k_blk), kv_spec(v_blk)],
        out_specs=pl.BlockSpec((tq, gw), q_map),
        scratch_shapes=[pltpu.VMEM((NSA_REP * tq, 2 * HEAD_DIM), BF16),
                        pltpu.VMEM((NSA_REP * tq, LANES), F32),
                        pltpu.VMEM((NSA_REP * tq, 2 * HEAD_DIM), F32)])
    return pl.pallas_call(
        functools.partial(_sel_kernel, tq=tq, tk=tk),
        grid_spec=grid_spec,
        out_shape=jax.ShapeDtypeStruct((T, NSA_HEADS * HEAD_DIM), F32),
        compiler_params=_cparams("parallel", "parallel", "arbitrary"),
        name="nsa_selection",
    )(qi, kj, pb, bias, pb, pb)


def _nsa_out_kernel(x_ref, oc_ref, os_ref, ow_ref, z_ref, gl_ref, w_ref, o_ref):
    sig = jax.nn.sigmoid(gl_ref[...])
    H = NSA_HEADS
    acc = x_ref[...]
    for hp in range(H // 2):
        parts = []
        for h in (2 * hp, 2 * hp + 1):
            sl = slice(h * HEAD_DIM, (h + 1) * HEAD_DIM)
            o = (sig[:, h:h + 1] * oc_ref[:, sl] + sig[:, H + h:H + h + 1] * os_ref[:, sl]
                 + sig[:, 2 * H + h:2 * H + h + 1] * ow_ref[:, sl])
            parts.append((o * _silu(z_ref[:, sl])).astype(BF16))
        rows = slice(2 * hp * HEAD_DIM, (2 * hp + 2) * HEAD_DIM)
        acc = acc + _dot(jnp.concatenate(parts, axis=1), w_ref[rows, :])
    o_ref[...] = acc


def _nsa_out(x, o_cmp, o_sel, o_win, pf, z_blk, gl_blk, w_out):
    T, D = x.shape
    W = o_cmp.shape[1]
    tm = min(T, OUT_TM)
    row = lambda i: (i, 0)
    return pl.pallas_call(
        _nsa_out_kernel,
        grid=(T // tm,),
        in_specs=[pl.BlockSpec((tm, D), row),
                  pl.BlockSpec((tm, W), row), pl.BlockSpec((tm, W), row), pl.BlockSpec((tm, W), row),
                  pl.BlockSpec((tm, W), lambda i: (i, z_blk)),
                  pl.BlockSpec((tm, LANES), lambda i: (i, gl_blk)),
                  pl.BlockSpec((W, D), lambda i: (0, 0))],
        out_specs=pl.BlockSpec((tm, D), row),
        out_shape=jax.ShapeDtypeStruct((T, D), F32),
        compiler_params=_cparams("parallel"),
        name="nsa_out",
    )(x, o_cmp, o_sel, o_win, pf, pf, w_out)


def _nsa_layer(x, B, S, tabs, norm_w, w_in, ck_pos, ck_w1, ck_w2, cv_pos, cv_w1, cv_w2, w_out):
    H, G, d = NSA_HEADS, NSA_KV_GROUPS, HEAD_DIM
    kvw = G * d
    c = [0, H * d]
    for _ in range(6):
        c.append(c[-1] + kvw)
    c.append(c[-1] + 3 * H)
    c.append(c[-1] + H * d)
    seg = lambda n: w_in[:, c[n]:c[n + 1]]
    wq, wkc, wvc, wks, wvs, wkw, wvw, wgl, wz = [seg(n) for n in range(9)]
    wq, wkc, wks, wkw = [_permute_head_dims(t, 1) for t in (wq, wkc, wks, wkw)]
    ck_pos = _permute_head_dims(ck_pos, 1)
    ck_w1 = _permute_head_dims(ck_w1, 1)
    ck_w2 = _permute_head_dims(ck_w2, 1)
    w_b = jnp.concatenate([wq, wks, wkw, wvs, wvw], axis=1).astype(BF16)
    w_f = jnp.concatenate([wz, wkc, wvc, wgl, jnp.zeros((w_in.shape[0], LANES - 3 * H), w_in.dtype)],
                          axis=1).astype(BF16)
    pb = _norm_proj(x, norm_w, w_b, BF16, 1024, rope=(0, H + 2 * G), tabs=tabs, q_heads=H)
    pf = _norm_proj(x, norm_w, w_f, F32, 640, rope=(H, H + G), tabs=tabs)
    ks_blk, kw_blk, vs_blk, vw_blk = H, H + G, H + 2 * G, H + 3 * G
    kc_blk, vc_blk, gl_blk = H, H + G, H + 2 * G

    kT, vcmp = _compress(pf, B, S, kc_blk, vc_blk, ck_pos, ck_w1, ck_w2, cv_pos, cv_w1, cv_w2)
    o_cmp, bias = _cmp_topk(pb, kT, vcmp, B, S)
    o_win = _window(pb, B, S, kw_blk, vw_blk)
    o_sel = _selection(pb, bias, B, S, ks_blk, vs_blk)
    return _nsa_out(x, o_cmp, o_sel, o_win, pf, 0, gl_blk, w_out.astype(BF16))


def _hgrn_kernel(q_ref, f_ref, i_ref, z_ref, lbl_ref, gw_ref, o_ref,
                 st_ref, *, C, sub, hb, layer):
    n = pl.program_id(2)

    @pl.when(n == 0)
    def _():
        st_ref[...] = jnp.zeros(st_ref.shape, F32)

    r0 = lax.broadcasted_iota(jnp.int32, (C, C), 0)
    r1 = lax.broadcasted_iota(jnp.int32, (C, C), 1)
    tri = (r0 >= r1).astype(BF16)
    dsub = sub // 2
    drow = lax.broadcasted_iota(jnp.int32, (C // dsub, dsub, HEAD_DIM), 1)
    row_id = lax.broadcasted_iota(jnp.int32, (C, HEAD_DIM), 0)

    for h in range(hb):
        cs = slice(h * HEAD_DIM, (h + 1) * HEAD_DIM)
        lg = lbl_ref[:, cs]
        e = jnp.exp(lg - jnp.max(lg, axis=0, keepdims=True))
        p = e / jnp.sum(e, axis=0, keepdims=True)
        lb = jnp.sum(p[0:layer + 1, :], axis=0, keepdims=True) - p[0:1, :]

        fl = f_ref[:, cs]
        t = jnp.exp(-jnp.abs(fl))
        r = 1.0 / (1.0 + t)
        pos = fl >= 0.0
        sig_p = jnp.where(pos, r, t * r)
        sig_n = jnp.where(pos, t * r, r)
        k = (1.0 - lb) * sig_n
        lc = jnp.minimum(fl, 0.0) - jnp.log(1.0 + t) + jnp.log1p(-lb)
        g = jnp.maximum(jnp.log(lb + (1.0 - lb) * sig_p), lc)
        qs = _silu(q_ref[:, cs])
        iv = i_ref[:, cs]
        ib = iv.astype(BF16)

        g1 = g.astype(BF16)
        rem = g - g1.astype(F32)
        g2 = rem.astype(BF16)
        g3 = (rem - g2.astype(F32)).astype(BF16)
        b = _dot(tri, g1) + _dot(tri, g2) + _dot(tri, g3)

        stT = st_ref[h]
        o_inter = _dot_nt((qs * jnp.exp(b)).astype(BF16), stT.astype(BF16))

        nd = C // dsub
        b3 = b.reshape(nd, dsub, HEAD_DIM)
        q3 = qs.reshape(nd, dsub, HEAD_DIM)
        k3 = k.reshape(nd, dsub, HEAD_DIM)
        i3 = iv.reshape(nd, dsub, HEAD_DIM)
        o3 = jnp.zeros((nd, dsub, HEAD_DIM), F32)
        for s in range(dsub):
            w = jnp.where(drow >= s, jnp.exp(b3 - b3[:, s:s + 1, :]), 0.0)
            col = jnp.sum(q3 * w * k3[:, s:s + 1, :], axis=-1, keepdims=True)
            o3 = o3 + col * i3[:, s:s + 1, :]

        att = None
        bsz = 2 * dsub
        while bsz <= C:
            bl = b.reshape(C // bsz, bsz, HEAD_DIM)
            bmid = jnp.broadcast_to(bl[:, bsz // 2 - 1:bsz // 2, :], bl.shape).reshape(C, HEAD_DIM)
            second = (row_id % bsz) >= bsz // 2
            qk = jnp.where(second, qs, k) * jnp.exp(-jnp.abs(b - bmid))
            qh = jnp.where(second, qk, 0.0).astype(BF16)
            kh = jnp.where(second, 0.0, qk).astype(BF16)
            a = _dot_nt(qh, kh)
            if bsz < C:
                a = jnp.where(r0 // bsz == r1 // bsz, a, 0.0)
            att = a if att is None else att + a
            bsz *= 2
        o = o_inter + o3.reshape(C, HEAD_DIM) + _dot(att.astype(BF16), ib)

        b_last = b[C - 1:C, :]
        kd = (k * jnp.exp(b_last - b)).astype(BF16)
        st_ref[h] = stT * jnp.exp(b_last) + _dot(iv.T.astype(BF16), kd)

        o = o * lax.rsqrt(jnp.mean(o * o, axis=-1, keepdims=True) + RMS_EPS) * gw_ref[...]
        o_ref[:, cs] = (o * _silu(z_ref[:, cs])).astype(BF16)


def _hgrn_recurrence(ph, lb_logits, gnorm_w, B, S, layer):
    H = HGRN_HEADS
    T = B * S
    C = min(S, HGRN_CHUNK)
    nc = S // C
    NL = lb_logits.shape[0]
    hb = HGRN_HEADS_PER_STEP
    hg = H // hb
    w = hb * HEAD_DIM

    def col(c0):
        return pl.BlockSpec((C, w), lambda b, h, n: (b * nc + n, c0 + h))

    return pl.pallas_call(
        functools.partial(_hgrn_kernel, C=C, sub=HGRN_SUB, hb=hb, layer=layer),
        grid=(B, hg, nc),
        in_specs=[col(0), col(hg), col(2 * hg), col(3 * hg),
                  pl.BlockSpec((NL, w), lambda b, h, n: (0, h)),
                  pl.BlockSpec((1, HEAD_DIM), lambda b, h, n: (0, 0))],
        out_specs=pl.BlockSpec((C, w), lambda b, h, n: (b * nc + n, h)),
        out_shape=jax.ShapeDtypeStruct((T, H * HGRN_I_DIM), BF16),
        scratch_shapes=[pltpu.VMEM((hb, HGRN_I_DIM, HGRN_F_DIM), F32)],
        compiler_params=_cparams("parallel", "parallel", "arbitrary"),
        name="hgrn_recurrence",
    )(ph, ph, ph, ph, lb_logits, gnorm_w.reshape(1, HEAD_DIM))


def _hgrn_out_kernel(x_ref, og_ref, w_ref, fw_ref, o_ref, *, final_norm):
    y = x_ref[...] + _dot(og_ref[...], w_ref[...])
    if final_norm:
        ms = jnp.mean(y * y, axis=-1, keepdims=True)
        y = y * lax.rsqrt(ms + RMS_EPS) * fw_ref[...]
    o_ref[...] = y


def _hgrn_out(x, og, w_out, final_w, final_norm):
    T, D = x.shape
    W = og.shape[1]
    tm = min(T, 2 * OUT_TM)
    row = lambda i: (i, 0)
    return pl.pallas_call(
        functools.partial(_hgrn_out_kernel, final_norm=final_norm),
        grid=(T // tm,),
        in_specs=[pl.BlockSpec((tm, D), row), pl.BlockSpec((tm, W), row),
                  pl.BlockSpec((W, D), lambda i: (0, 0)),
                  pl.BlockSpec((1, D), lambda i: (0, 0))],
        out_specs=pl.BlockSpec((tm, D), row),
        out_shape=jax.ShapeDtypeStruct((T, D), F32),
        compiler_params=_cparams("parallel"),
        name="hgrn_out",
    )(x, og, w_out, final_w.reshape(1, D))


def _hgrn_layer(x, B, S, norm_w, w_in, lb_logits, gnorm_w, w_out, layer, final_w, final_norm):
    ph = _norm_proj(x, norm_w, w_in.astype(BF16), F32, 1024)
    og = _hgrn_recurrence(ph, lb_logits, gnorm_w, B, S, layer)
    return _hgrn_out(x, og, w_out.astype(BF16), final_w, final_norm)


def kernel(x, positions, norm_w, final_norm_w, nsa_w_in, nsa_ck_pos, nsa_ck_w1, nsa_ck_w2,
           nsa_cv_pos, nsa_cv_w1, nsa_cv_w2, nsa_w_out, hgrn_w_in, hgrn_lb_logits,
           hgrn_gnorm_w, hgrn_w_out):
    B, S, D = x.shape
    depth = norm_w.shape[0]
    xf = x.reshape(B * S, D)
    tabs = _rope_tables(positions)
    for layer in range(depth):
        j = layer // 2
        if layer % 2 == 0:
            xf = _nsa_layer(xf, B, S, tabs, norm_w[layer], nsa_w_in[j], nsa_ck_pos[j], nsa_ck_w1[j],
                            nsa_ck_w2[j], nsa_cv_pos[j], nsa_cv_w1[j], nsa_cv_w2[j], nsa_w_out[j])
        else:
            xf = _hgrn_layer(xf, B, S, norm_w[layer], hgrn_w_in[j], hgrn_lb_logits, hgrn_gnorm_w[j],
                             hgrn_w_out[j], j, final_norm_w, layer == depth - 1)
    return xf.reshape(B, S, D)
```

```python
import functools
import math

import jax
import jax.numpy as jnp
from jax import lax
from jax.experimental import pallas as pl
from jax.experimental.pallas import tpu as pltpu

F32 = jnp.float32
BF16 = jnp.bfloat16

RMS_EPS = 1e-6
NEG = -1e30
BELOW_NEG = -3e38
HEAD_DIM = 128
NSA_HEADS = 16
NSA_KV_GROUPS = 4
NSA_REP = NSA_HEADS // NSA_KV_GROUPS
CMP_BLOCK = 32
CMP_STRIDE = 16
SEL_BLOCK = 64
SEL_TOPK = 16
WINDOW = 512
ROPE_THETA = 500000.0
ROPE_DIM = HEAD_DIM // 4
HGRN_HEADS = 16
HGRN_F_DIM = 128
HGRN_I_DIM = 128

LANES = 128
VMEM_LIMIT_BYTES = 56 * 1024 * 1024

ATTN_SCALE = HEAD_DIM ** -0.5
EXP2_SCALE = ATTN_SCALE * math.log2(math.e)

PROJ_TM = 1024
PROJ_CHUNK_HEADS = 2
CMP_TQ = 512
SEL_TQ = 512
SEL_TK = 1024
OUT_TM = 256
HGRN_CHUNK = 128
HGRN_SUB = 16
HGRN_HEADS_PER_STEP = 4


def _cparams(*sem):
    return pltpu.CompilerParams(dimension_semantics=sem, vmem_limit_bytes=VMEM_LIMIT_BYTES)


def _dot(a, b):
    return jnp.dot(a, b, preferred_element_type=F32)


def _dot_nt(a, b):
    return lax.dot_general(a, b, (((1,), (1,)), ((), ())), preferred_element_type=F32)


def _silu(v):
    return v * jax.nn.sigmoid(v)


_HALF = ROPE_DIM // 2
_PARTNER_SHIFT = LANES // 2
ROPE_PERM = (list(range(_HALF))
             + list(range(ROPE_DIM, ROPE_DIM + _PARTNER_SHIFT - _HALF))
             + list(range(_HALF, ROPE_DIM))
             + list(range(ROPE_DIM + _PARTNER_SHIFT - _HALF, HEAD_DIM)))


def _permute_head_dims(w, axis):
    axis = axis % w.ndim
    n = w.shape[axis] // HEAD_DIM
    shape = w.shape[:axis] + (n, HEAD_DIM) + w.shape[axis + 1:]
    wr = w.reshape(shape)
    cuts = [(0, _HALF), (ROPE_DIM, ROPE_DIM + _PARTNER_SHIFT - _HALF), (_HALF, ROPE_DIM),
            (ROPE_DIM + _PARTNER_SHIFT - _HALF, HEAD_DIM)]
    parts = [lax.slice_in_dim(wr, lo, hi, axis=axis + 1) for lo, hi in cuts]
    return jnp.concatenate(parts, axis=axis + 1).reshape(w.shape)


def _rope_table_kernel(pos_ref, inv_ref, cos_ref, sin_ref):
    ang = pos_ref[...] * inv_ref[...]
    lane = lax.broadcasted_iota(jnp.int32, ang.shape, 1)
    s = jnp.sin(ang)
    cos_ref[...] = jnp.cos(ang)
    sin_ref[...] = jnp.where(lane < _HALF, -s, s)


def _rope_tables(positions):
    T = positions.size
    tm = min(T, 2048)
    inv = ROPE_THETA ** (-jnp.arange(_HALF, dtype=F32) / _HALF)
    inv_pat = jnp.concatenate([inv, jnp.zeros((_PARTNER_SHIFT - _HALF,), F32), inv,
                               jnp.zeros((LANES - _PARTNER_SHIFT - _HALF,), F32)]).reshape(1, LANES)
    pos = positions.astype(F32).reshape(T, 1)
    return pl.pallas_call(
        _rope_table_kernel,
        grid=(T // tm,),
        in_specs=[pl.BlockSpec((tm, 1), lambda i: (i, 0)),
                  pl.BlockSpec((1, LANES), lambda i: (0, 0))],
        out_specs=[pl.BlockSpec((tm, LANES), lambda i: (i, 0))] * 2,
        out_shape=[jax.ShapeDtypeStruct((T, LANES), F32)] * 2,
        compiler_params=_cparams("parallel"),
        name="rope_tables",
    )(pos, inv_pat)


def _norm_proj_kernel(*refs, n_tiles, heads_per_tile, rope_lo, rope_hi, q_heads):
    if rope_hi > rope_lo:
        x_ref, nw_ref, w_ref, cos_ref, sin_ref, o_ref, hn_ref = refs
    else:
        x_ref, nw_ref, w_ref, o_ref, hn_ref = refs
    j = pl.program_id(1)

    @pl.when(j == 0)
    def _():
        x = x_ref[...]
        ms = jnp.mean(x * x, axis=-1, keepdims=True)
        hn_ref[...] = (x * lax.rsqrt(ms + RMS_EPS) * nw_ref[...]).astype(BF16)

    def tile_body(kinds):
        h = 0
        while h < heads_per_tile:
            nh = min(PROJ_CHUNK_HEADS, heads_per_tile - h)
            acc = _dot(hn_ref[...], w_ref[:, h * HEAD_DIM:(h + nh) * HEAD_DIM])
            for hh in range(nh):
                sl = slice((h + hh) * HEAD_DIM, (h + hh + 1) * HEAD_DIM)
                t = acc[:, hh * HEAD_DIM:(hh + 1) * HEAD_DIM]
                kind = kinds[h + hh]
                if kind != 'plain':
                    t = t * cos_ref[...] + pltpu.roll(t, _PARTNER_SHIFT, axis=1) * sin_ref[...]
                if kind == 'rope_q':
                    t = t * EXP2_SCALE
                o_ref[:, sl] = t.astype(o_ref.dtype)
            h += nh

    def kind_of(head):
        if rope_lo <= head < rope_hi:
            return 'rope_q' if head < q_heads else 'rope'
        return 'plain'

    patterns = {}
    for jj in range(n_tiles):
        kinds = tuple(kind_of(jj * heads_per_tile + h) for h in range(heads_per_tile))
        patterns.setdefault(kinds, []).append(jj)
    for kinds, tiles in patterns.items():
        cond = functools.reduce(jnp.logical_or, [j == jj for jj in tiles])
        pl.when(cond)(functools.partial(tile_body, kinds))


def _norm_proj(x, norm_w, w, layer, out_dtype, tn, rope=None, tabs=None, q_heads=0):
    T, D = x.shape
    N = w.shape[2]
    tm = min(T, PROJ_TM)
    rope_lo, rope_hi = rope if rope is not None else (0, 0)
    in_specs = [pl.BlockSpec((tm, D), lambda i, j: (i, 0)),
                pl.BlockSpec((1, D), lambda i, j: (0, 0)),
                pl.BlockSpec((None, D, tn), lambda i, j: (layer, 0, j))]
    args = [x, norm_w.reshape(1, D), w]
    if rope is not None:
        in_specs += [pl.BlockSpec((tm, LANES), lambda i, j: (i, 0))] * 2
        args += list(tabs)
    return pl.pallas_call(
        functools.partial(_norm_proj_kernel, n_tiles=N // tn, heads_per_tile=tn // HEAD_DIM,
                          rope_lo=rope_lo, rope_hi=rope_hi, q_heads=q_heads),
        grid=(T // tm, N // tn),
        in_specs=in_specs,
        out_specs=pl.BlockSpec((tm, tn), lambda i, j: (i, j)),
        out_shape=jax.ShapeDtypeStruct((T, N), out_dtype),
        scratch_shapes=[pltpu.VMEM((tm, D), BF16)],
        compiler_params=_cparams("parallel", "arbitrary"),
        name="norm_proj",
    )(*args)


def _compress_kernel(kc_ref, vc_ref, kpos_ref, kw1_ref, kw2_ref, vpos_ref, vw1_ref, vw2_ref,
                     kT_ref, v_ref, shift_ref, *, ncp):
    half_blk = CMP_BLOCK // 2

    def comp(t_ref, pos_ref, w1_ref, w2_ref):
        a = jnp.zeros((ncp, HEAD_DIM), F32)
        bm = jnp.zeros((ncp, HEAD_DIM), F32)
        for l in range(half_blk):
            xl = t_ref[pl.ds(l, ncp, stride=CMP_STRIDE), :]
            a = a + _dot((xl + pos_ref[l:l + 1, :]).astype(BF16), w1_ref[l])
            bm = bm + _dot((xl + pos_ref[half_blk + l:half_blk + l + 1, :]).astype(BF16),
                           w1_ref[half_blk + l])
        shift_ref[pl.ds(0, ncp), :] = bm
        shift_ref[pl.ds(ncp, 8), :] = jnp.zeros((8, HEAD_DIM), F32)
        hid = _silu(a + shift_ref[pl.ds(1, ncp), :])
        out = _dot(hid.astype(BF16), w2_ref[...])
        row = lax.broadcasted_iota(jnp.int32, out.shape, 0)
        return jnp.where(row < ncp - 1, out, 0.0)

    kT_ref[0, 0] = comp(kc_ref, kpos_ref, kw1_ref, kw2_ref).T.astype(BF16)
    v_ref[0, 0] = comp(vc_ref, vpos_ref, vw1_ref, vw2_ref).astype(BF16)


def _compress(pf, B, S, kc_blk, vc_blk, layer, ck_pos, ck_w1, ck_w2, cv_pos, cv_w1, cv_w2):
    G = NSA_KV_GROUPS
    ncp = S // CMP_STRIDE
    full2 = lambda b, g: (layer, 0, 0)
    full3 = lambda b, g: (layer, 0, 0, 0)
    return pl.pallas_call(
        functools.partial(_compress_kernel, ncp=ncp),
        grid=(B, G),
        in_specs=[pl.BlockSpec((S, HEAD_DIM), lambda b, g: (b, kc_blk + g)),
                  pl.BlockSpec((S, HEAD_DIM), lambda b, g: (b, vc_blk + g)),
                  pl.BlockSpec((None, CMP_BLOCK, HEAD_DIM), full2),
                  pl.BlockSpec((None, CMP_BLOCK, HEAD_DIM, HEAD_DIM), full3),
                  pl.BlockSpec((None, HEAD_DIM, HEAD_DIM), full2),
                  pl.BlockSpec((None, CMP_BLOCK, HEAD_DIM), full2),
                  pl.BlockSpec((None, CMP_BLOCK, HEAD_DIM, HEAD_DIM), full3),
                  pl.BlockSpec((None, HEAD_DIM, HEAD_DIM), full2)],
        out_specs=[pl.BlockSpec((1, 1, HEAD_DIM, ncp), lambda b, g: (b, g, 0, 0)),
                   pl.BlockSpec((1, 1, ncp, HEAD_DIM), lambda b, g: (b, g, 0, 0))],
        out_shape=[jax.ShapeDtypeStruct((B, G, HEAD_DIM, ncp), BF16),
                   jax.ShapeDtypeStruct((B, G, ncp, HEAD_DIM), BF16)],
        scratch_shapes=[pltpu.VMEM((ncp + 8, HEAD_DIM), F32)],
        compiler_params=_cparams("parallel", "parallel"),
        name="nsa_compress",
    )(pf, pf, ck_pos, ck_w1, ck_w2, cv_pos, cv_w1, cv_w2)


def _cmp_topk_kernel(q_ref, kT_ref, v_ref, o_ref, bias_ref, imp_ref, *, tq, ncp):
    i = pl.program_id(2)
    s0 = i * tq
    t_col = s0 + lax.broadcasted_iota(jnp.int32, (tq, 1), 0)
    any_visible = (t_col >= CMP_BLOCK - 1).astype(F32)

    def attend(nb):
        nc = nb * LANES
        n_row = lax.broadcasted_iota(jnp.int32, (1, nc), 1)
        mc = jnp.logical_and(n_row * CMP_STRIDE + (CMP_BLOCK - 1) <= t_col, n_row < ncp - 1)
        kT = kT_ref[0, 0, :, 0:nc]
        v = v_ref[0, 0, 0:nc, :]
        psum = jnp.zeros((tq, nc), F32)
        for r in range(NSA_REP):
            sl = slice(r * HEAD_DIM, (r + 1) * HEAD_DIM)
            s = jnp.where(mc, _dot(q_ref[:, sl], kT), NEG)
            m = jnp.max(s, axis=-1, keepdims=True)
            e = jnp.exp2(s - m)
            l = jnp.sum(e, axis=-1, keepdims=True)
            pc = e * (any_visible / l)
            o_ref[:, sl] = _dot(pc.astype(BF16), v)
            psum = psum + pc
        n_col = lax.broadcasted_iota(jnp.int32, (nc, LANES), 0) * CMP_STRIDE
        k_row = lax.broadcasted_iota(jnp.int32, (nc, LANES), 1) * SEL_BLOCK
        ovl = jnp.logical_and(n_col < k_row + SEL_BLOCK, n_col + CMP_BLOCK > k_row).astype(BF16)
        hi = psum.astype(BF16)
        lo = (psum - hi.astype(F32)).astype(BF16)
        imp_ref[...] = _dot(hi, ovl) + _dot(lo, ovl)

    nb_total = ncp // LANES
    last_visible = (s0 + tq - CMP_BLOCK) // CMP_STRIDE
    nb_needed = jnp.minimum(last_visible // LANES + 1, nb_total)
    for nb in range(1, nb_total + 1):
        pl.when(nb_needed == nb)(functools.partial(attend, nb))

    blk = lax.broadcasted_iota(jnp.int32, (LANES, LANES), 0)
    lane = lax.broadcasted_iota(jnp.int32, (LANES, LANES), 1)
    groups = []
    for c in range(0, tq, LANES):
        impT = imp_ref[c:c + LANES, :].T
        cur = (s0 + c + lane) // SEL_BLOCK
        forced = jnp.logical_or(blk == 0, jnp.logical_or(blk == cur, blk == cur - 1))
        groups.append(jnp.where(forced, BELOW_NEG, jnp.where(blk <= cur, impT, NEG)))
    for _ in range(SEL_TOPK - 3):
        nxt = []
        for sc in groups:
            m = jnp.max(sc, axis=0, keepdims=True)
            idx = jnp.min(jnp.where(sc == m, blk, LANES), axis=0, keepdims=True)
            nxt.append(jnp.where(blk == idx, BELOW_NEG, sc))
        groups = nxt
    for g, sc in enumerate(groups):
        biasT = jnp.where(sc == BELOW_NEG, 0.0, NEG)
        bias_ref[g * LANES:(g + 1) * LANES, :] = biasT.T.astype(BF16)


def _cmp_topk(pb, kT, vcmp, B, S):
    G = NSA_KV_GROUPS
    T = B * S
    tq = min(S, CMP_TQ)
    nq = S // tq
    ncp = S // CMP_STRIDE
    gw = NSA_REP * HEAD_DIM
    return pl.pallas_call(
        functools.partial(_cmp_topk_kernel, tq=tq, ncp=ncp),
        grid=(B, G, nq),
        in_specs=[pl.BlockSpec((tq, gw), lambda b, g, i: (b * nq + i, g)),
                  pl.BlockSpec((1, 1, HEAD_DIM, ncp), lambda b, g, i: (b, g, 0, 0)),
                  pl.BlockSpec((1, 1, ncp, HEAD_DIM), lambda b, g, i: (b, g, 0, 0))],
        out_specs=[pl.BlockSpec((tq, gw), lambda b, g, i: (b * nq + i, g)),
                   pl.BlockSpec((tq, LANES), lambda b, g, i: (b * nq + i, g))],
        out_shape=[jax.ShapeDtypeStruct((T, NSA_HEADS * HEAD_DIM), F32),
                   jax.ShapeDtypeStruct((T, G * LANES), BF16)],
        scratch_shapes=[pltpu.VMEM((tq, LANES), F32)],
        compiler_params=_cparams("parallel", "parallel", "parallel"),
        name="nsa_cmp_topk",
    )(pb, kT, vcmp)


def _win_kernel(q_ref, kp_ref, kc_ref, vp_ref, vc_ref, o_ref, *, sub):
    i = pl.program_id(2)
    row = lax.broadcasted_iota(jnp.int32, (sub, sub), 0)
    col = lax.broadcasted_iota(jnp.int32, (sub, sub), 1)
    ones = jnp.ones((sub, LANES), BF16)

    def body(first):
        ks = [kp_ref[0:sub, :], kp_ref[sub:2 * sub, :], kc_ref[0:sub, :], kc_ref[sub:2 * sub, :]]
        vs = [jnp.concatenate([v, ones], axis=1) for v in
              (vp_ref[0:sub, :], vp_ref[sub:2 * sub, :], vc_ref[0:sub, :], vc_ref[sub:2 * sub, :])]
        for a in range(2):
            blocks = [(a, col > row), (a + 1, None), (a + 2, col <= row)]
            if first:
                blocks = blocks[2 - a:]
            rows = slice(a * sub, (a + 1) * sub)
            for r in range(NSA_REP):
                sl = slice(r * HEAD_DIM, (r + 1) * HEAD_DIM)
                q = q_ref[rows, sl]
                ss = []
                for kb, mask in blocks:
                    s = _dot_nt(q, ks[kb])
                    ss.append(s if mask is None else jnp.where(mask, s, NEG))
                m = functools.reduce(jnp.maximum, ss)
                m = jnp.max(m, axis=-1, keepdims=True)
                acc = None
                for s, (kb, _) in zip(ss, blocks):
                    part = _dot(jnp.exp2(s - m).astype(BF16), vs[kb])
                    acc = part if acc is None else acc + part
                o_ref[rows, sl] = acc[:, 0:HEAD_DIM] / acc[:, HEAD_DIM:2 * HEAD_DIM]

    pl.when(i > 0)(functools.partial(body, False))
    pl.when(i == 0)(functools.partial(body, True))


def _window(pb, B, S, k_blk, v_blk):
    G = NSA_KV_GROUPS
    T = B * S
    sub = WINDOW // 2
    tq = 2 * sub
    nq = S // tq
    gw = NSA_REP * HEAD_DIM

    def kv_spec(col0, back):
        return pl.BlockSpec((tq, HEAD_DIM),
                            lambda b, g, i: (b * nq + jnp.maximum(i - back, 0), col0 + g))

    return pl.pallas_call(
        functools.partial(_win_kernel, sub=sub),
        grid=(B, G, nq),
        in_specs=[pl.BlockSpec((tq, gw), lambda b, g, i: (b * nq + i, g)),
                  kv_spec(k_blk, 1), kv_spec(k_blk, 0), kv_spec(v_blk, 1), kv_spec(v_blk, 0)],
        out_specs=pl.BlockSpec((tq, gw), lambda b, g, i: (b * nq + i, g)),
        out_shape=jax.ShapeDtypeStruct((T, NSA_HEADS * HEAD_DIM), F32),
        compiler_params=_cparams("parallel", "parallel", "parallel"),
        name="nsa_window",
    )(pb, pb, pb, pb, pb)


def _sel_kernel(qi_ref, kj_ref, q_ref, bias_ref, k_ref, v_ref, o_ref, qa_ref, m_ref, acc_ref, *, tq, tk):
    n = pl.program_id(2)
    i = qi_ref[n]
    j = kj_ref[n]
    last_j = (i * tq + tq - 1) // tk

    @pl.when(j == 0)
    def _():
        for r in range(NSA_REP):
            rows = slice(r * tq, (r + 1) * tq)
            qa_ref[rows, 0:HEAD_DIM] = q_ref[:, r * HEAD_DIM:(r + 1) * HEAD_DIM]
            qa_ref[rows, HEAD_DIM:2 * HEAD_DIM] = bias_ref[...]
        m_ref[...] = jnp.full(m_ref.shape, NEG, F32)
        acc_ref[...] = jnp.zeros(acc_ref.shape, F32)

    def step(col0, ncols, diagonal):
        key = j * tk + col0 + lax.broadcasted_iota(jnp.int32, (ncols, LANES), 0)
        blk = lax.broadcasted_iota(jnp.int32, (ncols, LANES), 1)
        onehot = (key // SEL_BLOCK == blk).astype(BF16)
        ka = jnp.concatenate([k_ref[col0:col0 + ncols, :], onehot], axis=1)
        va = jnp.concatenate([v_ref[col0:col0 + ncols, :], jnp.ones((ncols, LANES), BF16)], axis=1)
        if diagonal:
            visible = (lax.broadcasted_iota(jnp.int32, (tq, tq), 1)
                       <= lax.broadcasted_iota(jnp.int32, (tq, tq), 0))
        def scores(r):
            s = _dot_nt(qa_ref[r * tq:(r + 1) * tq, :], ka)
            return jnp.where(visible, s, NEG) if diagonal else s

        s_next = scores(0)
        for r in range(NSA_REP):
            rows = slice(r * tq, (r + 1) * tq)
            s = s_next
            if r + 1 < NSA_REP:
                s_next = scores(r + 1)
            m_prev = m_ref[rows, :]
            m_new = jnp.maximum(m_prev, jnp.max(s, axis=-1, keepdims=True))
            alpha = jnp.exp2(m_prev - m_new)
            p = jnp.exp2(s - jnp.tile(m_new, (1, ncols // LANES)))
            acc_ref[rows, :] = jnp.tile(alpha, (1, 2)) * acc_ref[rows, :] + _dot(p.astype(BF16), va)
            m_ref[rows, :] = m_new

    @pl.when(j < last_j)
    def _():
        step(0, tk, False)

    for sub in range(tk // tq):
        @pl.when(jnp.logical_and(j == last_j, i % (tk // tq) == sub))
        def _(sub=sub):
            if sub > 0:
                step(0, sub * tq, False)
            step(sub * tq, tq, True)

    @pl.when(j == last_j)
    def _():
        for r in range(NSA_REP):
            rows = slice(r * tq, (r + 1) * tq)
            o_ref[:, r * HEAD_DIM:(r + 1) * HEAD_DIM] = (acc_ref[rows, 0:HEAD_DIM]
                                                         / acc_ref[rows, HEAD_DIM:2 * HEAD_DIM])


def _selection(pb, bias, B, S, k_blk, v_blk):
    G = NSA_KV_GROUPS
    T = B * S
    tq = min(S, SEL_TQ)
    tk = min(S, SEL_TK)
    assert tk % tq == 0
    nq = S // tq
    nk = S // tk
    gw = NSA_REP * HEAD_DIM
    pairs = [(i, j) for i in range(nq) for j in range((i * tq + tq - 1) // tk + 1)]
    qi = jnp.asarray([p[0] for p in pairs], jnp.int32)
    kj = jnp.asarray([p[1] for p in pairs], jnp.int32)

    def q_map(b, g, n, qi_ref, kj_ref):
        return (b * nq + qi_ref[n], g)

    def kv_spec(col0):
        return pl.BlockSpec((tk, HEAD_DIM),
                            lambda b, g, n, qi_ref, kj_ref: (b * nk + kj_ref[n], col0 + g))

    grid_spec = pltpu.PrefetchScalarGridSpec(
        num_scalar_prefetch=2,
        grid=(B, G, len(pairs)),
        in_specs=[pl.BlockSpec((tq, gw), q_map), pl.BlockSpec((tq, LANES), q_map),
                  kv_spec(k_blk), kv_spec(v_blk)],
        out_specs=pl.BlockSpec((tq, gw), q_map),
        scratch_shapes=[pltpu.VMEM((NSA_REP * tq, 2 * HEAD_DIM), BF16),
                        pltpu.VMEM((NSA_REP * tq, LANES), F32),
                        pltpu.VMEM((NSA_REP * tq, 2 * HEAD_DIM), F32)])
    return pl.pallas_call(
        functools.partial(_sel_kernel, tq=tq, tk=tk),
        grid_spec=grid_spec,
        out_shape=jax.ShapeDtypeStruct((T, NSA_HEADS * HEAD_DIM), F32),
        compiler_params=_cparams("parallel", "parallel", "arbitrary"),
        name="nsa_selection",
    )(qi, kj, pb, bias, pb, pb)


def _nsa_out_kernel(x_ref, oc_ref, os_ref, ow_ref, z_ref, gl_ref, w_ref, o_ref):
    sig = jax.nn.sigmoid(gl_ref[...])
    H = NSA_HEADS
    acc = x_ref[...]
    for hp in range(H // 2):
        parts = []
        for h in (2 * hp, 2 * hp + 1):
            sl = slice(h * HEAD_DIM, (h + 1) * HEAD_DIM)
            o = (sig[:, h:h + 1] * oc_ref[:, sl] + sig[:, H + h:H + h + 1] * os_ref[:, sl]
                 + sig[:, 2 * H + h:2 * H + h + 1] * ow_ref[:, sl])
            parts.append((o * _silu(z_ref[:, sl])).astype(BF16))
        rows = slice(2 * hp * HEAD_DIM, (2 * hp + 2) * HEAD_DIM)
        acc = acc + _dot(jnp.concatenate(parts, axis=1), w_ref[rows, :])
    o_ref[...] = acc


def _nsa_out(x, o_cmp, o_sel, o_win, pf, z_blk, gl_blk, w_out, layer):
    T, D = x.shape
    W = o_cmp.shape[1]
    tm = min(T, OUT_TM)
    row = lambda i: (i, 0)
    return pl.pallas_call(
        _nsa_out_kernel,
        grid=(T // tm,),
        in_specs=[pl.BlockSpec((tm, D), row),
                  pl.BlockSpec((tm, W), row), pl.BlockSpec((tm, W), row), pl.BlockSpec((tm, W), row),
                  pl.BlockSpec((tm, W), lambda i: (i, z_blk)),
                  pl.BlockSpec((tm, LANES), lambda i: (i, gl_blk)),
                  pl.BlockSpec((None, W, D), lambda i: (layer, 0, 0))],
        out_specs=pl.BlockSpec((tm, D), row),
        out_shape=jax.ShapeDtypeStruct((T, D), F32),
        compiler_params=_cparams("parallel"),
        name="nsa_out",
    )(x, o_cmp, o_sel, o_win, pf, pf, w_out)


def _prepare_params(nsa_w_in, nsa_ck_pos, nsa_ck_w1, nsa_ck_w2, nsa_cv_pos, nsa_cv_w1, nsa_cv_w2,
                    nsa_w_out, hgrn_w_in, hgrn_w_out):
    H, G, d = NSA_HEADS, NSA_KV_GROUPS, HEAD_DIM
    kvw = G * d
    c = [0, H * d]
    for _ in range(6):
        c.append(c[-1] + kvw)
    c.append(c[-1] + 3 * H)
    c.append(c[-1] + H * d)
    seg = lambda n: nsa_w_in[:, :, c[n]:c[n + 1]]
    wq, wkc, wvc, wks, wvs, wkw, wvw, wgl, wz = [seg(n) for n in range(9)]
    wq, wkc, wks, wkw = [_permute_head_dims(t, 2) for t in (wq, wkc, wks, wkw)]
    pad = jnp.zeros(nsa_w_in.shape[:2] + (LANES - 3 * H,), nsa_w_in.dtype)
    return dict(
        nsa_wb=jnp.concatenate([wq, wks, wkw, wvs, wvw], axis=2).astype(BF16),
        nsa_wf=jnp.concatenate([wz, wkc, wvc, wgl, pad], axis=2).astype(BF16),
        ck_pos=_permute_head_dims(nsa_ck_pos, 2),
        ck_w1=_permute_head_dims(nsa_ck_w1, 2).astype(BF16),
        ck_w2=_permute_head_dims(nsa_ck_w2, 2).astype(BF16),
        cv_pos=nsa_cv_pos, cv_w1=nsa_cv_w1.astype(BF16), cv_w2=nsa_cv_w2.astype(BF16),
        nsa_wo=nsa_w_out.astype(BF16),
        hgrn_wi=hgrn_w_in.astype(BF16), hgrn_wo=hgrn_w_out.astype(BF16))


def _nsa_layer(x, B, S, tabs, norm_w, prm, j):
    H, G = NSA_HEADS, NSA_KV_GROUPS
    pb = _norm_proj(x, norm_w, prm["nsa_wb"], j, BF16, 1024, rope=(0, H + 2 * G), tabs=tabs, q_heads=H)
    pf = _norm_proj(x, norm_w, prm["nsa_wf"], j, F32, 640, rope=(H, H + G), tabs=tabs)
    ks_blk, kw_blk, vs_blk, vw_blk = H, H + G, H + 2 * G, H + 3 * G
    kc_blk, vc_blk, gl_blk = H, H + G, H + 2 * G

    kT, vcmp = _compress(pf, B, S, kc_blk, vc_blk, j, prm["ck_pos"], prm["ck_w1"], prm["ck_w2"],
                         prm["cv_pos"], prm["cv_w1"], prm["cv_w2"])
    o_cmp, bias = _cmp_topk(pb, kT, vcmp, B, S)
    o_win = _window(pb, B, S, kw_blk, vw_blk)
    o_sel = _selection(pb, bias, B, S, ks_blk, vs_blk)
    return _nsa_out(x, o_cmp, o_sel, o_win, pf, 0, gl_blk, prm["nsa_wo"], j)


def _hgrn_kernel(q_ref, f_ref, i_ref, z_ref, lbl_ref, gw_ref, o_ref,
                 st_ref, *, C, sub, hb, layer):
    n = pl.program_id(2)

    @pl.when(n == 0)
    def _():
        st_ref[...] = jnp.zeros(st_ref.shape, F32)

    r0 = lax.broadcasted_iota(jnp.int32, (C, C), 0)
    r1 = lax.broadcasted_iota(jnp.int32, (C, C), 1)
    tri = (r0 >= r1).astype(BF16)
    dsub = sub // 2
    drow = lax.broadcasted_iota(jnp.int32, (C // dsub, dsub, HEAD_DIM), 1)
    row_id = lax.broadcasted_iota(jnp.int32, (C, HEAD_DIM), 0)

    for h in range(hb):
        cs = slice(h * HEAD_DIM, (h + 1) * HEAD_DIM)
        lg = lbl_ref[:, cs]
        e = jnp.exp(lg - jnp.max(lg, axis=0, keepdims=True))
        p = e / jnp.sum(e, axis=0, keepdims=True)
        lb = jnp.sum(p[0:layer + 1, :], axis=0, keepdims=True) - p[0:1, :]

        fl = f_ref[:, cs]
        t = jnp.exp(-jnp.abs(fl))
        r = 1.0 / (1.0 + t)
        pos = fl >= 0.0
        sig_p = jnp.where(pos, r, t * r)
        sig_n = jnp.where(pos, t * r, r)
        k = (1.0 - lb) * sig_n
        lc = jnp.minimum(fl, 0.0) - jnp.log(1.0 + t) + jnp.log1p(-lb)
        g = jnp.maximum(jnp.log(lb + (1.0 - lb) * sig_p), lc)
        qs = _silu(q_ref[:, cs])
        iv = i_ref[:, cs]
        ib = iv.astype(BF16)

        g1 = g.astype(BF16)
        rem = g - g1.astype(F32)
        g2 = rem.astype(BF16)
        g3 = (rem - g2.astype(F32)).astype(BF16)
        b = _dot(tri, g1) + _dot(tri, g2) + _dot(tri, g3)

        stT = st_ref[h]
        o_inter = _dot_nt((qs * jnp.exp(b)).astype(BF16), stT.astype(BF16))

        nd = C // dsub
        b3 = b.reshape(nd, dsub, HEAD_DIM)
        q3 = qs.reshape(nd, dsub, HEAD_DIM)
        k3 = k.reshape(nd, dsub, HEAD_DIM)
        i3 = iv.reshape(nd, dsub, HEAD_DIM)
        o3 = jnp.zeros((nd, dsub, HEAD_DIM), F32)
        for s in range(dsub):
            w = jnp.where(drow >= s, jnp.exp(b3 - b3[:, s:s + 1, :]), 0.0)
            col = jnp.sum(q3 * w * k3[:, s:s + 1, :], axis=-1, keepdims=True)
            o3 = o3 + col * i3[:, s:s + 1, :]

        att = None
        bsz = 2 * dsub
        while bsz <= C:
            bl = b.reshape(C // bsz, bsz, HEAD_DIM)
            bmid = jnp.broadcast_to(bl[:, bsz // 2 - 1:bsz // 2, :], bl.shape).reshape(C, HEAD_DIM)
            second = (row_id % bsz) >= bsz // 2
            qk = jnp.where(second, qs, k) * jnp.exp(-jnp.abs(b - bmid))
            qh = jnp.where(second, qk, 0.0).astype(BF16)
            kh = jnp.where(second, 0.0, qk).astype(BF16)
            a = _dot_nt(qh, kh)
            if bsz < C:
                a = jnp.where(r0 // bsz == r1 // bsz, a, 0.0)
            att = a if att is None else att + a
            bsz *= 2
        o = o_inter + o3.reshape(C, HEAD_DIM) + _dot(att.astype(BF16), ib)

        b_last = b[C - 1:C, :]
        kd = (k * jnp.exp(b_last - b)).astype(BF16)
        st_ref[h] = stT * jnp.exp(b_last) + _dot(iv.T.astype(BF16), kd)

        o = o * lax.rsqrt(jnp.mean(o * o, axis=-1, keepdims=True) + RMS_EPS) * gw_ref[...]
        o_ref[:, cs] = (o * _silu(z_ref[:, cs])).astype(BF16)


def _hgrn_recurrence(ph, lb_logits, gnorm_w, B, S, layer):
    H = HGRN_HEADS
    T = B * S
    C = min(S, HGRN_CHUNK)
    nc = S // C
    NL = lb_logits.shape[0]
    hb = HGRN_HEADS_PER_STEP
    hg = H // hb
    w = hb * HEAD_DIM

    def col(c0):
        return pl.BlockSpec((C, w), lambda b, h, n: (b * nc + n, c0 + h))

    return pl.pallas_call(
        functools.partial(_hgrn_kernel, C=C, sub=HGRN_SUB, hb=hb, layer=layer),
        grid=(B, hg, nc),
        in_specs=[col(0), col(hg), col(2 * hg), col(3 * hg),
                  pl.BlockSpec((NL, w), lambda b, h, n: (0, h)),
                  pl.BlockSpec((1, HEAD_DIM), lambda b, h, n: (0, 0))],
        out_specs=pl.BlockSpec((C, w), lambda b, h, n: (b * nc + n, h)),
        out_shape=jax.ShapeDtypeStruct((T, H * HGRN_I_DIM), BF16),
        scratch_shapes=[pltpu.VMEM((hb, HGRN_I_DIM, HGRN_F_DIM), F32)],
        compiler_params=_cparams("parallel", "parallel", "arbitrary"),
        name="hgrn_recurrence",
    )(ph, ph, ph, ph, lb_logits, gnorm_w.reshape(1, HEAD_DIM))


def _hgrn_out_kernel(x_ref, og_ref, w_ref, fw_ref, o_ref, *, final_norm):
    y = x_ref[...] + _dot(og_ref[...], w_ref[...])
    if final_norm:
        ms = jnp.mean(y * y, axis=-1, keepdims=True)
        y = y * lax.rsqrt(ms + RMS_EPS) * fw_ref[...]
    o_ref[...] = y


def _hgrn_out(x, og, w_out, layer, final_w, final_norm):
    T, D = x.shape
    W = og.shape[1]
    tm = min(T, 2 * OUT_TM)
    row = lambda i: (i, 0)
    return pl.pallas_call(
        functools.partial(_hgrn_out_kernel, final_norm=final_norm),
        grid=(T // tm,),
        in_specs=[pl.BlockSpec((tm, D), row), pl.BlockSpec((tm, W), row),
                  pl.BlockSpec((None, W, D), lambda i: (layer, 0, 0)),
                  pl.BlockSpec((1, D), lambda i: (0, 0))],
        out_specs=pl.BlockSpec((tm, D), row),
        out_shape=jax.ShapeDtypeStruct((T, D), F32),
        compiler_params=_cparams("parallel"),
        name="hgrn_out",
    )(x, og, w_out, final_w.reshape(1, D))


def _hgrn_layer(x, B, S, norm_w, prm, lb_logits, gnorm_w, j, final_w, final_norm):
    ph = _norm_proj(x, norm_w, prm["hgrn_wi"], j, F32, 1024)
    og = _hgrn_recurrence(ph, lb_logits, gnorm_w, B, S, j)
    return _hgrn_out(x, og, prm["hgrn_wo"], j, final_w, final_norm)


def kernel(x, positions, norm_w, final_norm_w, nsa_w_in, nsa_ck_pos, nsa_ck_w1, nsa_ck_w2,
           nsa_cv_pos, nsa_cv_w1, nsa_cv_w2, nsa_w_out, hgrn_w_in, hgrn_lb_logits,
           hgrn_gnorm_w, hgrn_w_out):
    B, S, D = x.shape
    depth = norm_w.shape[0]
    xf = x.reshape(B * S, D)
    tabs = _rope_tables(positions)
    prm = _prepare_params(nsa_w_in, nsa_ck_pos, nsa_ck_w1, nsa_ck_w2, nsa_cv_pos, nsa_cv_w1,
                          nsa_cv_w2, nsa_w_out, hgrn_w_in, hgrn_w_out)
    for layer in range(depth):
        j = layer // 2
        if layer % 2 == 0:
            xf = _nsa_layer(xf, B, S, tabs, norm_w[layer], prm, j)
        else:
            xf = _hgrn_layer(xf, B, S, norm_w[layer], prm, hgrn_lb_logits, hgrn_gnorm_w[j], j,
                             final_norm_w, layer == depth - 1)
    return xf.reshape(B, S, D)
```

```python
import functools
import math

import jax
import jax.numpy as jnp
from jax import lax
from jax.experimental import pallas as pl
from jax.experimental.pallas import tpu as pltpu

F32 = jnp.float32
BF16 = jnp.bfloat16

RMS_EPS = 1e-6
NEG = -1e30
BELOW_NEG = -3e38
HEAD_DIM = 128
NSA_HEADS = 16
NSA_KV_GROUPS = 4
NSA_REP = NSA_HEADS // NSA_KV_GROUPS
CMP_BLOCK = 32
CMP_STRIDE = 16
SEL_BLOCK = 64
SEL_TOPK = 16
WINDOW = 512
ROPE_THETA = 500000.0
ROPE_DIM = HEAD_DIM // 4
HGRN_HEADS = 16
HGRN_F_DIM = 128
HGRN_I_DIM = 128

LANES = 128
VMEM_LIMIT_BYTES = 56 * 1024 * 1024

ATTN_SCALE = HEAD_DIM ** -0.5
EXP2_SCALE = ATTN_SCALE * math.log2(math.e)

PROJ_TM = 2048
PROJ_CHUNK_HEADS = 2
CMP_TQ = 512
SEL_TQ = 512
SEL_TK = 1024
OUT_TM = 256
HGRN_CHUNK = 128
HGRN_SUB = 16
HGRN_HEADS_PER_STEP = 4


def _cparams(*sem):
    return pltpu.CompilerParams(dimension_semantics=sem, vmem_limit_bytes=VMEM_LIMIT_BYTES)


def _dot(a, b):
    return jnp.dot(a, b, preferred_element_type=F32)


def _dot_nt(a, b):
    return lax.dot_general(a, b, (((1,), (1,)), ((), ())), preferred_element_type=F32)


def _silu(v):
    return v * jax.nn.sigmoid(v)


_HALF = ROPE_DIM // 2
_PARTNER_SHIFT = LANES // 2
ROPE_PERM = (list(range(_HALF))
             + list(range(ROPE_DIM, ROPE_DIM + _PARTNER_SHIFT - _HALF))
             + list(range(_HALF, ROPE_DIM))
             + list(range(ROPE_DIM + _PARTNER_SHIFT - _HALF, HEAD_DIM)))


def _permute_head_dims(w, axis):
    axis = axis % w.ndim
    n = w.shape[axis] // HEAD_DIM
    shape = w.shape[:axis] + (n, HEAD_DIM) + w.shape[axis + 1:]
    wr = w.reshape(shape)
    cuts = [(0, _HALF), (ROPE_DIM, ROPE_DIM + _PARTNER_SHIFT - _HALF), (_HALF, ROPE_DIM),
            (ROPE_DIM + _PARTNER_SHIFT - _HALF, HEAD_DIM)]
    parts = [lax.slice_in_dim(wr, lo, hi, axis=axis + 1) for lo, hi in cuts]
    return jnp.concatenate(parts, axis=axis + 1).reshape(w.shape)


def _rope_table_kernel(pos_ref, inv_ref, cos_ref, sin_ref):
    ang = pos_ref[...] * inv_ref[...]
    lane = lax.broadcasted_iota(jnp.int32, ang.shape, 1)
    s = jnp.sin(ang)
    cos_ref[...] = jnp.cos(ang)
    sin_ref[...] = jnp.where(lane < _HALF, -s, s)


def _rope_tables(positions):
    T = positions.size
    tm = min(T, 2048)
    inv = ROPE_THETA ** (-jnp.arange(_HALF, dtype=F32) / _HALF)
    inv_pat = jnp.concatenate([inv, jnp.zeros((_PARTNER_SHIFT - _HALF,), F32), inv,
                               jnp.zeros((LANES - _PARTNER_SHIFT - _HALF,), F32)]).reshape(1, LANES)
    pos = positions.astype(F32).reshape(T, 1)
    return pl.pallas_call(
        _rope_table_kernel,
        grid=(T // tm,),
        in_specs=[pl.BlockSpec((tm, 1), lambda i: (i, 0)),
                  pl.BlockSpec((1, LANES), lambda i: (0, 0))],
        out_specs=[pl.BlockSpec((tm, LANES), lambda i: (i, 0))] * 2,
        out_shape=[jax.ShapeDtypeStruct((T, LANES), F32)] * 2,
        compiler_params=_cparams("parallel"),
        name="rope_tables",
    )(pos, inv_pat)


def _rms_bf16(y, w):
    ms = jnp.mean(y * y, axis=-1, keepdims=True)
    return (y * lax.rsqrt(ms + RMS_EPS) * w).astype(BF16)


def _rmsnorm_kernel(x_ref, w_ref, o_ref):
    o_ref[...] = _rms_bf16(x_ref[...], w_ref[...])


def _rmsnorm(x, w):
    T, D = x.shape
    tm = min(T, 2 * OUT_TM)
    return pl.pallas_call(
        _rmsnorm_kernel,
        grid=(T // tm,),
        in_specs=[pl.BlockSpec((tm, D), lambda i: (i, 0)), pl.BlockSpec((1, D), lambda i: (0, 0))],
        out_specs=pl.BlockSpec((tm, D), lambda i: (i, 0)),
        out_shape=jax.ShapeDtypeStruct((T, D), BF16),
        compiler_params=_cparams("parallel"),
        name="rmsnorm",
    )(x, w.reshape(1, D))


def _proj_kernel(*refs, n_tiles, heads_per_tile, rope_lo, rope_hi, q_heads):
    if rope_hi > rope_lo:
        hn_ref, w_ref, cos_ref, sin_ref, o_ref = refs
    else:
        hn_ref, w_ref, o_ref = refs
    j = pl.program_id(1)

    def tile_body(kinds):
        h = 0
        while h < heads_per_tile:
            nh = min(PROJ_CHUNK_HEADS, heads_per_tile - h)
            acc = _dot(hn_ref[...], w_ref[:, h * HEAD_DIM:(h + nh) * HEAD_DIM])
            for hh in range(nh):
                sl = slice((h + hh) * HEAD_DIM, (h + hh + 1) * HEAD_DIM)
                t = acc[:, hh * HEAD_DIM:(hh + 1) * HEAD_DIM]
                kind = kinds[h + hh]
                if kind != 'plain':
                    t = t * cos_ref[...] + pltpu.roll(t, _PARTNER_SHIFT, axis=1) * sin_ref[...]
                if kind == 'rope_q':
                    t = t * EXP2_SCALE
                o_ref[:, sl] = t.astype(o_ref.dtype)
            h += nh

    def kind_of(head):
        if rope_lo <= head < rope_hi:
            return 'rope_q' if head < q_heads else 'rope'
        return 'plain'

    patterns = {}
    for jj in range(n_tiles):
        kinds = tuple(kind_of(jj * heads_per_tile + h) for h in range(heads_per_tile))
        patterns.setdefault(kinds, []).append(jj)
    for kinds, tiles in patterns.items():
        cond = functools.reduce(jnp.logical_or, [j == jj for jj in tiles])
        pl.when(cond)(functools.partial(tile_body, kinds))


def _proj(hn, w, layer, out_dtype, tn, rope=None, tabs=None, q_heads=0):
    T, D = hn.shape
    N = w.shape[2]
    tm = min(T, PROJ_TM)
    rope_lo, rope_hi = rope if rope is not None else (0, 0)
    in_specs = [pl.BlockSpec((tm, D), lambda i, j: (i, 0)),
                pl.BlockSpec((None, D, tn), lambda i, j: (layer, 0, j))]
    args = [hn, w]
    if rope is not None:
        in_specs += [pl.BlockSpec((tm, LANES), lambda i, j: (i, 0))] * 2
        args += list(tabs)
    return pl.pallas_call(
        functools.partial(_proj_kernel, n_tiles=N // tn, heads_per_tile=tn // HEAD_DIM,
                          rope_lo=rope_lo, rope_hi=rope_hi, q_heads=q_heads),
        grid=(T // tm, N // tn),
        in_specs=in_specs,
        out_specs=pl.BlockSpec((tm, tn), lambda i, j: (i, j)),
        out_shape=jax.ShapeDtypeStruct((T, N), out_dtype),
        compiler_params=_cparams("parallel", "arbitrary"),
        name="in_proj",
    )(*args)


def _compress_kernel(kc_ref, vc_ref, kpos_ref, kw1_ref, kw2_ref, vpos_ref, vw1_ref, vw2_ref,
                     kT_ref, v_ref, shift_ref, *, ncp):
    half_blk = CMP_BLOCK // 2

    def comp(t_ref, pos_ref, w1_ref, w2_ref):
        a = jnp.zeros((ncp, HEAD_DIM), F32)
        bm = jnp.zeros((ncp, HEAD_DIM), F32)
        for l in range(half_blk):
            xl = t_ref[pl.ds(l, ncp, stride=CMP_STRIDE), :]
            a = a + _dot((xl + pos_ref[l:l + 1, :]).astype(BF16), w1_ref[l])
            bm = bm + _dot((xl + pos_ref[half_blk + l:half_blk + l + 1, :]).astype(BF16),
                           w1_ref[half_blk + l])
        shift_ref[pl.ds(0, ncp), :] = bm
        shift_ref[pl.ds(ncp, 8), :] = jnp.zeros((8, HEAD_DIM), F32)
        hid = _silu(a + shift_ref[pl.ds(1, ncp), :])
        out = _dot(hid.astype(BF16), w2_ref[...])
        row = lax.broadcasted_iota(jnp.int32, out.shape, 0)
        return jnp.where(row < ncp - 1, out, 0.0)

    kT_ref[0, 0] = comp(kc_ref, kpos_ref, kw1_ref, kw2_ref).T.astype(BF16)
    v_ref[0, 0] = comp(vc_ref, vpos_ref, vw1_ref, vw2_ref).astype(BF16)


def _compress(pf, B, S, kc_blk, vc_blk, layer, ck_pos, ck_w1, ck_w2, cv_pos, cv_w1, cv_w2):
    G = NSA_KV_GROUPS
    ncp = S // CMP_STRIDE
    full2 = lambda b, g: (layer, 0, 0)
    full3 = lambda b, g: (layer, 0, 0, 0)
    return pl.pallas_call(
        functools.partial(_compress_kernel, ncp=ncp),
        grid=(B, G),
        in_specs=[pl.BlockSpec((S, HEAD_DIM), lambda b, g: (b, kc_blk + g)),
                  pl.BlockSpec((S, HEAD_DIM), lambda b, g: (b, vc_blk + g)),
                  pl.BlockSpec((None, CMP_BLOCK, HEAD_DIM), full2),
                  pl.BlockSpec((None, CMP_BLOCK, HEAD_DIM, HEAD_DIM), full3),
                  pl.BlockSpec((None, HEAD_DIM, HEAD_DIM), full2),
                  pl.BlockSpec((None, CMP_BLOCK, HEAD_DIM), full2),
                  pl.BlockSpec((None, CMP_BLOCK, HEAD_DIM, HEAD_DIM), full3),
                  pl.BlockSpec((None, HEAD_DIM, HEAD_DIM), full2)],
        out_specs=[pl.BlockSpec((1, 1, HEAD_DIM, ncp), lambda b, g: (b, g, 0, 0)),
                   pl.BlockSpec((1, 1, ncp, HEAD_DIM), lambda b, g: (b, g, 0, 0))],
        out_shape=[jax.ShapeDtypeStruct((B, G, HEAD_DIM, ncp), BF16),
                   jax.ShapeDtypeStruct((B, G, ncp, HEAD_DIM), BF16)],
        scratch_shapes=[pltpu.VMEM((ncp + 8, HEAD_DIM), F32)],
        compiler_params=_cparams("parallel", "parallel"),
        name="nsa_compress",
    )(pf, pf, ck_pos, ck_w1, ck_w2, cv_pos, cv_w1, cv_w2)


def _cmp_topk_kernel(q_ref, kT_ref, v_ref, o_ref, bias_ref, imp_ref, *, tq, ncp):
    i = pl.program_id(2)
    s0 = i * tq
    t_col = s0 + lax.broadcasted_iota(jnp.int32, (tq, 1), 0)
    any_visible = (t_col >= CMP_BLOCK - 1).astype(F32)

    def attend(nb):
        nc = nb * LANES
        n_row = lax.broadcasted_iota(jnp.int32, (1, nc), 1)
        mc = jnp.logical_and(n_row * CMP_STRIDE + (CMP_BLOCK - 1) <= t_col, n_row < ncp - 1)
        kT = kT_ref[0, 0, :, 0:nc]
        v = v_ref[0, 0, 0:nc, :]
        psum = jnp.zeros((tq, nc), F32)
        for r in range(NSA_REP):
            sl = slice(r * HEAD_DIM, (r + 1) * HEAD_DIM)
            s = jnp.where(mc, _dot(q_ref[:, sl], kT), NEG)
            m = jnp.max(s, axis=-1, keepdims=True)
            e = jnp.exp2(s - m)
            l = jnp.sum(e, axis=-1, keepdims=True)
            pc = e * (any_visible / l)
            o_ref[:, sl] = _dot(pc.astype(BF16), v)
            psum = psum + pc
        n_col = lax.broadcasted_iota(jnp.int32, (nc, LANES), 0) * CMP_STRIDE
        k_row = lax.broadcasted_iota(jnp.int32, (nc, LANES), 1) * SEL_BLOCK
        ovl = jnp.logical_and(n_col < k_row + SEL_BLOCK, n_col + CMP_BLOCK > k_row).astype(BF16)
        hi = psum.astype(BF16)
        lo = (psum - hi.astype(F32)).astype(BF16)
        imp_ref[...] = _dot(hi, ovl) + _dot(lo, ovl)

    nb_total = ncp // LANES
    last_visible = (s0 + tq - CMP_BLOCK) // CMP_STRIDE
    nb_needed = jnp.minimum(last_visible // LANES + 1, nb_total)
    for nb in range(1, nb_total + 1):
        pl.when(nb_needed == nb)(functools.partial(attend, nb))

    blk = lax.broadcasted_iota(jnp.int32, (LANES, LANES), 0)
    lane = lax.broadcasted_iota(jnp.int32, (LANES, LANES), 1)
    groups = []
    for c in range(0, tq, LANES):
        impT = imp_ref[c:c + LANES, :].T
        cur = (s0 + c + lane) // SEL_BLOCK
        forced = jnp.logical_or(blk == 0, jnp.logical_or(blk == cur, blk == cur - 1))
        groups.append(jnp.where(forced, BELOW_NEG, jnp.where(blk <= cur, impT, NEG)))
    for _ in range(SEL_TOPK - 3):
        nxt = []
        for sc in groups:
            m = jnp.max(sc, axis=0, keepdims=True)
            idx = jnp.min(jnp.where(sc == m, blk, LANES), axis=0, keepdims=True)
            nxt.append(jnp.where(blk == idx, BELOW_NEG, sc))
        groups = nxt
    for g, sc in enumerate(groups):
        biasT = jnp.where(sc == BELOW_NEG, 0.0, NEG)
        bias_ref[g * LANES:(g + 1) * LANES, :] = biasT.T.astype(BF16)


def _cmp_topk(pb, kT, vcmp, B, S):
    G = NSA_KV_GROUPS
    T = B * S
    tq = min(S, CMP_TQ)
    nq = S // tq
    ncp = S // CMP_STRIDE
    gw = NSA_REP * HEAD_DIM
    return pl.pallas_call(
        functools.partial(_cmp_topk_kernel, tq=tq, ncp=ncp),
        grid=(B, G, nq),
        in_specs=[pl.BlockSpec((tq, gw), lambda b, g, i: (b * nq + i, g)),
                  pl.BlockSpec((1, 1, HEAD_DIM, ncp), lambda b, g, i: (b, g, 0, 0)),
                  pl.BlockSpec((1, 1, ncp, HEAD_DIM), lambda b, g, i: (b, g, 0, 0))],
        out_specs=[pl.BlockSpec((tq, gw), lambda b, g, i: (b * nq + i, g)),
                   pl.BlockSpec((tq, LANES), lambda b, g, i: (b * nq + i, g))],
        out_shape=[jax.ShapeDtypeStruct((T, NSA_HEADS * HEAD_DIM), F32),
                   jax.ShapeDtypeStruct((T, G * LANES), BF16)],
        scratch_shapes=[pltpu.VMEM((tq, LANES), F32)],
        compiler_params=_cparams("parallel", "parallel", "parallel"),
        name="nsa_cmp_topk",
    )(pb, kT, vcmp)


def _win_kernel(q_ref, kp_ref, kc_ref, vp_ref, vc_ref, o_ref, *, sub):
    i = pl.program_id(2)
    row = lax.broadcasted_iota(jnp.int32, (sub, sub), 0)
    col = lax.broadcasted_iota(jnp.int32, (sub, sub), 1)
    ones = jnp.ones((sub, LANES), BF16)

    def body(first):
        ks = [kp_ref[0:sub, :], kp_ref[sub:2 * sub, :], kc_ref[0:sub, :], kc_ref[sub:2 * sub, :]]
        vs = [jnp.concatenate([v, ones], axis=1) for v in
              (vp_ref[0:sub, :], vp_ref[sub:2 * sub, :], vc_ref[0:sub, :], vc_ref[sub:2 * sub, :])]
        for a in range(2):
            blocks = [(a, col > row), (a + 1, None), (a + 2, col <= row)]
            if first:
                blocks = blocks[2 - a:]
            rows = slice(a * sub, (a + 1) * sub)
            for r in range(NSA_REP):
                sl = slice(r * HEAD_DIM, (r + 1) * HEAD_DIM)
                q = q_ref[rows, sl]
                ss = []
                for kb, mask in blocks:
                    s = _dot_nt(q, ks[kb])
                    ss.append(s if mask is None else jnp.where(mask, s, NEG))
                m = functools.reduce(jnp.maximum, ss)
                m = jnp.max(m, axis=-1, keepdims=True)
                acc = None
                for s, (kb, _) in zip(ss, blocks):
                    part = _dot(jnp.exp2(s - m).astype(BF16), vs[kb])
                    acc = part if acc is None else acc + part
                o_ref[rows, sl] = acc[:, 0:HEAD_DIM] / acc[:, HEAD_DIM:2 * HEAD_DIM]

    pl.when(i > 0)(functools.partial(body, False))
    pl.when(i == 0)(functools.partial(body, True))


def _window(pb, B, S, k_blk, v_blk):
    G = NSA_KV_GROUPS
    T = B * S
    sub = WINDOW // 2
    tq = 2 * sub
    nq = S // tq
    gw = NSA_REP * HEAD_DIM

    def kv_spec(col0, back):
        return pl.BlockSpec((tq, HEAD_DIM),
                            lambda b, g, i: (b * nq + jnp.maximum(i - back, 0), col0 + g))

    return pl.pallas_call(
        functools.partial(_win_kernel, sub=sub),
        grid=(B, G, nq),
        in_specs=[pl.BlockSpec((tq, gw), lambda b, g, i: (b * nq + i, g)),
                  kv_spec(k_blk, 1), kv_spec(k_blk, 0), kv_spec(v_blk, 1), kv_spec(v_blk, 0)],
        out_specs=pl.BlockSpec((tq, gw), lambda b, g, i: (b * nq + i, g)),
        out_shape=jax.ShapeDtypeStruct((T, NSA_HEADS * HEAD_DIM), F32),
        compiler_params=_cparams("parallel", "parallel", "parallel"),
        name="nsa_window",
    )(pb, pb, pb, pb, pb)


def _sel_kernel(qi_ref, kj_ref, q_ref, bias_ref, k_ref, v_ref, o_ref, qa_ref, m_ref, acc_ref, *, tq, tk):
    n = pl.program_id(2)
    i = qi_ref[n]
    j = kj_ref[n]
    last_j = (i * tq + tq - 1) // tk

    @pl.when(j == 0)
    def _():
        for r in range(NSA_REP):
            rows = slice(r * tq, (r + 1) * tq)
            qa_ref[rows, 0:HEAD_DIM] = q_ref[:, r * HEAD_DIM:(r + 1) * HEAD_DIM]
            qa_ref[rows, HEAD_DIM:2 * HEAD_DIM] = bias_ref[...]
        m_ref[...] = jnp.full(m_ref.shape, NEG, F32)
        acc_ref[...] = jnp.zeros(acc_ref.shape, F32)

    def step(col0, ncols, diagonal):
        key = j * tk + col0 + lax.broadcasted_iota(jnp.int32, (ncols, LANES), 0)
        blk = lax.broadcasted_iota(jnp.int32, (ncols, LANES), 1)
        onehot = (key // SEL_BLOCK == blk).astype(BF16)
        ka = jnp.concatenate([k_ref[col0:col0 + ncols, :], onehot], axis=1)
        va = jnp.concatenate([v_ref[col0:col0 + ncols, :], jnp.ones((ncols, LANES), BF16)], axis=1)
        if diagonal:
            visible = (lax.broadcasted_iota(jnp.int32, (tq, tq), 1)
                       <= lax.broadcasted_iota(jnp.int32, (tq, tq), 0))
        def scores(r):
            s = _dot_nt(qa_ref[r * tq:(r + 1) * tq, :], ka)
            return jnp.where(visible, s, NEG) if diagonal else s

        s_next = scores(0)
        for r in range(NSA_REP):
            rows = slice(r * tq, (r + 1) * tq)
            s = s_next
            if r + 1 < NSA_REP:
                s_next = scores(r + 1)
            m_prev = m_ref[rows, :]
            m_new = jnp.maximum(m_prev, jnp.max(s, axis=-1, keepdims=True))
            alpha = jnp.exp2(m_prev - m_new)
            p = jnp.exp2(s - jnp.tile(m_new, (1, ncols // LANES)))
            acc_ref[rows, :] = jnp.tile(alpha, (1, 2)) * acc_ref[rows, :] + _dot(p.astype(BF16), va)
            m_ref[rows, :] = m_new

    @pl.when(j < last_j)
    def _():
        step(0, tk, False)

    for sub in range(tk // tq):
        @pl.when(jnp.logical_and(j == last_j, i % (tk // tq) == sub))
        def _(sub=sub):
            if sub > 0:
                step(0, sub * tq, False)
            step(sub * tq, tq, True)

    @pl.when(j == last_j)
    def _():
        for r in range(NSA_REP):
            rows = slice(r * tq, (r + 1) * tq)
            o_ref[:, r * HEAD_DIM:(r + 1) * HEAD_DIM] = (acc_ref[rows, 0:HEAD_DIM]
                                                         / acc_ref[rows, HEAD_DIM:2 * HEAD_DIM])


def _selection(pb, bias, B, S, k_blk, v_blk):
    G = NSA_KV_GROUPS
    T = B * S
    tq = min(S, SEL_TQ)
    tk = min(S, SEL_TK)
    assert tk % tq == 0
    nq = S // tq
    nk = S // tk
    gw = NSA_REP * HEAD_DIM
    pairs = [(i, j) for i in range(nq) for j in range((i * tq + tq - 1) // tk + 1)]
    qi = jnp.asarray([p[0] for p in pairs], jnp.int32)
    kj = jnp.asarray([p[1] for p in pairs], jnp.int32)

    def q_map(b, g, n, qi_ref, kj_ref):
        return (b * nq + qi_ref[n], g)

    def kv_spec(col0):
        return pl.BlockSpec((tk, HEAD_DIM),
                            lambda b, g, n, qi_ref, kj_ref: (b * nk + kj_ref[n], col0 + g))

    grid_spec = pltpu.PrefetchScalarGridSpec(
        num_scalar_prefetch=2,
        grid=(B, G, len(pairs)),
        in_specs=[pl.BlockSpec((tq, gw), q_map), pl.BlockSpec((tq, LANES), q_map),
                  kv_spec(k_blk), kv_spec(v_blk)],
        out_specs=pl.BlockSpec((tq, gw), q_map),
        scratch_shapes=[pltpu.VMEM((NSA_REP * tq, 2 * HEAD_DIM), BF16),
                        pltpu.VMEM((NSA_REP * tq, LANES), F32),
                        pltpu.VMEM((NSA_REP * tq, 2 * HEAD_DIM), F32)])
    return pl.pallas_call(
        functools.partial(_sel_kernel, tq=tq, tk=tk),
        grid_spec=grid_spec,
        out_shape=jax.ShapeDtypeStruct((T, NSA_HEADS * HEAD_DIM), F32),
        compiler_params=_cparams("parallel", "parallel", "arbitrary"),
        name="nsa_selection",
    )(qi, kj, pb, bias, pb, pb)


def _nsa_out_kernel(x_ref, oc_ref, os_ref, ow_ref, z_ref, gl_ref, w_ref, nw_ref, o_ref, hn_ref):
    sig = jax.nn.sigmoid(gl_ref[...])
    H = NSA_HEADS
    acc = x_ref[...]
    for hp in range(H // 2):
        parts = []
        for h in (2 * hp, 2 * hp + 1):
            sl = slice(h * HEAD_DIM, (h + 1) * HEAD_DIM)
            o = (sig[:, h:h + 1] * oc_ref[:, sl] + sig[:, H + h:H + h + 1] * os_ref[:, sl]
                 + sig[:, 2 * H + h:2 * H + h + 1] * ow_ref[:, sl])
            parts.append((o * _silu(z_ref[:, sl])).astype(BF16))
        rows = slice(2 * hp * HEAD_DIM, (2 * hp + 2) * HEAD_DIM)
        acc = acc + _dot(jnp.concatenate(parts, axis=1), w_ref[rows, :])
    o_ref[...] = acc
    hn_ref[...] = _rms_bf16(acc, nw_ref[...])


def _nsa_out(x, o_cmp, o_sel, o_win, pf, z_blk, gl_blk, w_out, layer, next_norm_w):
    T, D = x.shape
    W = o_cmp.shape[1]
    tm = min(T, OUT_TM)
    row = lambda i: (i, 0)
    return pl.pallas_call(
        _nsa_out_kernel,
        grid=(T // tm,),
        in_specs=[pl.BlockSpec((tm, D), row),
                  pl.BlockSpec((tm, W), row), pl.BlockSpec((tm, W), row), pl.BlockSpec((tm, W), row),
                  pl.BlockSpec((tm, W), lambda i: (i, z_blk)),
                  pl.BlockSpec((tm, LANES), lambda i: (i, gl_blk)),
                  pl.BlockSpec((None, W, D), lambda i: (layer, 0, 0)),
                  pl.BlockSpec((1, D), lambda i: (0, 0))],
        out_specs=[pl.BlockSpec((tm, D), row), pl.BlockSpec((tm, D), row)],
        out_shape=[jax.ShapeDtypeStruct((T, D), F32), jax.ShapeDtypeStruct((T, D), BF16)],
        compiler_params=_cparams("parallel"),
        name="nsa_out",
    )(x, o_cmp, o_sel, o_win, pf, pf, w_out, next_norm_w.reshape(1, D))


def _prepare_params(nsa_w_in, nsa_ck_pos, nsa_ck_w1, nsa_ck_w2, nsa_cv_pos, nsa_cv_w1, nsa_cv_w2,
                    nsa_w_out, hgrn_w_in, hgrn_w_out):
    H, G, d = NSA_HEADS, NSA_KV_GROUPS, HEAD_DIM
    kvw = G * d
    c = [0, H * d]
    for _ in range(6):
        c.append(c[-1] + kvw)
    c.append(c[-1] + 3 * H)
    c.append(c[-1] + H * d)
    seg = lambda n: nsa_w_in[:, :, c[n]:c[n + 1]]
    wq, wkc, wvc, wks, wvs, wkw, wvw, wgl, wz = [seg(n) for n in range(9)]
    wq, wkc, wks, wkw = [_permute_head_dims(t, 2) for t in (wq, wkc, wks, wkw)]
    pad = jnp.zeros(nsa_w_in.shape[:2] + (LANES - 3 * H,), nsa_w_in.dtype)
    return dict(
        nsa_wb=jnp.concatenate([wq, wks, wkw, wvs, wvw], axis=2).astype(BF16),
        nsa_wf=jnp.concatenate([wz, wkc, wvc, wgl, pad], axis=2).astype(BF16),
        ck_pos=_permute_head_dims(nsa_ck_pos, 2),
        ck_w1=_permute_head_dims(nsa_ck_w1, 2).astype(BF16),
        ck_w2=_permute_head_dims(nsa_ck_w2, 2).astype(BF16),
        cv_pos=nsa_cv_pos, cv_w1=nsa_cv_w1.astype(BF16), cv_w2=nsa_cv_w2.astype(BF16),
        nsa_wo=nsa_w_out.astype(BF16),
        hgrn_wi=hgrn_w_in.astype(BF16), hgrn_wo=hgrn_w_out.astype(BF16))


def _nsa_layer(x, hn, B, S, tabs, prm, j, next_norm_w):
    H, G = NSA_HEADS, NSA_KV_GROUPS
    pb = _proj(hn, prm["nsa_wb"], j, BF16, 1024, rope=(0, H + 2 * G), tabs=tabs, q_heads=H)
    pf = _proj(hn, prm["nsa_wf"], j, F32, 640, rope=(H, H + G), tabs=tabs)
    ks_blk, kw_blk, vs_blk, vw_blk = H, H + G, H + 2 * G, H + 3 * G
    kc_blk, vc_blk, gl_blk = H, H + G, H + 2 * G

    kT, vcmp = _compress(pf, B, S, kc_blk, vc_blk, j, prm["ck_pos"], prm["ck_w1"], prm["ck_w2"],
                         prm["cv_pos"], prm["cv_w1"], prm["cv_w2"])
    o_cmp, bias = _cmp_topk(pb, kT, vcmp, B, S)
    o_win = _window(pb, B, S, kw_blk, vw_blk)
    o_sel = _selection(pb, bias, B, S, ks_blk, vs_blk)
    return _nsa_out(x, o_cmp, o_sel, o_win, pf, 0, gl_blk, prm["nsa_wo"], j, next_norm_w)


def _hgrn_kernel(q_ref, f_ref, i_ref, z_ref, lbl_ref, gw_ref, o_ref,
                 st_ref, *, C, sub, hb, layer):
    n = pl.program_id(2)

    @pl.when(n == 0)
    def _():
        st_ref[...] = jnp.zeros(st_ref.shape, F32)

    r0 = lax.broadcasted_iota(jnp.int32, (C, C), 0)
    r1 = lax.broadcasted_iota(jnp.int32, (C, C), 1)
    tri = (r0 >= r1).astype(BF16)
    dsub = sub // 2
    drow = lax.broadcasted_iota(jnp.int32, (C // dsub, dsub, HEAD_DIM), 1)
    row_id = lax.broadcasted_iota(jnp.int32, (C, HEAD_DIM), 0)

    for h in range(hb):
        cs = slice(h * HEAD_DIM, (h + 1) * HEAD_DIM)
        lg = lbl_ref[:, cs]
        e = jnp.exp(lg - jnp.max(lg, axis=0, keepdims=True))
        p = e / jnp.sum(e, axis=0, keepdims=True)
        lb = jnp.sum(p[0:layer + 1, :], axis=0, keepdims=True) - p[0:1, :]

        fl = f_ref[:, cs]
        t = jnp.exp(-jnp.abs(fl))
        r = 1.0 / (1.0 + t)
        pos = fl >= 0.0
        sig_p = jnp.where(pos, r, t * r)
        sig_n = jnp.where(pos, t * r, r)
        k = (1.0 - lb) * sig_n
        lc = jnp.minimum(fl, 0.0) - jnp.log(1.0 + t) + jnp.log1p(-lb)
        g = jnp.maximum(jnp.log(lb + (1.0 - lb) * sig_p), lc)
        qs = _silu(q_ref[:, cs])
        iv = i_ref[:, cs]
        ib = iv.astype(BF16)

        g1 = g.astype(BF16)
        rem = g - g1.astype(F32)
        g2 = rem.astype(BF16)
        g3 = (rem - g2.astype(F32)).astype(BF16)
        b = _dot(tri, g1) + _dot(tri, g2) + _dot(tri, g3)

        stT = st_ref[h]
        o_inter = _dot_nt((qs * jnp.exp(b)).astype(BF16), stT.astype(BF16))

        nd = C // dsub
        b3 = b.reshape(nd, dsub, HEAD_DIM)
        q3 = qs.reshape(nd, dsub, HEAD_DIM)
        k3 = k.reshape(nd, dsub, HEAD_DIM)
        i3 = iv.reshape(nd, dsub, HEAD_DIM)
        o3 = jnp.zeros((nd, dsub, HEAD_DIM), F32)
        for s in range(dsub):
            w = jnp.where(drow >= s, jnp.exp(b3 - b3[:, s:s + 1, :]), 0.0)
            col = jnp.sum(q3 * w * k3[:, s:s + 1, :], axis=-1, keepdims=True)
            o3 = o3 + col * i3[:, s:s + 1, :]

        att = None
        bsz = 2 * dsub
        while bsz <= C:
            bl = b.reshape(C // bsz, bsz, HEAD_DIM)
            bmid = jnp.broadcast_to(bl[:, bsz // 2 - 1:bsz // 2, :], bl.shape).reshape(C, HEAD_DIM)
            second = (row_id % bsz) >= bsz // 2
            qk = jnp.where(second, qs, k) * jnp.exp(-jnp.abs(b - bmid))
            qh = jnp.where(second, qk, 0.0).astype(BF16)
            kh = jnp.where(second, 0.0, qk).astype(BF16)
            a = _dot_nt(qh, kh)
            if bsz < C:
                a = jnp.where(r0 // bsz == r1 // bsz, a, 0.0)
            att = a if att is None else att + a
            bsz *= 2
        o = o_inter + o3.reshape(C, HEAD_DIM) + _dot(att.astype(BF16), ib)

        b_last = b[C - 1:C, :]
        kd = (k * jnp.exp(b_last - b)).astype(BF16)
        st_ref[h] = stT * jnp.exp(b_last) + _dot(iv.T.astype(BF16), kd)

        o = o * lax.rsqrt(jnp.mean(o * o, axis=-1, keepdims=True) + RMS_EPS) * gw_ref[...]
        o_ref[:, cs] = (o * _silu(z_ref[:, cs])).astype(BF16)


def _hgrn_recurrence(ph, lb_logits, gnorm_w, B, S, layer):
    H = HGRN_HEADS
    T = B * S
    C = min(S, HGRN_CHUNK)
    nc = S // C
    NL = lb_logits.shape[0]
    hb = HGRN_HEADS_PER_STEP
    hg = H // hb
    w = hb * HEAD_DIM

    def col(c0):
        return pl.BlockSpec((C, w), lambda b, h, n: (b * nc + n, c0 + h))

    return pl.pallas_call(
        functools.partial(_hgrn_kernel, C=C, sub=HGRN_SUB, hb=hb, layer=layer),
        grid=(B, hg, nc),
        in_specs=[col(0), col(hg), col(2 * hg), col(3 * hg),
                  pl.BlockSpec((NL, w), lambda b, h, n: (0, h)),
                  pl.BlockSpec((1, HEAD_DIM), lambda b, h, n: (0, 0))],
        out_specs=pl.BlockSpec((C, w), lambda b, h, n: (b * nc + n, h)),
        out_shape=jax.ShapeDtypeStruct((T, H * HGRN_I_DIM), BF16),
        scratch_shapes=[pltpu.VMEM((hb, HGRN_I_DIM, HGRN_F_DIM), F32)],
        compiler_params=_cparams("parallel", "parallel", "arbitrary"),
        name="hgrn_recurrence",
    )(ph, ph, ph, ph, lb_logits, gnorm_w.reshape(1, HEAD_DIM))


def _hgrn_out_kernel(x_ref, og_ref, w_ref, nw_ref, *out_refs, final_norm):
    y = x_ref[...] + _dot(og_ref[...], w_ref[...])
    if final_norm:
        ms = jnp.mean(y * y, axis=-1, keepdims=True)
        out_refs[0][...] = y * lax.rsqrt(ms + RMS_EPS) * nw_ref[...]
    else:
        out_refs[0][...] = y
        out_refs[1][...] = _rms_bf16(y, nw_ref[...])


def _hgrn_out(x, og, w_out, layer, norm_w, final_norm):
    T, D = x.shape
    W = og.shape[1]
    tm = min(T, 2 * OUT_TM)
    row = lambda i: (i, 0)
    out_specs = [pl.BlockSpec((tm, D), row)]
    out_shape = [jax.ShapeDtypeStruct((T, D), F32)]
    if not final_norm:
        out_specs.append(pl.BlockSpec((tm, D), row))
        out_shape.append(jax.ShapeDtypeStruct((T, D), BF16))
    return pl.pallas_call(
        functools.partial(_hgrn_out_kernel, final_norm=final_norm),
        grid=(T // tm,),
        in_specs=[pl.BlockSpec((tm, D), row), pl.BlockSpec((tm, W), row),
                  pl.BlockSpec((None, W, D), lambda i: (layer, 0, 0)),
                  pl.BlockSpec((1, D), lambda i: (0, 0))],
        out_specs=out_specs,
        out_shape=out_shape,
        compiler_params=_cparams("parallel"),
        name="hgrn_out",
    )(x, og, w_out, norm_w.reshape(1, D))


def _hgrn_layer(x, hn, B, S, prm, lb_logits, gnorm_w, j, norm_w, final_norm):
    ph = _proj(hn, prm["hgrn_wi"], j, F32, 1024)
    og = _hgrn_recurrence(ph, lb_logits, gnorm_w, B, S, j)
    return _hgrn_out(x, og, prm["hgrn_wo"], j, norm_w, final_norm)


def kernel(x, positions, norm_w, final_norm_w, nsa_w_in, nsa_ck_pos, nsa_ck_w1, nsa_ck_w2,
           nsa_cv_pos, nsa_cv_w1, nsa_cv_w2, nsa_w_out, hgrn_w_in, hgrn_lb_logits,
           hgrn_gnorm_w, hgrn_w_out):
    B, S, D = x.shape
    depth = norm_w.shape[0]
    assert depth % 2 == 0
    xf = x.reshape(B * S, D)
    tabs = _rope_tables(positions)
    prm = _prepare_params(nsa_w_in, nsa_ck_pos, nsa_ck_w1, nsa_ck_w2, nsa_cv_pos, nsa_cv_w1,
                          nsa_cv_w2, nsa_w_out, hgrn_w_in, hgrn_w_out)
    hn = _rmsnorm(xf, norm_w[0])
    for layer in range(depth):
        j = layer // 2
        last = layer == depth - 1
        if layer % 2 == 0:
            xf, hn = _nsa_layer(xf, hn, B, S, tabs, prm, j, norm_w[layer + 1])
        elif last:
            (xf,) = _hgrn_layer(xf, hn, B, S, prm, hgrn_lb_logits, hgrn_gnorm_w[j], j, final_norm_w, True)
        else:
            xf, hn = _hgrn_layer(xf, hn, B, S, prm, hgrn_lb_logits, hgrn_gnorm_w[j], j,
                                 norm_w[layer + 1], False)
    return xf.reshape(B, S, D)
```

```python
import functools
import math

import jax
import jax.numpy as jnp
from jax import lax
from jax.experimental import pallas as pl
from jax.experimental.pallas import tpu as pltpu

F32 = jnp.float32
BF16 = jnp.bfloat16

RMS_EPS = 1e-6
NEG = -1e30
BELOW_NEG = -3e38
HEAD_DIM = 128
NSA_HEADS = 16
NSA_KV_GROUPS = 4
NSA_REP = NSA_HEADS // NSA_KV_GROUPS
CMP_BLOCK = 32
CMP_STRIDE = 16
SEL_BLOCK = 64
SEL_TOPK = 16
WINDOW = 512
ROPE_THETA = 500000.0
ROPE_DIM = HEAD_DIM // 4
HGRN_HEADS = 16
HGRN_F_DIM = 128
HGRN_I_DIM = 128

LANES = 128
VMEM_LIMIT_BYTES = 56 * 1024 * 1024

ATTN_SCALE = HEAD_DIM ** -0.5
EXP2_SCALE = ATTN_SCALE * math.log2(math.e)

PROJ_TM = 2048
PROJ_CHUNK_HEADS = 2
CMP_TQ = 512
SEL_TQ = 512
SEL_TK = 1024
OUT_TM = 256
HGRN_CHUNK = 128
HGRN_SUB = 16
HGRN_HEADS_PER_STEP = 4


def _cparams(*sem):
    return pltpu.CompilerParams(dimension_semantics=sem, vmem_limit_bytes=VMEM_LIMIT_BYTES)


def _dot(a, b):
    return jnp.dot(a, b, preferred_element_type=F32)


def _dot_nt(a, b):
    return lax.dot_general(a, b, (((1,), (1,)), ((), ())), preferred_element_type=F32)


def _silu(v):
    return v * jax.nn.sigmoid(v)


_HALF = ROPE_DIM // 2
_PARTNER_SHIFT = LANES // 2
ROPE_PERM = (list(range(_HALF))
             + list(range(ROPE_DIM, ROPE_DIM + _PARTNER_SHIFT - _HALF))
             + list(range(_HALF, ROPE_DIM))
             + list(range(ROPE_DIM + _PARTNER_SHIFT - _HALF, HEAD_DIM)))


def _permute_head_dims(w, axis):
    axis = axis % w.ndim
    n = w.shape[axis] // HEAD_DIM
    shape = w.shape[:axis] + (n, HEAD_DIM) + w.shape[axis + 1:]
    wr = w.reshape(shape)
    cuts = [(0, _HALF), (ROPE_DIM, ROPE_DIM + _PARTNER_SHIFT - _HALF), (_HALF, ROPE_DIM),
            (ROPE_DIM + _PARTNER_SHIFT - _HALF, HEAD_DIM)]
    parts = [lax.slice_in_dim(wr, lo, hi, axis=axis + 1) for lo, hi in cuts]
    return jnp.concatenate(parts, axis=axis + 1).reshape(w.shape)


def _rope_table_kernel(pos_ref, inv_ref, cos_ref, sin_ref):
    ang = pos_ref[...] * inv_ref[...]
    lane = lax.broadcasted_iota(jnp.int32, ang.shape, 1)
    s = jnp.sin(ang)
    cos_ref[...] = jnp.cos(ang)
    sin_ref[...] = jnp.where(lane < _HALF, -s, s)


def _rope_tables(positions):
    T = positions.size
    tm = min(T, 2048)
    inv = ROPE_THETA ** (-jnp.arange(_HALF, dtype=F32) / _HALF)
    inv_pat = jnp.concatenate([inv, jnp.zeros((_PARTNER_SHIFT - _HALF,), F32), inv,
                               jnp.zeros((LANES - _PARTNER_SHIFT - _HALF,), F32)]).reshape(1, LANES)
    pos = positions.astype(F32).reshape(T, 1)
    return pl.pallas_call(
        _rope_table_kernel,
        grid=(T // tm,),
        in_specs=[pl.BlockSpec((tm, 1), lambda i: (i, 0)),
                  pl.BlockSpec((1, LANES), lambda i: (0, 0))],
        out_specs=[pl.BlockSpec((tm, LANES), lambda i: (i, 0))] * 2,
        out_shape=[jax.ShapeDtypeStruct((T, LANES), F32)] * 2,
        compiler_params=_cparams("parallel"),
        name="rope_tables",
    )(pos, inv_pat)


def _rms_bf16(y, w):
    ms = jnp.mean(y * y, axis=-1, keepdims=True)
    return (y * lax.rsqrt(ms + RMS_EPS) * w).astype(BF16)


def _rmsnorm_kernel(x_ref, w_ref, o_ref):
    o_ref[...] = _rms_bf16(x_ref[...], w_ref[...])


def _rmsnorm(x, w):
    T, D = x.shape
    tm = min(T, 2 * OUT_TM)
    return pl.pallas_call(
        _rmsnorm_kernel,
        grid=(T // tm,),
        in_specs=[pl.BlockSpec((tm, D), lambda i: (i, 0)), pl.BlockSpec((1, D), lambda i: (0, 0))],
        out_specs=pl.BlockSpec((tm, D), lambda i: (i, 0)),
        out_shape=jax.ShapeDtypeStruct((T, D), BF16),
        compiler_params=_cparams("parallel"),
        name="rmsnorm",
    )(x, w.reshape(1, D))


def _log_forget(t, lbl, layer):
    e = jnp.exp(lbl - jnp.max(lbl, axis=0, keepdims=True))
    p = e / jnp.sum(e, axis=0, keepdims=True)
    lb = jnp.sum(p[0:layer + 1, :], axis=0, keepdims=True) - p[0:1, :]
    tt = jnp.exp(-jnp.abs(t))
    r = 1.0 / (1.0 + tt)
    sig = jnp.where(t >= 0.0, r, tt * r)
    floor = jnp.minimum(t, 0.0) - jnp.log(1.0 + tt) + jnp.log1p(-lb)
    return jnp.maximum(jnp.log(lb + (1.0 - lb) * sig), floor)


def _proj_kernel(*refs, kinds_of_tile, has_rope, layer):
    if has_rope:
        hn_ref, w_ref, cos_ref, sin_ref, o_ref = refs
    elif any(isinstance(k, tuple) for kinds in kinds_of_tile for k in kinds):
        hn_ref, w_ref, lbl_ref, o_ref = refs
    else:
        hn_ref, w_ref, o_ref = refs
    j = pl.program_id(1)
    heads_per_tile = len(kinds_of_tile[0])

    def tile_body(kinds):
        h = 0
        while h < heads_per_tile:
            nh = min(PROJ_CHUNK_HEADS, heads_per_tile - h)
            acc = _dot(hn_ref[...], w_ref[:, h * HEAD_DIM:(h + nh) * HEAD_DIM].astype(BF16))
            for hh in range(nh):
                sl = slice((h + hh) * HEAD_DIM, (h + hh + 1) * HEAD_DIM)
                t = acc[:, hh * HEAD_DIM:(hh + 1) * HEAD_DIM]
                kind = kinds[h + hh]
                if kind in ('rope', 'rope_q'):
                    t = t * cos_ref[...] + pltpu.roll(t, _PARTNER_SHIFT, axis=1) * sin_ref[...]
                if kind == 'rope_q':
                    t = t * EXP2_SCALE
                elif kind == 'silu':
                    t = _silu(t)
                elif isinstance(kind, tuple):
                    t = _log_forget(t, lbl_ref[:, kind[1] * HEAD_DIM:(kind[1] + 1) * HEAD_DIM], layer)
                o_ref[:, sl] = t.astype(o_ref.dtype)
            h += nh

    patterns = {}
    for jj, kinds in enumerate(kinds_of_tile):
        patterns.setdefault(kinds, []).append(jj)
    for kinds, tiles in patterns.items():
        cond = functools.reduce(jnp.logical_or, [j == jj for jj in tiles])
        pl.when(cond)(functools.partial(tile_body, kinds))


def _proj(hn, w, layer, out_dtype, tn, kinds, tabs=None, lb_logits=None):
    T, D = hn.shape
    N = w.shape[2]
    tm = min(T, PROJ_TM)
    hpt = tn // HEAD_DIM
    kinds_of_tile = tuple(tuple(kinds[jj * hpt:(jj + 1) * hpt]) for jj in range(N // tn))
    in_specs = [pl.BlockSpec((tm, D), lambda i, j: (i, 0)),
                pl.BlockSpec((None, D, tn), lambda i, j: (layer, 0, j))]
    args = [hn, w]
    if tabs is not None:
        in_specs += [pl.BlockSpec((tm, LANES), lambda i, j: (i, 0))] * 2
        args += list(tabs)
    if lb_logits is not None:
        in_specs.append(pl.BlockSpec(lb_logits.shape, lambda i, j: (0, 0)))
        args.append(lb_logits)
    return pl.pallas_call(
        functools.partial(_proj_kernel, kinds_of_tile=kinds_of_tile, has_rope=tabs is not None, layer=layer),
        grid=(T // tm, N // tn),
        in_specs=in_specs,
        out_specs=pl.BlockSpec((tm, tn), lambda i, j: (i, j)),
        out_shape=jax.ShapeDtypeStruct((T, N), out_dtype),
        compiler_params=_cparams("parallel", "arbitrary"),
        name="in_proj",
    )(*args)


def _compress_kernel(kc_ref, vc_ref, kpos_ref, kw1_ref, kw2_ref, vpos_ref, vw1_ref, vw2_ref,
                     kT_ref, v_ref, shift_ref, *, ncp):
    half_blk = CMP_BLOCK // 2

    def comp(t_ref, pos_ref, w1_ref, w2_ref):
        a = jnp.zeros((ncp, HEAD_DIM), F32)
        bm = jnp.zeros((ncp, HEAD_DIM), F32)
        for l in range(half_blk):
            xl = t_ref[pl.ds(l, ncp, stride=CMP_STRIDE), :]
            a = a + _dot((xl + pos_ref[l:l + 1, :]).astype(BF16), w1_ref[l])
            bm = bm + _dot((xl + pos_ref[half_blk + l:half_blk + l + 1, :]).astype(BF16),
                           w1_ref[half_blk + l])
        shift_ref[pl.ds(0, ncp), :] = bm
        shift_ref[pl.ds(ncp, 8), :] = jnp.zeros((8, HEAD_DIM), F32)
        hid = _silu(a + shift_ref[pl.ds(1, ncp), :])
        out = _dot(hid.astype(BF16), w2_ref[...])
        row = lax.broadcasted_iota(jnp.int32, out.shape, 0)
        return jnp.where(row < ncp - 1, out, 0.0)

    kT_ref[0, 0] = comp(kc_ref, kpos_ref, kw1_ref, kw2_ref).T.astype(BF16)
    v_ref[0, 0] = comp(vc_ref, vpos_ref, vw1_ref, vw2_ref).astype(BF16)


def _compress(pf, B, S, kc_blk, vc_blk, layer, ck_pos, ck_w1, ck_w2, cv_pos, cv_w1, cv_w2):
    G = NSA_KV_GROUPS
    ncp = S // CMP_STRIDE
    full2 = lambda b, g: (layer, 0, 0)
    full3 = lambda b, g: (layer, 0, 0, 0)
    return pl.pallas_call(
        functools.partial(_compress_kernel, ncp=ncp),
        grid=(B, G),
        in_specs=[pl.BlockSpec((S, HEAD_DIM), lambda b, g: (b, kc_blk + g)),
                  pl.BlockSpec((S, HEAD_DIM), lambda b, g: (b, vc_blk + g)),
                  pl.BlockSpec((None, CMP_BLOCK, HEAD_DIM), full2),
                  pl.BlockSpec((None, CMP_BLOCK, HEAD_DIM, HEAD_DIM), full3),
                  pl.BlockSpec((None, HEAD_DIM, HEAD_DIM), full2),
                  pl.BlockSpec((None, CMP_BLOCK, HEAD_DIM), full2),
                  pl.BlockSpec((None, CMP_BLOCK, HEAD_DIM, HEAD_DIM), full3),
                  pl.BlockSpec((None, HEAD_DIM, HEAD_DIM), full2)],
        out_specs=[pl.BlockSpec((1, 1, HEAD_DIM, ncp), lambda b, g: (b, g, 0, 0)),
                   pl.BlockSpec((1, 1, ncp, HEAD_DIM), lambda b, g: (b, g, 0, 0))],
        out_shape=[jax.ShapeDtypeStruct((B, G, HEAD_DIM, ncp), BF16),
                   jax.ShapeDtypeStruct((B, G, ncp, HEAD_DIM), BF16)],
        scratch_shapes=[pltpu.VMEM((ncp + 8, HEAD_DIM), F32)],
        compiler_params=_cparams("parallel", "parallel"),
        name="nsa_compress",
    )(pf, pf, ck_pos, ck_w1, ck_w2, cv_pos, cv_w1, cv_w2)


def _cmp_topk_kernel(q_ref, kT_ref, v_ref, o_ref, bias_ref, imp_ref, *, tq, ncp):
    i = pl.program_id(2)
    s0 = i * tq
    t_col = s0 + lax.broadcasted_iota(jnp.int32, (tq, 1), 0)
    any_visible = (t_col >= CMP_BLOCK - 1).astype(F32)

    def attend(nb):
        nc = nb * LANES
        n_row = lax.broadcasted_iota(jnp.int32, (1, nc), 1)
        mc = jnp.logical_and(n_row * CMP_STRIDE + (CMP_BLOCK - 1) <= t_col, n_row < ncp - 1)
        kT = kT_ref[0, 0, :, 0:nc]
        v = v_ref[0, 0, 0:nc, :]
        psum = jnp.zeros((tq, nc), F32)
        for r in range(NSA_REP):
            sl = slice(r * HEAD_DIM, (r + 1) * HEAD_DIM)
            s = jnp.where(mc, _dot(q_ref[:, sl], kT), NEG)
            m = jnp.max(s, axis=-1, keepdims=True)
            e = jnp.exp2(s - m)
            l = jnp.sum(e, axis=-1, keepdims=True)
            pc = e * (any_visible / l)
            o_ref[:, sl] = _dot(pc.astype(BF16), v)
            psum = psum + pc
        n_col = lax.broadcasted_iota(jnp.int32, (nc, LANES), 0) * CMP_STRIDE
        k_row = lax.broadcasted_iota(jnp.int32, (nc, LANES), 1) * SEL_BLOCK
        ovl = jnp.logical_and(n_col < k_row + SEL_BLOCK, n_col + CMP_BLOCK > k_row).astype(BF16)
        hi = psum.astype(BF16)
        lo = (psum - hi.astype(F32)).astype(BF16)
        imp_ref[...] = _dot(hi, ovl) + _dot(lo, ovl)

    nb_total = ncp // LANES
    last_visible = (s0 + tq - CMP_BLOCK) // CMP_STRIDE
    nb_needed = jnp.minimum(last_visible // LANES + 1, nb_total)
    for nb in range(1, nb_total + 1):
        pl.when(nb_needed == nb)(functools.partial(attend, nb))

    blk = lax.broadcasted_iota(jnp.int32, (LANES, LANES), 0)
    lane = lax.broadcasted_iota(jnp.int32, (LANES, LANES), 1)
    groups = []
    for c in range(0, tq, LANES):
        impT = imp_ref[c:c + LANES, :].T
        cur = (s0 + c + lane) // SEL_BLOCK
        forced = jnp.logical_or(blk == 0, jnp.logical_or(blk == cur, blk == cur - 1))
        groups.append(jnp.where(forced, BELOW_NEG, jnp.where(blk <= cur, impT, NEG)))
    for _ in range(SEL_TOPK - 3):
        nxt = []
        for sc in groups:
            m = jnp.max(sc, axis=0, keepdims=True)
            idx = jnp.min(jnp.where(sc == m, blk, LANES), axis=0, keepdims=True)
            nxt.append(jnp.where(blk == idx, BELOW_NEG, sc))
        groups = nxt
    for g, sc in enumerate(groups):
        biasT = jnp.where(sc == BELOW_NEG, 0.0, NEG)
        bias_ref[g * LANES:(g + 1) * LANES, :] = biasT.T.astype(BF16)


def _cmp_topk(pb, kT, vcmp, B, S):
    G = NSA_KV_GROUPS
    T = B * S
    tq = min(S, CMP_TQ)
    nq = S // tq
    ncp = S // CMP_STRIDE
    gw = NSA_REP * HEAD_DIM
    return pl.pallas_call(
        functools.partial(_cmp_topk_kernel, tq=tq, ncp=ncp),
        grid=(B, G, nq),
        in_specs=[pl.BlockSpec((tq, gw), lambda b, g, i: (b * nq + i, g)),
                  pl.BlockSpec((1, 1, HEAD_DIM, ncp), lambda b, g, i: (b, g, 0, 0)),
                  pl.BlockSpec((1, 1, ncp, HEAD_DIM), lambda b, g, i: (b, g, 0, 0))],
        out_specs=[pl.BlockSpec((tq, gw), lambda b, g, i: (b * nq + i, g)),
                   pl.BlockSpec((tq, LANES), lambda b, g, i: (b * nq + i, g))],
        out_shape=[jax.ShapeDtypeStruct((T, NSA_HEADS * HEAD_DIM), F32),
                   jax.ShapeDtypeStruct((T, G * LANES), BF16)],
        scratch_shapes=[pltpu.VMEM((tq, LANES), F32)],
        compiler_params=_cparams("parallel", "parallel", "parallel"),
        name="nsa_cmp_topk",
    )(pb, kT, vcmp)


def _win_kernel(q_ref, kp_ref, kc_ref, vp_ref, vc_ref, o_ref, *, sub):
    i = pl.program_id(2)
    row = lax.broadcasted_iota(jnp.int32, (sub, sub), 0)
    col = lax.broadcasted_iota(jnp.int32, (sub, sub), 1)
    ones = jnp.ones((sub, LANES), BF16)

    def body(first):
        ks = [kp_ref[0:sub, :], kp_ref[sub:2 * sub, :], kc_ref[0:sub, :], kc_ref[sub:2 * sub, :]]
        vs = [jnp.concatenate([v, ones], axis=1) for v in
              (vp_ref[0:sub, :], vp_ref[sub:2 * sub, :], vc_ref[0:sub, :], vc_ref[sub:2 * sub, :])]
        for a in range(2):
            blocks = [(a, col > row), (a + 1, None), (a + 2, col <= row)]
            if first:
                blocks = blocks[2 - a:]
            rows = slice(a * sub, (a + 1) * sub)
            for r in range(NSA_REP):
                sl = slice(r * HEAD_DIM, (r + 1) * HEAD_DIM)
                q = q_ref[rows, sl]
                ss = []
                for kb, mask in blocks:
                    s = _dot_nt(q, ks[kb])
                    ss.append(s if mask is None else jnp.where(mask, s, NEG))
                m = functools.reduce(jnp.maximum, ss)
                m = jnp.max(m, axis=-1, keepdims=True)
                acc = None
                for s, (kb, _) in zip(ss, blocks):
                    part = _dot(jnp.exp2(s - m).astype(BF16), vs[kb])
                    acc = part if acc is None else acc + part
                o_ref[rows, sl] = acc[:, 0:HEAD_DIM] / acc[:, HEAD_DIM:2 * HEAD_DIM]

    pl.when(i > 0)(functools.partial(body, False))
    pl.when(i == 0)(functools.partial(body, True))


def _window(pb, B, S, k_blk, v_blk):
    G = NSA_KV_GROUPS
    T = B * S
    sub = WINDOW // 2
    tq = 2 * sub
    nq = S // tq
    gw = NSA_REP * HEAD_DIM

    def kv_spec(col0, back):
        return pl.BlockSpec((tq, HEAD_DIM),
                            lambda b, g, i: (b * nq + jnp.maximum(i - back, 0), col0 + g))

    return pl.pallas_call(
        functools.partial(_win_kernel, sub=sub),
        grid=(B, G, nq),
        in_specs=[pl.BlockSpec((tq, gw), lambda b, g, i: (b * nq + i, g)),
                  kv_spec(k_blk, 1), kv_spec(k_blk, 0), kv_spec(v_blk, 1), kv_spec(v_blk, 0)],
        out_specs=pl.BlockSpec((tq, gw), lambda b, g, i: (b * nq + i, g)),
        out_shape=jax.ShapeDtypeStruct((T, NSA_HEADS * HEAD_DIM), F32),
        compiler_params=_cparams("parallel", "parallel", "parallel"),
        name="nsa_window",
    )(pb, pb, pb, pb, pb)


def _sel_kernel(qi_ref, kj_ref, q_ref, bias_ref, k_ref, v_ref, o_ref, qa_ref, m_ref, acc_ref, *, tq, tk):
    n = pl.program_id(2)
    i = qi_ref[n]
    j = kj_ref[n]
    last_j = (i * tq + tq - 1) // tk

    @pl.when(j == 0)
    def _():
        for r in range(NSA_REP):
            rows = slice(r * tq, (r + 1) * tq)
            qa_ref[rows, 0:HEAD_DIM] = q_ref[:, r * HEAD_DIM:(r + 1) * HEAD_DIM]
            qa_ref[rows, HEAD_DIM:2 * HEAD_DIM] = bias_ref[...]
        m_ref[...] = jnp.full(m_ref.shape, NEG, F32)
        acc_ref[...] = jnp.zeros(acc_ref.shape, F32)

    def step(col0, ncols, diagonal):
        key = j * tk + col0 + lax.broadcasted_iota(jnp.int32, (ncols, LANES), 0)
        blk = lax.broadcasted_iota(jnp.int32, (ncols, LANES), 1)
        onehot = (key // SEL_BLOCK == blk).astype(BF16)
        ka = jnp.concatenate([k_ref[col0:col0 + ncols, :], onehot], axis=1)
        va = jnp.concatenate([v_ref[col0:col0 + ncols, :], jnp.ones((ncols, LANES), BF16)], axis=1)
        if diagonal:
            visible = (lax.broadcasted_iota(jnp.int32, (tq, tq), 1)
                       <= lax.broadcasted_iota(jnp.int32, (tq, tq), 0))
        def scores(r):
            s = _dot_nt(qa_ref[r * tq:(r + 1) * tq, :], ka)
            return jnp.where(visible, s, NEG) if diagonal else s

        s_next = scores(0)
        for r in range(NSA_REP):
            rows = slice(r * tq, (r + 1) * tq)
            s = s_next
            if r + 1 < NSA_REP:
                s_next = scores(r + 1)
            m_prev = m_ref[rows, :]
            m_new = jnp.maximum(m_prev, jnp.max(s, axis=-1, keepdims=True))
            alpha = jnp.exp2(m_prev - m_new)
            p = jnp.exp2(s - jnp.tile(m_new, (1, ncols // LANES)))
            acc_ref[rows, :] = jnp.tile(alpha, (1, 2)) * acc_ref[rows, :] + _dot(p.astype(BF16), va)
            m_ref[rows, :] = m_new

    @pl.when(j < last_j)
    def _():
        step(0, tk, False)

    for sub in range(tk // tq):
        @pl.when(jnp.logical_and(j == last_j, i % (tk // tq) == sub))
        def _(sub=sub):
            if sub > 0:
                step(0, sub * tq, False)
            step(sub * tq, tq, True)

    @pl.when(j == last_j)
    def _():
        for r in range(NSA_REP):
            rows = slice(r * tq, (r + 1) * tq)
            o_ref[:, r * HEAD_DIM:(r + 1) * HEAD_DIM] = (acc_ref[rows, 0:HEAD_DIM]
                                                         / acc_ref[rows, HEAD_DIM:2 * HEAD_DIM])


def _selection(pb, bias, B, S, k_blk, v_blk):
    G = NSA_KV_GROUPS
    T = B * S
    tq = min(S, SEL_TQ)
    tk = min(S, SEL_TK)
    assert tk % tq == 0
    nq = S // tq
    nk = S // tk
    gw = NSA_REP * HEAD_DIM
    pairs = [(i, j) for i in range(nq) for j in range((i * tq + tq - 1) // tk + 1)]
    qi = jnp.asarray([p[0] for p in pairs], jnp.int32)
    kj = jnp.asarray([p[1] for p in pairs], jnp.int32)

    def q_map(b, g, n, qi_ref, kj_ref):
        return (b * nq + qi_ref[n], g)

    def kv_spec(col0):
        return pl.BlockSpec((tk, HEAD_DIM),
                            lambda b, g, n, qi_ref, kj_ref: (b * nk + kj_ref[n], col0 + g))

    grid_spec = pltpu.PrefetchScalarGridSpec(
        num_scalar_prefetch=2,
        grid=(B, G, len(pairs)),
        in_specs=[pl.BlockSpec((tq, gw), q_map), pl.BlockSpec((tq, LANES), q_map),
                  kv_spec(k_blk), kv_spec(v_blk)],
        out_specs=pl.BlockSpec((tq, gw), q_map),
        scratch_shapes=[pltpu.VMEM((NSA_REP * tq, 2 * HEAD_DIM), BF16),
                        pltpu.VMEM((NSA_REP * tq, LANES), F32),
                        pltpu.VMEM((NSA_REP * tq, 2 * HEAD_DIM), F32)])
    return pl.pallas_call(
        functools.partial(_sel_kernel, tq=tq, tk=tk),
        grid_spec=grid_spec,
        out_shape=jax.ShapeDtypeStruct((T, NSA_HEADS * HEAD_DIM), F32),
        compiler_params=_cparams("parallel", "parallel", "arbitrary"),
        name="nsa_selection",
    )(qi, kj, pb, bias, pb, pb)


def _nsa_out_kernel(x_ref, oc_ref, os_ref, ow_ref, z_ref, gl_ref, w_ref, nw_ref, o_ref, hn_ref):
    sig = jax.nn.sigmoid(gl_ref[...])
    H = NSA_HEADS
    acc = x_ref[...]
    for hp in range(H // 2):
        parts = []
        for h in (2 * hp, 2 * hp + 1):
            sl = slice(h * HEAD_DIM, (h + 1) * HEAD_DIM)
            o = (sig[:, h:h + 1] * oc_ref[:, sl] + sig[:, H + h:H + h + 1] * os_ref[:, sl]
                 + sig[:, 2 * H + h:2 * H + h + 1] * ow_ref[:, sl])
            parts.append((o * _silu(z_ref[:, sl])).astype(BF16))
        rows = slice(2 * hp * HEAD_DIM, (2 * hp + 2) * HEAD_DIM)
        acc = acc + _dot(jnp.concatenate(parts, axis=1), w_ref[rows, :])
    o_ref[...] = acc
    hn_ref[...] = _rms_bf16(acc, nw_ref[...])


def _nsa_out(x, o_cmp, o_sel, o_win, pf, z_blk, gl_blk, w_out, layer, next_norm_w):
    T, D = x.shape
    W = o_cmp.shape[1]
    tm = min(T, OUT_TM)
    row = lambda i: (i, 0)
    return pl.pallas_call(
        _nsa_out_kernel,
        grid=(T // tm,),
        in_specs=[pl.BlockSpec((tm, D), row),
                  pl.BlockSpec((tm, W), row), pl.BlockSpec((tm, W), row), pl.BlockSpec((tm, W), row),
                  pl.BlockSpec((tm, W), lambda i: (i, z_blk)),
                  pl.BlockSpec((tm, LANES), lambda i: (i, gl_blk)),
                  pl.BlockSpec((None, W, D), lambda i: (layer, 0, 0)),
                  pl.BlockSpec((1, D), lambda i: (0, 0))],
        out_specs=[pl.BlockSpec((tm, D), row), pl.BlockSpec((tm, D), row)],
        out_shape=[jax.ShapeDtypeStruct((T, D), F32), jax.ShapeDtypeStruct((T, D), BF16)],
        compiler_params=_cparams("parallel"),
        name="nsa_out",
    )(x, o_cmp, o_sel, o_win, pf, pf, w_out, next_norm_w.reshape(1, D))


def _prepare_params(nsa_w_in, nsa_ck_pos, nsa_ck_w1, nsa_ck_w2, nsa_cv_pos, nsa_cv_w1, nsa_cv_w2,
                    nsa_w_out, hgrn_w_in, hgrn_w_out):
    H, G, d = NSA_HEADS, NSA_KV_GROUPS, HEAD_DIM
    kvw = G * d
    c = [0, H * d]
    for _ in range(6):
        c.append(c[-1] + kvw)
    c.append(c[-1] + 3 * H)
    c.append(c[-1] + H * d)
    seg = lambda n: nsa_w_in[:, :, c[n]:c[n + 1]]
    wq, wkc, wvc, wks, wvs, wkw, wvw, wgl, wz = [seg(n) for n in range(9)]
    wq, wkc, wks, wkw = [_permute_head_dims(t, 2) for t in (wq, wkc, wks, wkw)]
    pad = jnp.zeros(nsa_w_in.shape[:2] + (LANES - 3 * H,), nsa_w_in.dtype)
    return dict(
        nsa_wb=jnp.concatenate([wq, wks, wkw, wvs, wvw], axis=2).astype(BF16),
        nsa_wf=jnp.concatenate([wz, wkc, wvc, wgl, pad], axis=2).astype(BF16),
        ck_pos=_permute_head_dims(nsa_ck_pos, 2),
        ck_w1=_permute_head_dims(nsa_ck_w1, 2).astype(BF16),
        ck_w2=_permute_head_dims(nsa_ck_w2, 2).astype(BF16),
        cv_pos=nsa_cv_pos, cv_w1=nsa_cv_w1.astype(BF16), cv_w2=nsa_cv_w2.astype(BF16),
        nsa_wo=nsa_w_out.astype(BF16),
        hgrn_wi=hgrn_w_in, hgrn_wo=hgrn_w_out.astype(BF16))


def _nsa_layer(x, hn, B, S, tabs, prm, j, next_norm_w):
    H, G = NSA_HEADS, NSA_KV_GROUPS
    pb = _proj(hn, prm["nsa_wb"], j, BF16, 1024, ['rope_q'] * H + ['rope'] * (2 * G) + ['plain'] * (2 * G),
               tabs=tabs)
    pf = _proj(hn, prm["nsa_wf"], j, F32, 640, ['plain'] * H + ['rope'] * G + ['plain'] * (G + 1), tabs=tabs)
    ks_blk, kw_blk, vs_blk, vw_blk = H, H + G, H + 2 * G, H + 3 * G
    kc_blk, vc_blk, gl_blk = H, H + G, H + 2 * G

    kT, vcmp = _compress(pf, B, S, kc_blk, vc_blk, j, prm["ck_pos"], prm["ck_w1"], prm["ck_w2"],
                         prm["cv_pos"], prm["cv_w1"], prm["cv_w2"])
    o_cmp, bias = _cmp_topk(pb, kT, vcmp, B, S)
    o_win = _window(pb, B, S, kw_blk, vw_blk)
    o_sel = _selection(pb, bias, B, S, ks_blk, vs_blk)
    return _nsa_out(x, o_cmp, o_sel, o_win, pf, 0, gl_blk, prm["nsa_wo"], j, next_norm_w)


def _hgrn_kernel(q_ref, g_ref, i_ref, z_ref, gw_ref, o_ref, st_ref, *, C, sub, hb):
    n = pl.program_id(2)

    @pl.when(n == 0)
    def _():
        st_ref[...] = jnp.zeros(st_ref.shape, F32)

    r0 = lax.broadcasted_iota(jnp.int32, (C, C), 0)
    r1 = lax.broadcasted_iota(jnp.int32, (C, C), 1)
    tri = (r0 >= r1).astype(BF16)
    dsub = sub // 2
    drow = lax.broadcasted_iota(jnp.int32, (C // dsub, dsub, HEAD_DIM), 1)
    row_id = lax.broadcasted_iota(jnp.int32, (C, HEAD_DIM), 0)

    for h in range(hb):
        cs = slice(h * HEAD_DIM, (h + 1) * HEAD_DIM)
        g = g_ref[:, cs]
        k = 1.0 - jnp.exp(g)
        qs = q_ref[:, cs]
        iv = i_ref[:, cs]
        ib = iv.astype(BF16)

        g1 = g.astype(BF16)
        rem = g - g1.astype(F32)
        g2 = rem.astype(BF16)
        g3 = (rem - g2.astype(F32)).astype(BF16)
        b = _dot(tri, g1) + _dot(tri, g2) + _dot(tri, g3)

        stT = st_ref[h]
        o_inter = _dot_nt((qs * jnp.exp(b)).astype(BF16), stT.astype(BF16))

        nd = C // dsub
        b3 = b.reshape(nd, dsub, HEAD_DIM)
        q3 = qs.reshape(nd, dsub, HEAD_DIM)
        k3 = k.reshape(nd, dsub, HEAD_DIM)
        i3 = iv.reshape(nd, dsub, HEAD_DIM)
        o3 = jnp.zeros((nd, dsub, HEAD_DIM), F32)
        for s in range(dsub):
            w = jnp.where(drow >= s, jnp.exp(b3 - b3[:, s:s + 1, :]), 0.0)
            col = jnp.sum(q3 * w * k3[:, s:s + 1, :], axis=-1, keepdims=True)
            o3 = o3 + col * i3[:, s:s + 1, :]

        att = None
        bsz = 2 * dsub
        while bsz <= C:
            bl = b.reshape(C // bsz, bsz, HEAD_DIM)
            bmid = jnp.broadcast_to(bl[:, bsz // 2 - 1:bsz // 2, :], bl.shape).reshape(C, HEAD_DIM)
            second = (row_id % bsz) >= bsz // 2
            qk = jnp.where(second, qs, k) * jnp.exp(-jnp.abs(b - bmid))
            qh = jnp.where(second, qk, 0.0).astype(BF16)
            kh = jnp.where(second, 0.0, qk).astype(BF16)
            a = _dot_nt(qh, kh)
            if bsz < C:
                a = jnp.where(r0 // bsz == r1 // bsz, a, 0.0)
            att = a if att is None else att + a
            bsz *= 2
        o = o_inter + o3.reshape(C, HEAD_DIM) + _dot(att.astype(BF16), ib)

        b_last = b[C - 1:C, :]
        kd = (k * jnp.exp(b_last - b)).astype(BF16)
        st_ref[h] = stT * jnp.exp(b_last) + _dot(iv.T.astype(BF16), kd)

        o = o * lax.rsqrt(jnp.mean(o * o, axis=-1, keepdims=True) + RMS_EPS) * gw_ref[...]
        o_ref[:, cs] = (o * z_ref[:, cs]).astype(BF16)


def _hgrn_recurrence(ph, gnorm_w, B, S):
    H = HGRN_HEADS
    T = B * S
    C = min(S, HGRN_CHUNK)
    nc = S // C
    hb = HGRN_HEADS_PER_STEP
    hg = H // hb
    w = hb * HEAD_DIM

    def col(c0):
        return pl.BlockSpec((C, w), lambda b, h, n: (b * nc + n, c0 + h))

    return pl.pallas_call(
        functools.partial(_hgrn_kernel, C=C, sub=HGRN_SUB, hb=hb),
        grid=(B, hg, nc),
        in_specs=[col(0), col(hg), col(2 * hg), col(3 * hg),
                  pl.BlockSpec((1, HEAD_DIM), lambda b, h, n: (0, 0))],
        out_specs=pl.BlockSpec((C, w), lambda b, h, n: (b * nc + n, h)),
        out_shape=jax.ShapeDtypeStruct((T, H * HGRN_I_DIM), BF16),
        scratch_shapes=[pltpu.VMEM((hb, HGRN_I_DIM, HGRN_F_DIM), F32)],
        compiler_params=_cparams("parallel", "parallel", "arbitrary"),
        name="hgrn_recurrence",
    )(ph, ph, ph, ph, gnorm_w.reshape(1, HEAD_DIM))


def _hgrn_out_kernel(x_ref, og_ref, w_ref, nw_ref, *out_refs, final_norm):
    y = x_ref[...] + _dot(og_ref[...], w_ref[...])
    if final_norm:
        ms = jnp.mean(y * y, axis=-1, keepdims=True)
        out_refs[0][...] = y * lax.rsqrt(ms + RMS_EPS) * nw_ref[...]
    else:
        out_refs[0][...] = y
        out_refs[1][...] = _rms_bf16(y, nw_ref[...])


def _hgrn_out(x, og, w_out, layer, norm_w, final_norm):
    T, D = x.shape
    W = og.shape[1]
    tm = min(T, 2 * OUT_TM)
    row = lambda i: (i, 0)
    out_specs = [pl.BlockSpec((tm, D), row)]
    out_shape = [jax.ShapeDtypeStruct((T, D), F32)]
    if not final_norm:
        out_specs.append(pl.BlockSpec((tm, D), row))
        out_shape.append(jax.ShapeDtypeStruct((T, D), BF16))
    return pl.pallas_call(
        functools.partial(_hgrn_out_kernel, final_norm=final_norm),
        grid=(T // tm,),
        in_specs=[pl.BlockSpec((tm, D), row), pl.BlockSpec((tm, W), row),
                  pl.BlockSpec((None, W, D), lambda i: (layer, 0, 0)),
                  pl.BlockSpec((1, D), lambda i: (0, 0))],
        out_specs=out_specs,
        out_shape=out_shape,
        compiler_params=_cparams("parallel"),
        name="hgrn_out",
    )(x, og, w_out, norm_w.reshape(1, D))


def _hgrn_layer(x, hn, B, S, prm, lb_logits, gnorm_w, j, norm_w, final_norm):
    H = HGRN_HEADS
    kinds = ['silu'] * H + [('logf', h) for h in range(H)] + ['plain'] * H + ['silu'] * H
    ph = _proj(hn, prm["hgrn_wi"], j, F32, 512, kinds, lb_logits=lb_logits)
    og = _hgrn_recurrence(ph, gnorm_w, B, S)
    return _hgrn_out(x, og, prm["hgrn_wo"], j, norm_w, final_norm)


def kernel(x, positions, norm_w, final_norm_w, nsa_w_in, nsa_ck_pos, nsa_ck_w1, nsa_ck_w2,
           nsa_cv_pos, nsa_cv_w1, nsa_cv_w2, nsa_w_out, hgrn_w_in, hgrn_lb_logits,
           hgrn_gnorm_w, hgrn_w_out):
    B, S, D = x.shape
    depth = norm_w.shape[0]
    assert depth % 2 == 0
    xf = x.reshape(B * S, D)
    tabs = _rope_tables(positions)
    prm = _prepare_params(nsa_w_in, nsa_ck_pos, nsa_ck_w1, nsa_ck_w2, nsa_cv_pos, nsa_cv_w1,
                          nsa_cv_w2, nsa_w_out, hgrn_w_in, hgrn_w_out)
    hn = _rmsnorm(xf, norm_w[0])
    for layer in range(depth):
        j = layer // 2
        last = layer == depth - 1
        if layer % 2 == 0:
            xf, hn = _nsa_layer(xf, hn, B, S, tabs, prm, j, norm_w[layer + 1])
        elif last:
            (xf,) = _hgrn_layer(xf, hn, B, S, prm, hgrn_lb_logits, hgrn_gnorm_w[j], j, final_norm_w, True)
        else:
            xf, hn = _hgrn_layer(xf, hn, B, S, prm, hgrn_lb_logits, hgrn_gnorm_w[j], j,
                                 norm_w[layer + 1], False)
    return xf.reshape(B, S, D)
```

```python
import functools
import math

import jax
import jax.numpy as jnp
from jax import lax
from jax.experimental import pallas as pl
from jax.experimental.pallas import tpu as pltpu

F32 = jnp.float32
BF16 = jnp.bfloat16

RMS_EPS = 1e-6
NEG = -1e30
BELOW_NEG = -3e38
HEAD_DIM = 128
NSA_HEADS = 16
NSA_KV_GROUPS = 4
NSA_REP = NSA_HEADS // NSA_KV_GROUPS
CMP_BLOCK = 32
CMP_STRIDE = 16
SEL_BLOCK = 64
SEL_TOPK = 16
WINDOW = 512
ROPE_THETA = 500000.0
ROPE_DIM = HEAD_DIM // 4
HGRN_HEADS = 16
HGRN_F_DIM = 128
HGRN_I_DIM = 128

LANES = 128
VMEM_LIMIT_BYTES = 56 * 1024 * 1024

ATTN_SCALE = HEAD_DIM ** -0.5
EXP2_SCALE = ATTN_SCALE * math.log2(math.e)

PROJ_TM = 2048
PROJ_CHUNK_HEADS = 2
CMP_TQ = 512
SEL_TQ = 512
SEL_TK = 2048
OUT_TM = 256
HGRN_CHUNK = 128
HGRN_SUB = 16
HGRN_HEADS_PER_STEP = 4


def _cparams(*sem):
    return pltpu.CompilerParams(dimension_semantics=sem, vmem_limit_bytes=VMEM_LIMIT_BYTES)


def _dot(a, b):
    return jnp.dot(a, b, preferred_element_type=F32)


def _dot_nt(a, b):
    return lax.dot_general(a, b, (((1,), (1,)), ((), ())), preferred_element_type=F32)


def _silu(v):
    return v * jax.nn.sigmoid(v)


_HALF = ROPE_DIM // 2
_PARTNER_SHIFT = LANES // 2
ROPE_PERM = (list(range(_HALF))
             + list(range(ROPE_DIM, ROPE_DIM + _PARTNER_SHIFT - _HALF))
             + list(range(_HALF, ROPE_DIM))
             + list(range(ROPE_DIM + _PARTNER_SHIFT - _HALF, HEAD_DIM)))


def _permute_head_dims(w, axis):
    axis = axis % w.ndim
    n = w.shape[axis] // HEAD_DIM
    shape = w.shape[:axis] + (n, HEAD_DIM) + w.shape[axis + 1:]
    wr = w.reshape(shape)
    cuts = [(0, _HALF), (ROPE_DIM, ROPE_DIM + _PARTNER_SHIFT - _HALF), (_HALF, ROPE_DIM),
            (ROPE_DIM + _PARTNER_SHIFT - _HALF, HEAD_DIM)]
    parts = [lax.slice_in_dim(wr, lo, hi, axis=axis + 1) for lo, hi in cuts]
    return jnp.concatenate(parts, axis=axis + 1).reshape(w.shape)


def _rope_table_kernel(pos_ref, inv_ref, cos_ref, sin_ref):
    ang = pos_ref[...] * inv_ref[...]
    lane = lax.broadcasted_iota(jnp.int32, ang.shape, 1)
    s = jnp.sin(ang)
    cos_ref[...] = jnp.cos(ang)
    sin_ref[...] = jnp.where(lane < _HALF, -s, s)


def _rope_tables(positions):
    T = positions.size
    tm = min(T, 2048)
    inv = ROPE_THETA ** (-jnp.arange(_HALF, dtype=F32) / _HALF)
    inv_pat = jnp.concatenate([inv, jnp.zeros((_PARTNER_SHIFT - _HALF,), F32), inv,
                               jnp.zeros((LANES - _PARTNER_SHIFT - _HALF,), F32)]).reshape(1, LANES)
    pos = positions.astype(F32).reshape(T, 1)
    return pl.pallas_call(
        _rope_table_kernel,
        grid=(T // tm,),
        in_specs=[pl.BlockSpec((tm, 1), lambda i: (i, 0)),
                  pl.BlockSpec((1, LANES), lambda i: (0, 0))],
        out_specs=[pl.BlockSpec((tm, LANES), lambda i: (i, 0))] * 2,
        out_shape=[jax.ShapeDtypeStruct((T, LANES), F32)] * 2,
        compiler_params=_cparams("parallel"),
        name="rope_tables",
    )(pos, inv_pat)


def _rms_bf16(y, w):
    ms = jnp.mean(y * y, axis=-1, keepdims=True)
    return (y * lax.rsqrt(ms + RMS_EPS) * w).astype(BF16)


def _rmsnorm_kernel(x_ref, w_ref, o_ref):
    o_ref[...] = _rms_bf16(x_ref[...], w_ref[...])


def _rmsnorm(x, w):
    T, D = x.shape
    tm = min(T, 2 * OUT_TM)
    return pl.pallas_call(
        _rmsnorm_kernel,
        grid=(T // tm,),
        in_specs=[pl.BlockSpec((tm, D), lambda i: (i, 0)), pl.BlockSpec((1, D), lambda i: (0, 0))],
        out_specs=pl.BlockSpec((tm, D), lambda i: (i, 0)),
        out_shape=jax.ShapeDtypeStruct((T, D), BF16),
        compiler_params=_cparams("parallel"),
        name="rmsnorm",
    )(x, w.reshape(1, D))


def _log_forget(t, lbl, layer):
    e = jnp.exp(lbl - jnp.max(lbl, axis=0, keepdims=True))
    p = e / jnp.sum(e, axis=0, keepdims=True)
    lb = jnp.sum(p[0:layer + 1, :], axis=0, keepdims=True) - p[0:1, :]
    tt = jnp.exp(-jnp.abs(t))
    r = 1.0 / (1.0 + tt)
    sig = jnp.where(t >= 0.0, r, tt * r)
    floor = jnp.minimum(t, 0.0) - jnp.log(1.0 + tt) + jnp.log1p(-lb)
    return jnp.maximum(jnp.log(lb + (1.0 - lb) * sig), floor)


def _proj_kernel(*refs, kinds_of_tile, has_rope, layer):
    if has_rope:
        hn_ref, w_ref, cos_ref, sin_ref, o_ref = refs
    elif any(isinstance(k, tuple) for kinds in kinds_of_tile for k in kinds):
        hn_ref, w_ref, lbl_ref, o_ref = refs
    else:
        hn_ref, w_ref, o_ref = refs
    j = pl.program_id(1)
    heads_per_tile = len(kinds_of_tile[0])

    def tile_body(kinds):
        h = 0
        while h < heads_per_tile:
            nh = min(PROJ_CHUNK_HEADS, heads_per_tile - h)
            acc = _dot(hn_ref[...], w_ref[:, h * HEAD_DIM:(h + nh) * HEAD_DIM].astype(BF16))
            for hh in range(nh):
                sl = slice((h + hh) * HEAD_DIM, (h + hh + 1) * HEAD_DIM)
                t = acc[:, hh * HEAD_DIM:(hh + 1) * HEAD_DIM]
                kind = kinds[h + hh]
                if kind in ('rope', 'rope_q'):
                    t = t * cos_ref[...] + pltpu.roll(t, _PARTNER_SHIFT, axis=1) * sin_ref[...]
                if kind == 'rope_q':
                    t = t * EXP2_SCALE
                elif kind == 'silu':
                    t = _silu(t)
                elif isinstance(kind, tuple):
                    t = _log_forget(t, lbl_ref[:, kind[1] * HEAD_DIM:(kind[1] + 1) * HEAD_DIM], layer)
                o_ref[:, sl] = t.astype(o_ref.dtype)
            h += nh

    patterns = {}
    for jj, kinds in enumerate(kinds_of_tile):
        patterns.setdefault(kinds, []).append(jj)
    for kinds, tiles in patterns.items():
        cond = functools.reduce(jnp.logical_or, [j == jj for jj in tiles])
        pl.when(cond)(functools.partial(tile_body, kinds))


def _proj(hn, w, layer, out_dtype, tn, kinds, tabs=None, lb_logits=None):
    T, D = hn.shape
    N = w.shape[2]
    tm = min(T, PROJ_TM)
    hpt = tn // HEAD_DIM
    kinds_of_tile = tuple(tuple(kinds[jj * hpt:(jj + 1) * hpt]) for jj in range(N // tn))
    in_specs = [pl.BlockSpec((tm, D), lambda i, j: (i, 0)),
                pl.BlockSpec((None, D, tn), lambda i, j: (layer, 0, j))]
    args = [hn, w]
    if tabs is not None:
        in_specs += [pl.BlockSpec((tm, LANES), lambda i, j: (i, 0))] * 2
        args += list(tabs)
    if lb_logits is not None:
        in_specs.append(pl.BlockSpec(lb_logits.shape, lambda i, j: (0, 0)))
        args.append(lb_logits)
    return pl.pallas_call(
        functools.partial(_proj_kernel, kinds_of_tile=kinds_of_tile, has_rope=tabs is not None, layer=layer),
        grid=(T // tm, N // tn),
        in_specs=in_specs,
        out_specs=pl.BlockSpec((tm, tn), lambda i, j: (i, j)),
        out_shape=jax.ShapeDtypeStruct((T, N), out_dtype),
        compiler_params=_cparams("parallel", "arbitrary"),
        name="in_proj",
    )(*args)


def _compress_kernel(kc_ref, vc_ref, kpos_ref, kw1_ref, kw2_ref, vpos_ref, vw1_ref, vw2_ref,
                     kT_ref, v_ref, shift_ref, *, ncp):
    half_blk = CMP_BLOCK // 2

    def comp(t_ref, pos_ref, w1_ref, w2_ref):
        a = jnp.zeros((ncp, HEAD_DIM), F32)
        bm = jnp.zeros((ncp, HEAD_DIM), F32)
        for l in range(half_blk):
            xl = t_ref[pl.ds(l, ncp, stride=CMP_STRIDE), :]
            a = a + _dot((xl + pos_ref[l:l + 1, :]).astype(BF16), w1_ref[l])
            bm = bm + _dot((xl + pos_ref[half_blk + l:half_blk + l + 1, :]).astype(BF16),
                           w1_ref[half_blk + l])
        shift_ref[pl.ds(0, ncp), :] = bm
        shift_ref[pl.ds(ncp, 8), :] = jnp.zeros((8, HEAD_DIM), F32)
        hid = _silu(a + shift_ref[pl.ds(1, ncp), :])
        out = _dot(hid.astype(BF16), w2_ref[...])
        row = lax.broadcasted_iota(jnp.int32, out.shape, 0)
        return jnp.where(row < ncp - 1, out, 0.0)

    kT_ref[0, 0] = comp(kc_ref, kpos_ref, kw1_ref, kw2_ref).T.astype(BF16)
    v_ref[0, 0] = comp(vc_ref, vpos_ref, vw1_ref, vw2_ref).astype(BF16)


def _compress(pf, B, S, kc_blk, vc_blk, layer, ck_pos, ck_w1, ck_w2, cv_pos, cv_w1, cv_w2):
    G = NSA_KV_GROUPS
    ncp = S // CMP_STRIDE
    full2 = lambda b, g: (layer, 0, 0)
    full3 = lambda b, g: (layer, 0, 0, 0)
    return pl.pallas_call(
        functools.partial(_compress_kernel, ncp=ncp),
        grid=(B, G),
        in_specs=[pl.BlockSpec((S, HEAD_DIM), lambda b, g: (b, kc_blk + g)),
                  pl.BlockSpec((S, HEAD_DIM), lambda b, g: (b, vc_blk + g)),
                  pl.BlockSpec((None, CMP_BLOCK, HEAD_DIM), full2),
                  pl.BlockSpec((None, CMP_BLOCK, HEAD_DIM, HEAD_DIM), full3),
                  pl.BlockSpec((None, HEAD_DIM, HEAD_DIM), full2),
                  pl.BlockSpec((None, CMP_BLOCK, HEAD_DIM), full2),
                  pl.BlockSpec((None, CMP_BLOCK, HEAD_DIM, HEAD_DIM), full3),
                  pl.BlockSpec((None, HEAD_DIM, HEAD_DIM), full2)],
        out_specs=[pl.BlockSpec((1, 1, HEAD_DIM, ncp), lambda b, g: (b, g, 0, 0)),
                   pl.BlockSpec((1, 1, ncp, HEAD_DIM), lambda b, g: (b, g, 0, 0))],
        out_shape=[jax.ShapeDtypeStruct((B, G, HEAD_DIM, ncp), BF16),
                   jax.ShapeDtypeStruct((B, G, ncp, HEAD_DIM), BF16)],
        scratch_shapes=[pltpu.VMEM((ncp + 8, HEAD_DIM), F32)],
        compiler_params=_cparams("parallel", "parallel"),
        name="nsa_compress",
    )(pf, pf, ck_pos, ck_w1, ck_w2, cv_pos, cv_w1, cv_w2)


def _cmp_topk_kernel(q_ref, kT_ref, v_ref, o_ref, bias_ref, imp_ref, *, tq, ncp):
    i = pl.program_id(2)
    s0 = i * tq
    t_col = s0 + lax.broadcasted_iota(jnp.int32, (tq, 1), 0)
    any_visible = (t_col >= CMP_BLOCK - 1).astype(F32)

    def attend(nb):
        nc = nb * LANES
        n_row = lax.broadcasted_iota(jnp.int32, (1, nc), 1)
        mc = jnp.logical_and(n_row * CMP_STRIDE + (CMP_BLOCK - 1) <= t_col, n_row < ncp - 1)
        kT = kT_ref[0, 0, :, 0:nc]
        v = v_ref[0, 0, 0:nc, :]
        psum = jnp.zeros((tq, nc), F32)
        for r in range(NSA_REP):
            sl = slice(r * HEAD_DIM, (r + 1) * HEAD_DIM)
            s = jnp.where(mc, _dot(q_ref[:, sl], kT), NEG)
            m = jnp.max(s, axis=-1, keepdims=True)
            e = jnp.exp2(s - m)
            l = jnp.sum(e, axis=-1, keepdims=True)
            pc = e * (any_visible / l)
            o_ref[:, sl] = _dot(pc.astype(BF16), v)
            psum = psum + pc
        n_col = lax.broadcasted_iota(jnp.int32, (nc, LANES), 0) * CMP_STRIDE
        k_row = lax.broadcasted_iota(jnp.int32, (nc, LANES), 1) * SEL_BLOCK
        ovl = jnp.logical_and(n_col < k_row + SEL_BLOCK, n_col + CMP_BLOCK > k_row).astype(BF16)
        hi = psum.astype(BF16)
        lo = (psum - hi.astype(F32)).astype(BF16)
        imp_ref[...] = _dot(hi, ovl) + _dot(lo, ovl)

    nb_total = ncp // LANES
    last_visible = (s0 + tq - CMP_BLOCK) // CMP_STRIDE
    nb_needed = jnp.minimum(last_visible // LANES + 1, nb_total)
    for nb in range(1, nb_total + 1):
        pl.when(nb_needed == nb)(functools.partial(attend, nb))

    blk = lax.broadcasted_iota(jnp.int32, (LANES, LANES), 0)
    lane = lax.broadcasted_iota(jnp.int32, (LANES, LANES), 1)
    groups = []
    for c in range(0, tq, LANES):
        impT = imp_ref[c:c + LANES, :].T
        cur = (s0 + c + lane) // SEL_BLOCK
        forced = jnp.logical_or(blk == 0, jnp.logical_or(blk == cur, blk == cur - 1))
        groups.append(jnp.where(forced, BELOW_NEG, jnp.where(blk <= cur, impT, NEG)))
    for _ in range(SEL_TOPK - 3):
        nxt = []
        for sc in groups:
            m = jnp.max(sc, axis=0, keepdims=True)
            idx = jnp.min(jnp.where(sc == m, blk, LANES), axis=0, keepdims=True)
            nxt.append(jnp.where(blk == idx, BELOW_NEG, sc))
        groups = nxt
    for g, sc in enumerate(groups):
        biasT = jnp.where(sc == BELOW_NEG, 0.0, NEG)
        bias_ref[g * LANES:(g + 1) * LANES, :] = biasT.T.astype(BF16)


def _cmp_topk(pb, kT, vcmp, B, S):
    G = NSA_KV_GROUPS
    T = B * S
    tq = min(S, CMP_TQ)
    nq = S // tq
    ncp = S // CMP_STRIDE
    gw = NSA_REP * HEAD_DIM
    return pl.pallas_call(
        functools.partial(_cmp_topk_kernel, tq=tq, ncp=ncp),
        grid=(B, G, nq),
        in_specs=[pl.BlockSpec((tq, gw), lambda b, g, i: (b * nq + i, g)),
                  pl.BlockSpec((1, 1, HEAD_DIM, ncp), lambda b, g, i: (b, g, 0, 0)),
                  pl.BlockSpec((1, 1, ncp, HEAD_DIM), lambda b, g, i: (b, g, 0, 0))],
        out_specs=[pl.BlockSpec((tq, gw), lambda b, g, i: (b * nq + i, g)),
                   pl.BlockSpec((tq, LANES), lambda b, g, i: (b * nq + i, g))],
        out_shape=[jax.ShapeDtypeStruct((T, NSA_HEADS * HEAD_DIM), F32),
                   jax.ShapeDtypeStruct((T, G * LANES), BF16)],
        scratch_shapes=[pltpu.VMEM((tq, LANES), F32)],
        compiler_params=_cparams("parallel", "parallel", "parallel"),
        name="nsa_cmp_topk",
    )(pb, kT, vcmp)


def _win_kernel(q_ref, kp_ref, kc_ref, vp_ref, vc_ref, o_ref, *, sub):
    i = pl.program_id(2)
    row = lax.broadcasted_iota(jnp.int32, (sub, sub), 0)
    col = lax.broadcasted_iota(jnp.int32, (sub, sub), 1)
    ones = jnp.ones((sub, LANES), BF16)

    def body(first):
        ks = [kp_ref[0:sub, :], kp_ref[sub:2 * sub, :], kc_ref[0:sub, :], kc_ref[sub:2 * sub, :]]
        vs = [jnp.concatenate([v, ones], axis=1) for v in
              (vp_ref[0:sub, :], vp_ref[sub:2 * sub, :], vc_ref[0:sub, :], vc_ref[sub:2 * sub, :])]
        for a in range(2):
            blocks = [(a, col > row), (a + 1, None), (a + 2, col <= row)]
            if first:
                blocks = blocks[2 - a:]
            rows = slice(a * sub, (a + 1) * sub)
            for r in range(NSA_REP):
                sl = slice(r * HEAD_DIM, (r + 1) * HEAD_DIM)
                q = q_ref[rows, sl]
                ss = []
                for kb, mask in blocks:
                    s = _dot_nt(q, ks[kb])
                    ss.append(s if mask is None else jnp.where(mask, s, NEG))
                m = functools.reduce(jnp.maximum, ss)
                m = jnp.max(m, axis=-1, keepdims=True)
                acc = None
                for s, (kb, _) in zip(ss, blocks):
                    part = _dot(jnp.exp2(s - m).astype(BF16), vs[kb])
                    acc = part if acc is None else acc + part
                o_ref[rows, sl] = acc[:, 0:HEAD_DIM] / acc[:, HEAD_DIM:2 * HEAD_DIM]

    pl.when(i > 0)(functools.partial(body, False))
    pl.when(i == 0)(functools.partial(body, True))


def _window(pb, B, S, k_blk, v_blk):
    G = NSA_KV_GROUPS
    T = B * S
    sub = WINDOW // 2
    tq = 2 * sub
    nq = S // tq
    gw = NSA_REP * HEAD_DIM

    def kv_spec(col0, back):
        return pl.BlockSpec((tq, HEAD_DIM),
                            lambda b, g, i: (b * nq + jnp.maximum(i - back, 0), col0 + g))

    return pl.pallas_call(
        functools.partial(_win_kernel, sub=sub),
        grid=(B, G, nq),
        in_specs=[pl.BlockSpec((tq, gw), lambda b, g, i: (b * nq + i, g)),
                  kv_spec(k_blk, 1), kv_spec(k_blk, 0), kv_spec(v_blk, 1), kv_spec(v_blk, 0)],
        out_specs=pl.BlockSpec((tq, gw), lambda b, g, i: (b * nq + i, g)),
        out_shape=jax.ShapeDtypeStruct((T, NSA_HEADS * HEAD_DIM), F32),
        compiler_params=_cparams("parallel", "parallel", "parallel"),
        name="nsa_window",
    )(pb, pb, pb, pb, pb)


def _sel_kernel(qi_ref, kj_ref, q_ref, bias_ref, k_ref, v_ref, o_ref, qa_ref, m_ref, acc_ref, *, tq, tk):
    n = pl.program_id(2)
    i = qi_ref[n]
    j = kj_ref[n]
    last_j = (i * tq + tq - 1) // tk

    @pl.when(j == 0)
    def _():
        for r in range(NSA_REP):
            rows = slice(r * tq, (r + 1) * tq)
            qa_ref[rows, 0:HEAD_DIM] = q_ref[:, r * HEAD_DIM:(r + 1) * HEAD_DIM]
            qa_ref[rows, HEAD_DIM:2 * HEAD_DIM] = bias_ref[...]
        m_ref[...] = jnp.full(m_ref.shape, NEG, F32)
        acc_ref[...] = jnp.zeros(acc_ref.shape, F32)

    def step(col0, ncols, diagonal):
        key = j * tk + col0 + lax.broadcasted_iota(jnp.int32, (ncols, LANES), 0)
        blk = lax.broadcasted_iota(jnp.int32, (ncols, LANES), 1)
        onehot = (key // SEL_BLOCK == blk).astype(BF16)
        ka = jnp.concatenate([k_ref[col0:col0 + ncols, :], onehot], axis=1)
        va = jnp.concatenate([v_ref[col0:col0 + ncols, :], jnp.ones((ncols, LANES), BF16)], axis=1)
        if diagonal:
            visible = (lax.broadcasted_iota(jnp.int32, (tq, tq), 1)
                       <= lax.broadcasted_iota(jnp.int32, (tq, tq), 0))
        def scores(r):
            s = _dot_nt(qa_ref[r * tq:(r + 1) * tq, :], ka)
            return jnp.where(visible, s, NEG) if diagonal else s

        s_next = scores(0)
        for r in range(NSA_REP):
            rows = slice(r * tq, (r + 1) * tq)
            s = s_next
            if r + 1 < NSA_REP:
                s_next = scores(r + 1)
            m_prev = m_ref[rows, :]
            m_new = jnp.maximum(m_prev, jnp.max(s, axis=-1, keepdims=True))
            alpha = jnp.exp2(m_prev - m_new)
            p = jnp.exp2(s - jnp.tile(m_new, (1, ncols // LANES)))
            acc_ref[rows, :] = jnp.tile(alpha, (1, 2)) * acc_ref[rows, :] + _dot(p.astype(BF16), va)
            m_ref[rows, :] = m_new

    @pl.when(j < last_j)
    def _():
        step(0, tk, False)

    for sub in range(tk // tq):
        @pl.when(jnp.logical_and(j == last_j, i % (tk // tq) == sub))
        def _(sub=sub):
            if sub > 0:
                step(0, sub * tq, False)
            step(sub * tq, tq, True)

    @pl.when(j == last_j)
    def _():
        for r in range(NSA_REP):
            rows = slice(r * tq, (r + 1) * tq)
            o_ref[:, r * HEAD_DIM:(r + 1) * HEAD_DIM] = (acc_ref[rows, 0:HEAD_DIM]
                                                         / acc_ref[rows, HEAD_DIM:2 * HEAD_DIM])


def _selection(pb, bias, B, S, k_blk, v_blk):
    G = NSA_KV_GROUPS
    T = B * S
    tq = min(S, SEL_TQ)
    tk = min(S, SEL_TK)
    assert tk % tq == 0
    nq = S // tq
    nk = S // tk
    gw = NSA_REP * HEAD_DIM
    pairs = [(i, j) for i in range(nq) for j in range((i * tq + tq - 1) // tk + 1)]
    qi = jnp.asarray([p[0] for p in pairs], jnp.int32)
    kj = jnp.asarray([p[1] for p in pairs], jnp.int32)

    def q_map(b, g, n, qi_ref, kj_ref):
        return (b * nq + qi_ref[n], g)

    def kv_spec(col0):
        return pl.BlockSpec((tk, HEAD_DIM),
                            lambda b, g, n, qi_ref, kj_ref: (b * nk + kj_ref[n], col0 + g))

    grid_spec = pltpu.PrefetchScalarGridSpec(
        num_scalar_prefetch=2,
        grid=(B, G, len(pairs)),
        in_specs=[pl.BlockSpec((tq, gw), q_map), pl.BlockSpec((tq, LANES), q_map),
                  kv_spec(k_blk), kv_spec(v_blk)],
        out_specs=pl.BlockSpec((tq, gw), q_map),
        scratch_shapes=[pltpu.VMEM((NSA_REP * tq, 2 * HEAD_DIM), BF16),
                        pltpu.VMEM((NSA_REP * tq, LANES), F32),
                        pltpu.VMEM((NSA_REP * tq, 2 * HEAD_DIM), F32)])
    return pl.pallas_call(
        functools.partial(_sel_kernel, tq=tq, tk=tk),
        grid_spec=grid_spec,
        out_shape=jax.ShapeDtypeStruct((T, NSA_HEADS * HEAD_DIM), F32),
        compiler_params=_cparams("parallel", "parallel", "arbitrary"),
        name="nsa_selection",
    )(qi, kj, pb, bias, pb, pb)


def _nsa_out_kernel(x_ref, oc_ref, os_ref, ow_ref, z_ref, gl_ref, w_ref, nw_ref, o_ref, hn_ref):
    sig = jax.nn.sigmoid(gl_ref[...])
    H = NSA_HEADS
    acc = x_ref[...]
    for hp in range(H // 2):
        parts = []
        for h in (2 * hp, 2 * hp + 1):
            sl = slice(h * HEAD_DIM, (h + 1) * HEAD_DIM)
            o = (sig[:, h:h + 1] * oc_ref[:, sl] + sig[:, H + h:H + h + 1] * os_ref[:, sl]
                 + sig[:, 2 * H + h:2 * H + h + 1] * ow_ref[:, sl])
            parts.append((o * _silu(z_ref[:, sl])).astype(BF16))
        rows = slice(2 * hp * HEAD_DIM, (2 * hp + 2) * HEAD_DIM)
        acc = acc + _dot(jnp.concatenate(parts, axis=1), w_ref[rows, :])
    o_ref[...] = acc
    hn_ref[...] = _rms_bf16(acc, nw_ref[...])


def _nsa_out(x, o_cmp, o_sel, o_win, pf, z_blk, gl_blk, w_out, layer, next_norm_w):
    T, D = x.shape
    W = o_cmp.shape[1]
    tm = min(T, OUT_TM)
    row = lambda i: (i, 0)
    return pl.pallas_call(
        _nsa_out_kernel,
        grid=(T // tm,),
        in_specs=[pl.BlockSpec((tm, D), row),
                  pl.BlockSpec((tm, W), row), pl.BlockSpec((tm, W), row), pl.BlockSpec((tm, W), row),
                  pl.BlockSpec((tm, W), lambda i: (i, z_blk)),
                  pl.BlockSpec((tm, LANES), lambda i: (i, gl_blk)),
                  pl.BlockSpec((None, W, D), lambda i: (layer, 0, 0)),
                  pl.BlockSpec((1, D), lambda i: (0, 0))],
        out_specs=[pl.BlockSpec((tm, D), row), pl.BlockSpec((tm, D), row)],
        out_shape=[jax.ShapeDtypeStruct((T, D), F32), jax.ShapeDtypeStruct((T, D), BF16)],
        compiler_params=_cparams("parallel"),
        name="nsa_out",
    )(x, o_cmp, o_sel, o_win, pf, pf, w_out, next_norm_w.reshape(1, D))


def _prepare_params(nsa_w_in, nsa_ck_pos, nsa_ck_w1, nsa_ck_w2, nsa_cv_pos, nsa_cv_w1, nsa_cv_w2,
                    nsa_w_out, hgrn_w_in, hgrn_w_out):
    H, G, d = NSA_HEADS, NSA_KV_GROUPS, HEAD_DIM
    kvw = G * d
    c = [0, H * d]
    for _ in range(6):
        c.append(c[-1] + kvw)
    c.append(c[-1] + 3 * H)
    c.append(c[-1] + H * d)
    seg = lambda n: nsa_w_in[:, :, c[n]:c[n + 1]]
    wq, wkc, wvc, wks, wvs, wkw, wvw, wgl, wz = [seg(n) for n in range(9)]
    wq, wkc, wks, wkw = [_permute_head_dims(t, 2) for t in (wq, wkc, wks, wkw)]
    pad = jnp.zeros(nsa_w_in.shape[:2] + (LANES - 3 * H,), nsa_w_in.dtype)
    return dict(
        nsa_wb=jnp.concatenate([wq, wks, wkw, wvs, wvw], axis=2).astype(BF16),
        nsa_wf=jnp.concatenate([wz, wkc, wvc, wgl, pad], axis=2).astype(BF16),
        ck_pos=_permute_head_dims(nsa_ck_pos, 2),
        ck_w1=_permute_head_dims(nsa_ck_w1, 2).astype(BF16),
        ck_w2=_permute_head_dims(nsa_ck_w2, 2).astype(BF16),
        cv_pos=nsa_cv_pos, cv_w1=nsa_cv_w1.astype(BF16), cv_w2=nsa_cv_w2.astype(BF16),
        nsa_wo=nsa_w_out.astype(BF16),
        hgrn_wi=hgrn_w_in, hgrn_wo=hgrn_w_out.astype(BF16))


def _nsa_layer(x, hn, B, S, tabs, prm, j, next_norm_w):
    H, G = NSA_HEADS, NSA_KV_GROUPS
    pb = _proj(hn, prm["nsa_wb"], j, BF16, 1024, ['rope_q'] * H + ['rope'] * (2 * G) + ['plain'] * (2 * G),
               tabs=tabs)
    pf = _proj(hn, prm["nsa_wf"], j, F32, 640, ['plain'] * H + ['rope'] * G + ['plain'] * (G + 1), tabs=tabs)
    ks_blk, kw_blk, vs_blk, vw_blk = H, H + G, H + 2 * G, H + 3 * G
    kc_blk, vc_blk, gl_blk = H, H + G, H + 2 * G

    kT, vcmp = _compress(pf, B, S, kc_blk, vc_blk, j, prm["ck_pos"], prm["ck_w1"], prm["ck_w2"],
                         prm["cv_pos"], prm["cv_w1"], prm["cv_w2"])
    o_cmp, bias = _cmp_topk(pb, kT, vcmp, B, S)
    o_win = _window(pb, B, S, kw_blk, vw_blk)
    o_sel = _selection(pb, bias, B, S, ks_blk, vs_blk)
    return _nsa_out(x, o_cmp, o_sel, o_win, pf, 0, gl_blk, prm["nsa_wo"], j, next_norm_w)


def _hgrn_kernel(q_ref, g_ref, i_ref, z_ref, gw_ref, o_ref, st_ref, *, C, sub, hb):
    n = pl.program_id(2)

    @pl.when(n == 0)
    def _():
        st_ref[...] = jnp.zeros(st_ref.shape, F32)

    r0 = lax.broadcasted_iota(jnp.int32, (C, C), 0)
    r1 = lax.broadcasted_iota(jnp.int32, (C, C), 1)
    tri = (r0 >= r1).astype(BF16)
    dsub = sub // 2
    drow = lax.broadcasted_iota(jnp.int32, (C // dsub, dsub, HEAD_DIM), 1)
    row_id = lax.broadcasted_iota(jnp.int32, (C, HEAD_DIM), 0)

    for h in range(hb):
        cs = slice(h * HEAD_DIM, (h + 1) * HEAD_DIM)
        g = g_ref[:, cs]
        k = 1.0 - jnp.exp(g)
        qs = q_ref[:, cs]
        iv = i_ref[:, cs]
        ib = iv.astype(BF16)

        g1 = g.astype(BF16)
        rem = g - g1.astype(F32)
        g2 = rem.astype(BF16)
        g3 = (rem - g2.astype(F32)).astype(BF16)
        b = _dot(tri, g1) + _dot(tri, g2) + _dot(tri, g3)

        stT = st_ref[h]
        o_inter = _dot_nt((qs * jnp.exp(b)).astype(BF16), stT.astype(BF16))

        nd = C // dsub
        b3 = b.reshape(nd, dsub, HEAD_DIM)
        q3 = qs.reshape(nd, dsub, HEAD_DIM)
        k3 = k.reshape(nd, dsub, HEAD_DIM)
        i3 = iv.reshape(nd, dsub, HEAD_DIM)
        o3 = jnp.zeros((nd, dsub, HEAD_DIM), F32)
        for s in range(dsub):
            w = jnp.where(drow >= s, jnp.exp(b3 - b3[:, s:s + 1, :]), 0.0)
            col = jnp.sum(q3 * w * k3[:, s:s + 1, :], axis=-1, keepdims=True)
            o3 = o3 + col * i3[:, s:s + 1, :]

        att = None
        bsz = 2 * dsub
        while bsz <= C:
            bl = b.reshape(C // bsz, bsz, HEAD_DIM)
            bmid = jnp.broadcast_to(bl[:, bsz // 2 - 1:bsz // 2, :], bl.shape).reshape(C, HEAD_DIM)
            second = (row_id % bsz) >= bsz // 2
            qk = jnp.where(second, qs, k) * jnp.exp(-jnp.abs(b - bmid))
            qh = jnp.where(second, qk, 0.0).astype(BF16)
            kh = jnp.where(second, 0.0, qk).astype(BF16)
            a = _dot_nt(qh, kh)
            if bsz < C:
                a = jnp.where(r0 // bsz == r1 // bsz, a, 0.0)
            att = a if att is None else att + a
            bsz *= 2
        o = o_inter + o3.reshape(C, HEAD_DIM) + _dot(att.astype(BF16), ib)

        b_last = b[C - 1:C, :]
        kd = (k * jnp.exp(b_last - b)).astype(BF16)
        st_ref[h] = stT * jnp.exp(b_last) + _dot(iv.T.astype(BF16), kd)

        o = o * lax.rsqrt(jnp.mean(o * o, axis=-1, keepdims=True) + RMS_EPS) * gw_ref[...]
        o_ref[:, cs] = (o * z_ref[:, cs]).astype(BF16)


def _hgrn_recurrence(ph, gnorm_w, B, S):
    H = HGRN_HEADS
    T = B * S
    C = min(S, HGRN_CHUNK)
    nc = S // C
    hb = HGRN_HEADS_PER_STEP
    hg = H // hb
    w = hb * HEAD_DIM

    def col(c0):
        return pl.BlockSpec((C, w), lambda b, h, n: (b * nc + n, c0 + h))

    return pl.pallas_call(
        functools.partial(_hgrn_kernel, C=C, sub=HGRN_SUB, hb=hb),
        grid=(B, hg, nc),
        in_specs=[col(0), col(hg), col(2 * hg), col(3 * hg),
                  pl.BlockSpec((1, HEAD_DIM), lambda b, h, n: (0, 0))],
        out_specs=pl.BlockSpec((C, w), lambda b, h, n: (b * nc + n, h)),
        out_shape=jax.ShapeDtypeStruct((T, H * HGRN_I_DIM), BF16),
        scratch_shapes=[pltpu.VMEM((hb, HGRN_I_DIM, HGRN_F_DIM), F32)],
        compiler_params=_cparams("parallel", "parallel", "arbitrary"),
        name="hgrn_recurrence",
    )(ph, ph, ph, ph, gnorm_w.reshape(1, HEAD_DIM))


def _hgrn_out_kernel(x_ref, og_ref, w_ref, nw_ref, *out_refs, final_norm):
    y = x_ref[...] + _dot(og_ref[...], w_ref[...])
    if final_norm:
        ms = jnp.mean(y * y, axis=-1, keepdims=True)
        out_refs[0][...] = y * lax.rsqrt(ms + RMS_EPS) * nw_ref[...]
    else:
        out_refs[0][...] = y
        out_refs[1][...] = _rms_bf16(y, nw_ref[...])


def _hgrn_out(x, og, w_out, layer, norm_w, final_norm):
    T, D = x.shape
    W = og.shape[1]
    tm = min(T, 2 * OUT_TM)
    row = lambda i: (i, 0)
    out_specs = [pl.BlockSpec((tm, D), row)]
    out_shape = [jax.ShapeDtypeStruct((T, D), F32)]
    if not final_norm:
        out_specs.append(pl.BlockSpec((tm, D), row))
        out_shape.append(jax.ShapeDtypeStruct((T, D), BF16))
    return pl.pallas_call(
        functools.partial(_hgrn_out_kernel, final_norm=final_norm),
        grid=(T // tm,),
        in_specs=[pl.BlockSpec((tm, D), row), pl.BlockSpec((tm, W), row),
                  pl.BlockSpec((None, W, D), lambda i: (layer, 0, 0)),
                  pl.BlockSpec((1, D), lambda i: (0, 0))],
        out_specs=out_specs,
        out_shape=out_shape,
        compiler_params=_cparams("parallel"),
        name="hgrn_out",
    )(x, og, w_out, norm_w.reshape(1, D))


def _hgrn_layer(x, hn, B, S, prm, lb_logits, gnorm_w, j, norm_w, final_norm):
    H = HGRN_HEADS
    kinds = ['silu'] * H + [('logf', h) for h in range(H)] + ['plain'] * H + ['silu'] * H
    ph = _proj(hn, prm["hgrn_wi"], j, F32, 512, kinds, lb_logits=lb_logits)
    og = _hgrn_recurrence(ph, gnorm_w, B, S)
    return _hgrn_out(x, og, prm["hgrn_wo"], j, norm_w, final_norm)


def kernel(x, positions, norm_w, final_norm_w, nsa_w_in, nsa_ck_pos, nsa_ck_w1, nsa_ck_w2,
           nsa_cv_pos, nsa_cv_w1, nsa_cv_w2, nsa_w_out, hgrn_w_in, hgrn_lb_logits,
           hgrn_gnorm_w, hgrn_w_out):
    B, S, D = x.shape
    depth = norm_w.shape[0]
    assert depth % 2 == 0
    xf = x.reshape(B * S, D)
    tabs = _rope_tables(positions)
    prm = _prepare_params(nsa_w_in, nsa_ck_pos, nsa_ck_w1, nsa_ck_w2, nsa_cv_pos, nsa_cv_w1,
                          nsa_cv_w2, nsa_w_out, hgrn_w_in, hgrn_w_out)
    hn = _rmsnorm(xf, norm_w[0])
    for layer in range(depth):
        j = layer // 2
        last = layer == depth - 1
        if layer % 2 == 0:
            xf, hn = _nsa_layer(xf, hn, B, S, tabs, prm, j, norm_w[layer + 1])
        elif last:
            (xf,) = _hgrn_layer(xf, hn, B, S, prm, hgrn_lb_logits, hgrn_gnorm_w[j], j, final_norm_w, True)
        else:
            xf, hn = _hgrn_layer(xf, hn, B, S, prm, hgrn_lb_logits, hgrn_gnorm_w[j], j,
                                 norm_w[layer + 1], False)
    return xf.reshape(B, S, D)
```

```python
import functools
import math

import jax
import jax.numpy as jnp
from jax import lax
from jax.experimental import pallas as pl
from jax.experimental.pallas import tpu as pltpu

F32 = jnp.float32
BF16 = jnp.bfloat16

RMS_EPS = 1e-6
NEG = -1e30
BELOW_NEG = -3e38
HEAD_DIM = 128
NSA_HEADS = 16
NSA_KV_GROUPS = 4
NSA_REP = NSA_HEADS // NSA_KV_GROUPS
CMP_BLOCK = 32
CMP_STRIDE = 16
SEL_BLOCK = 64
SEL_TOPK = 16
WINDOW = 512
ROPE_THETA = 500000.0
ROPE_DIM = HEAD_DIM // 4
HGRN_HEADS = 16
HGRN_F_DIM = 128
HGRN_I_DIM = 128

LANES = 128
VMEM_LIMIT_BYTES = 56 * 1024 * 1024

ATTN_SCALE = HEAD_DIM ** -0.5
EXP2_SCALE = ATTN_SCALE * math.log2(math.e)

PROJ_TM = 2048
PROJ_CHUNK_HEADS = 2
CMP_TQ = 512
SEL_TQ = 512
SEL_TK = 2048
OUT_TM = 256
HGRN_CHUNK = 128
HGRN_SUB = 16
HGRN_HEADS_PER_STEP = 4


def _cparams(*sem):
    return pltpu.CompilerParams(dimension_semantics=sem, vmem_limit_bytes=VMEM_LIMIT_BYTES)


def _dot(a, b):
    return jnp.dot(a, b, preferred_element_type=F32)


def _dot_nt(a, b):
    return lax.dot_general(a, b, (((1,), (1,)), ((), ())), preferred_element_type=F32)


def _silu(v):
    return v * jax.nn.sigmoid(v)


_HALF = ROPE_DIM // 2


def _rope_table_kernel(pos_ref, inv_ref, cos_ref, sin_lo_ref, sin_hi_ref):
    ang = pos_ref[...] * inv_ref[...]
    lane = lax.broadcasted_iota(jnp.int32, ang.shape, 1)
    s = jnp.sin(ang)
    cos_ref[...] = jnp.cos(ang)
    sin_lo_ref[...] = jnp.where(lane < _HALF, -s, 0.0)
    sin_hi_ref[...] = jnp.where(lane < _HALF, 0.0, s)


def _rope_tables(positions):
    T = positions.size
    tm = min(T, PROJ_TM)
    inv = ROPE_THETA ** (-jnp.arange(_HALF, dtype=F32) / _HALF)
    inv_pat = jnp.concatenate([inv, inv, jnp.zeros((LANES - ROPE_DIM,), F32)]).reshape(1, LANES)
    pos = positions.astype(F32).reshape(T, 1)
    return pl.pallas_call(
        _rope_table_kernel,
        grid=(T // tm,),
        in_specs=[pl.BlockSpec((tm, 1), lambda i: (i, 0)),
                  pl.BlockSpec((1, LANES), lambda i: (0, 0))],
        out_specs=[pl.BlockSpec((tm, LANES), lambda i: (i, 0))] * 3,
        out_shape=[jax.ShapeDtypeStruct((T, LANES), F32)] * 3,
        compiler_params=_cparams("parallel"),
        name="rope_tables",
    )(pos, inv_pat)


def _rms_bf16(y, w):
    ms = jnp.mean(y * y, axis=-1, keepdims=True)
    return (y * lax.rsqrt(ms + RMS_EPS) * w).astype(BF16)


def _rmsnorm_kernel(x_ref, w_ref, o_ref):
    o_ref[...] = _rms_bf16(x_ref[...], w_ref[...])


def _rmsnorm(x, w):
    T, D = x.shape
    tm = min(T, 2 * OUT_TM)
    return pl.pallas_call(
        _rmsnorm_kernel,
        grid=(T // tm,),
        in_specs=[pl.BlockSpec((tm, D), lambda i: (i, 0)), pl.BlockSpec((1, D), lambda i: (0, 0))],
        out_specs=pl.BlockSpec((tm, D), lambda i: (i, 0)),
        out_shape=jax.ShapeDtypeStruct((T, D), BF16),
        compiler_params=_cparams("parallel"),
        name="rmsnorm",
    )(x, w.reshape(1, D))


def _log_forget(t, lbl, layer):
    e = jnp.exp(lbl - jnp.max(lbl, axis=0, keepdims=True))
    p = e / jnp.sum(e, axis=0, keepdims=True)
    lb = jnp.sum(p[0:layer + 1, :], axis=0, keepdims=True) - p[0:1, :]
    tt = jnp.exp(-jnp.abs(t))
    r = 1.0 / (1.0 + tt)
    sig = jnp.where(t >= 0.0, r, tt * r)
    floor = jnp.minimum(t, 0.0) - jnp.log(1.0 + tt) + jnp.log1p(-lb)
    return jnp.maximum(jnp.log(lb + (1.0 - lb) * sig), floor)


def _proj_kernel(*refs, kinds_of_tile, has_rope, layer):
    if has_rope:
        hn_ref, w_ref, cos_ref, sin_lo_ref, sin_hi_ref, o_ref = refs
    elif any(isinstance(k, tuple) for kinds in kinds_of_tile for k in kinds):
        hn_ref, w_ref, lbl_ref, o_ref = refs
    else:
        hn_ref, w_ref, o_ref = refs
    j = pl.program_id(1)
    heads_per_tile = len(kinds_of_tile[0])

    def tile_body(kinds):
        h = 0
        while h < heads_per_tile:
            nh = min(PROJ_CHUNK_HEADS, heads_per_tile - h)
            acc = _dot(hn_ref[...], w_ref[:, h * HEAD_DIM:(h + nh) * HEAD_DIM].astype(BF16))
            for hh in range(nh):
                sl = slice((h + hh) * HEAD_DIM, (h + hh + 1) * HEAD_DIM)
                t = acc[:, hh * HEAD_DIM:(hh + 1) * HEAD_DIM]
                kind = kinds[h + hh]
                if kind in ('rope', 'rope_q'):
                    t = (t * cos_ref[...] + pltpu.roll(t, HEAD_DIM - _HALF, axis=1) * sin_lo_ref[...]
                         + pltpu.roll(t, _HALF, axis=1) * sin_hi_ref[...])
                if kind == 'rope_q':
                    t = t * EXP2_SCALE
                elif kind == 'silu':
                    t = _silu(t)
                elif isinstance(kind, tuple):
                    t = _log_forget(t, lbl_ref[:, kind[1] * HEAD_DIM:(kind[1] + 1) * HEAD_DIM], layer)
                o_ref[:, sl] = t.astype(o_ref.dtype)
            h += nh

    patterns = {}
    for jj, kinds in enumerate(kinds_of_tile):
        patterns.setdefault(kinds, []).append(jj)
    for kinds, tiles in patterns.items():
        cond = functools.reduce(jnp.logical_or, [j == jj for jj in tiles])
        pl.when(cond)(functools.partial(tile_body, kinds))


def _proj(hn, w, layer, out_dtype, tn, kinds, tabs=None, lb_logits=None):
    T, D = hn.shape
    N = w.shape[2]
    tm = min(T, PROJ_TM)
    hpt = tn // HEAD_DIM
    kinds_of_tile = tuple(tuple(kinds[jj * hpt:(jj + 1) * hpt]) for jj in range(N // tn))
    in_specs = [pl.BlockSpec((tm, D), lambda i, j: (i, 0)),
                pl.BlockSpec((None, D, tn), lambda i, j: (layer, 0, j))]
    args = [hn, w]
    if tabs is not None:
        in_specs += [pl.BlockSpec((tm, LANES), lambda i, j: (i, 0))] * len(tabs)
        args += list(tabs)
    if lb_logits is not None:
        in_specs.append(pl.BlockSpec(lb_logits.shape, lambda i, j: (0, 0)))
        args.append(lb_logits)
    return pl.pallas_call(
        functools.partial(_proj_kernel, kinds_of_tile=kinds_of_tile, has_rope=tabs is not None, layer=layer),
        grid=(T // tm, N // tn),
        in_specs=in_specs,
        out_specs=pl.BlockSpec((tm, tn), lambda i, j: (i, j)),
        out_shape=jax.ShapeDtypeStruct((T, N), out_dtype),
        compiler_params=_cparams("parallel", "arbitrary"),
        name="in_proj",
    )(*args)


def _compress_kernel(kc_ref, vc_ref, kpos_ref, kw1_ref, kw2_ref, vpos_ref, vw1_ref, vw2_ref,
                     kT_ref, v_ref, shift_ref, *, ncp):
    half_blk = CMP_BLOCK // 2

    def comp(t_ref, pos_ref, w1_ref, w2_ref):
        a = jnp.zeros((ncp, HEAD_DIM), F32)
        bm = jnp.zeros((ncp, HEAD_DIM), F32)
        for l in range(half_blk):
            xl = t_ref[pl.ds(l, ncp, stride=CMP_STRIDE), :]
            a = a + _dot((xl + pos_ref[l:l + 1, :]).astype(BF16), w1_ref[l])
            bm = bm + _dot((xl + pos_ref[half_blk + l:half_blk + l + 1, :]).astype(BF16),
                           w1_ref[half_blk + l])
        shift_ref[pl.ds(0, ncp), :] = bm
        shift_ref[pl.ds(ncp, 8), :] = jnp.zeros((8, HEAD_DIM), F32)
        hid = _silu(a + shift_ref[pl.ds(1, ncp), :])
        out = _dot(hid.astype(BF16), w2_ref[...])
        row = lax.broadcasted_iota(jnp.int32, out.shape, 0)
        return jnp.where(row < ncp - 1, out, 0.0)

    kT_ref[0, 0] = comp(kc_ref, kpos_ref, kw1_ref, kw2_ref).T.astype(BF16)
    v_ref[0, 0] = comp(vc_ref, vpos_ref, vw1_ref, vw2_ref).astype(BF16)


def _compress(pf, B, S, kc_blk, vc_blk, layer, ck_pos, ck_w1, ck_w2, cv_pos, cv_w1, cv_w2):
    G = NSA_KV_GROUPS
    ncp = S // CMP_STRIDE
    full2 = lambda b, g: (layer, 0, 0)
    full3 = lambda b, g: (layer, 0, 0, 0)
    return pl.pallas_call(
        functools.partial(_compress_kernel, ncp=ncp),
        grid=(B, G),
        in_specs=[pl.BlockSpec((S, HEAD_DIM), lambda b, g: (b, kc_blk + g)),
                  pl.BlockSpec((S, HEAD_DIM), lambda b, g: (b, vc_blk + g)),
                  pl.BlockSpec((None, CMP_BLOCK, HEAD_DIM), full2),
                  pl.BlockSpec((None, CMP_BLOCK, HEAD_DIM, HEAD_DIM), full3),
                  pl.BlockSpec((None, HEAD_DIM, HEAD_DIM), full2),
                  pl.BlockSpec((None, CMP_BLOCK, HEAD_DIM), full2),
                  pl.BlockSpec((None, CMP_BLOCK, HEAD_DIM, HEAD_DIM), full3),
                  pl.BlockSpec((None, HEAD_DIM, HEAD_DIM), full2)],
        out_specs=[pl.BlockSpec((1, 1, HEAD_DIM, ncp), lambda b, g: (b, g, 0, 0)),
                   pl.BlockSpec((1, 1, ncp, HEAD_DIM), lambda b, g: (b, g, 0, 0))],
        out_shape=[jax.ShapeDtypeStruct((B, G, HEAD_DIM, ncp), BF16),
                   jax.ShapeDtypeStruct((B, G, ncp, HEAD_DIM), BF16)],
        scratch_shapes=[pltpu.VMEM((ncp + 8, HEAD_DIM), F32)],
        compiler_params=_cparams("parallel", "parallel"),
        name="nsa_compress",
    )(pf, pf, ck_pos, ck_w1, ck_w2, cv_pos, cv_w1, cv_w2)


def _cmp_topk_kernel(q_ref, kT_ref, v_ref, o_ref, bias_ref, imp_ref, *, tq, ncp):
    i = pl.program_id(2)
    s0 = i * tq
    t_col = s0 + lax.broadcasted_iota(jnp.int32, (tq, 1), 0)
    any_visible = (t_col >= CMP_BLOCK - 1).astype(F32)

    def attend(nb):
        nc = nb * LANES
        n_row = lax.broadcasted_iota(jnp.int32, (1, nc), 1)
        mc = jnp.logical_and(n_row * CMP_STRIDE + (CMP_BLOCK - 1) <= t_col, n_row < ncp - 1)
        kT = kT_ref[0, 0, :, 0:nc]
        v = v_ref[0, 0, 0:nc, :]
        psum = jnp.zeros((tq, nc), F32)
        for r in range(NSA_REP):
            sl = slice(r * HEAD_DIM, (r + 1) * HEAD_DIM)
            s = jnp.where(mc, _dot(q_ref[:, sl], kT), NEG)
            m = jnp.max(s, axis=-1, keepdims=True)
            e = jnp.exp2(s - m)
            l = jnp.sum(e, axis=-1, keepdims=True)
            pc = e * (any_visible / l)
            o_ref[:, sl] = _dot(pc.astype(BF16), v)
            psum = psum + pc
        n_col = lax.broadcasted_iota(jnp.int32, (nc, LANES), 0) * CMP_STRIDE
        k_row = lax.broadcasted_iota(jnp.int32, (nc, LANES), 1) * SEL_BLOCK
        ovl = jnp.logical_and(n_col < k_row + SEL_BLOCK, n_col + CMP_BLOCK > k_row).astype(BF16)
        hi = psum.astype(BF16)
        lo = (psum - hi.astype(F32)).astype(BF16)
        imp_ref[...] = _dot(hi, ovl) + _dot(lo, ovl)

    nb_total = ncp // LANES
    last_visible = (s0 + tq - CMP_BLOCK) // CMP_STRIDE
    nb_needed = jnp.minimum(last_visible // LANES + 1, nb_total)
    for nb in range(1, nb_total + 1):
        pl.when(nb_needed == nb)(functools.partial(attend, nb))

    blk = lax.broadcasted_iota(jnp.int32, (LANES, LANES), 0)
    lane = lax.broadcasted_iota(jnp.int32, (LANES, LANES), 1)
    groups = []
    for c in range(0, tq, LANES):
        impT = imp_ref[c:c + LANES, :].T
        cur = (s0 + c + lane) // SEL_BLOCK
        forced = jnp.logical_or(blk == 0, jnp.logical_or(blk == cur, blk == cur - 1))
        groups.append(jnp.where(forced, BELOW_NEG, jnp.where(blk <= cur, impT, NEG)))
    for _ in range(SEL_TOPK - 3):
        nxt = []
        for sc in groups:
            m = jnp.max(sc, axis=0, keepdims=True)
            idx = jnp.min(jnp.where(sc == m, blk, LANES), axis=0, keepdims=True)
            nxt.append(jnp.where(blk == idx, BELOW_NEG, sc))
        groups = nxt
    for g, sc in enumerate(groups):
        biasT = jnp.where(sc == BELOW_NEG, 0.0, NEG)
        bias_ref[g * LANES:(g + 1) * LANES, :] = biasT.T.astype(BF16)


def _cmp_topk(pb, kT, vcmp, B, S):
    G = NSA_KV_GROUPS
    T = B * S
    tq = min(S, CMP_TQ)
    nq = S // tq
    ncp = S // CMP_STRIDE
    gw = NSA_REP * HEAD_DIM
    return pl.pallas_call(
        functools.partial(_cmp_topk_kernel, tq=tq, ncp=ncp),
        grid=(B, G, nq),
        in_specs=[pl.BlockSpec((tq, gw), lambda b, g, i: (b * nq + i, g)),
                  pl.BlockSpec((1, 1, HEAD_DIM, ncp), lambda b, g, i: (b, g, 0, 0)),
                  pl.BlockSpec((1, 1, ncp, HEAD_DIM), lambda b, g, i: (b, g, 0, 0))],
        out_specs=[pl.BlockSpec((tq, gw), lambda b, g, i: (b * nq + i, g)),
                   pl.BlockSpec((tq, LANES), lambda b, g, i: (b * nq + i, g))],
        out_shape=[jax.ShapeDtypeStruct((T, NSA_HEADS * HEAD_DIM), F32),
                   jax.ShapeDtypeStruct((T, G * LANES), BF16)],
        scratch_shapes=[pltpu.VMEM((tq, LANES), F32)],
        compiler_params=_cparams("parallel", "parallel", "parallel"),
        name="nsa_cmp_topk",
    )(pb, kT, vcmp)


def _win_kernel(q_ref, kp_ref, kc_ref, vp_ref, vc_ref, o_ref, *, sub):
    i = pl.program_id(2)
    row = lax.broadcasted_iota(jnp.int32, (sub, sub), 0)
    col = lax.broadcasted_iota(jnp.int32, (sub, sub), 1)
    ones = jnp.ones((sub, LANES), BF16)

    def body(first):
        ks = [kp_ref[0:sub, :], kp_ref[sub:2 * sub, :], kc_ref[0:sub, :], kc_ref[sub:2 * sub, :]]
        vs = [jnp.concatenate([v, ones], axis=1) for v in
              (vp_ref[0:sub, :], vp_ref[sub:2 * sub, :], vc_ref[0:sub, :], vc_ref[sub:2 * sub, :])]
        for a in range(2):
            blocks = [(a, col > row), (a + 1, None), (a + 2, col <= row)]
            if first:
                blocks = blocks[2 - a:]
            rows = slice(a * sub, (a + 1) * sub)
            for r in range(NSA_REP):
                sl = slice(r * HEAD_DIM, (r + 1) * HEAD_DIM)
                q = q_ref[rows, sl]
                ss = []
                for kb, mask in blocks:
                    s = _dot_nt(q, ks[kb])
                    ss.append(s if mask is None else jnp.where(mask, s, NEG))
                m = functools.reduce(jnp.maximum, ss)
                m = jnp.max(m, axis=-1, keepdims=True)
                acc = None
                for s, (kb, _) in zip(ss, blocks):
                    part = _dot(jnp.exp2(s - m).astype(BF16), vs[kb])
                    acc = part if acc is None else acc + part
                o_ref[rows, sl] = acc[:, 0:HEAD_DIM] / acc[:, HEAD_DIM:2 * HEAD_DIM]

    pl.when(i > 0)(functools.partial(body, False))
    pl.when(i == 0)(functools.partial(body, True))


def _window(pb, B, S, k_blk, v_blk):
    G = NSA_KV_GROUPS
    T = B * S
    sub = WINDOW // 2
    tq = 2 * sub
    nq = S // tq
    gw = NSA_REP * HEAD_DIM

    def kv_spec(col0, back):
        return pl.BlockSpec((tq, HEAD_DIM),
                            lambda b, g, i: (b * nq + jnp.maximum(i - back, 0), col0 + g))

    return pl.pallas_call(
        functools.partial(_win_kernel, sub=sub),
        grid=(B, G, nq),
        in_specs=[pl.BlockSpec((tq, gw), lambda b, g, i: (b * nq + i, g)),
                  kv_spec(k_blk, 1), kv_spec(k_blk, 0), kv_spec(v_blk, 1), kv_spec(v_blk, 0)],
        out_specs=pl.BlockSpec((tq, gw), lambda b, g, i: (b * nq + i, g)),
        out_shape=jax.ShapeDtypeStruct((T, NSA_HEADS * HEAD_DIM), F32),
        compiler_params=_cparams("parallel", "parallel", "parallel"),
        name="nsa_window",
    )(pb, pb, pb, pb, pb)


def _sel_kernel(qi_ref, kj_ref, q_ref, bias_ref, k_ref, v_ref, o_ref, qa_ref, m_ref, acc_ref, *, tq, tk):
    n = pl.program_id(2)
    i = qi_ref[n]
    j = kj_ref[n]
    last_j = (i * tq + tq - 1) // tk

    @pl.when(j == 0)
    def _():
        for r in range(NSA_REP):
            rows = slice(r * tq, (r + 1) * tq)
            qa_ref[rows, 0:HEAD_DIM] = q_ref[:, r * HEAD_DIM:(r + 1) * HEAD_DIM]
            qa_ref[rows, HEAD_DIM:2 * HEAD_DIM] = bias_ref[...]
        m_ref[...] = jnp.full(m_ref.shape, NEG, F32)
        acc_ref[...] = jnp.zeros(acc_ref.shape, F32)

    def step(col0, ncols, diagonal):
        key = j * tk + col0 + lax.broadcasted_iota(jnp.int32, (ncols, LANES), 0)
        blk = lax.broadcasted_iota(jnp.int32, (ncols, LANES), 1)
        onehot = (key // SEL_BLOCK == blk).astype(BF16)
        ka = jnp.concatenate([k_ref[col0:col0 + ncols, :], onehot], axis=1)
        va = jnp.concatenate([v_ref[col0:col0 + ncols, :], jnp.ones((ncols, LANES), BF16)], axis=1)
        if diagonal:
            visible = (lax.broadcasted_iota(jnp.int32, (tq, tq), 1)
                       <= lax.broadcasted_iota(jnp.int32, (tq, tq), 0))
        def scores(r):
            s = _dot_nt(qa_ref[r * tq:(r + 1) * tq, :], ka)
            return jnp.where(visible, s, NEG) if diagonal else s

        s_next = scores(0)
        for r in range(NSA_REP):
            rows = slice(r * tq, (r + 1) * tq)
            s = s_next
            if r + 1 < NSA_REP:
                s_next = scores(r + 1)
            m_prev = m_ref[rows, :]
            m_new = jnp.maximum(m_prev, jnp.max(s, axis=-1, keepdims=True))
            alpha = jnp.exp2(m_prev - m_new)
            p = jnp.exp2(s - jnp.tile(m_new, (1, ncols // LANES)))
            acc_ref[rows, :] = jnp.tile(alpha, (1, 2)) * acc_ref[rows, :] + _dot(p.astype(BF16), va)
            m_ref[rows, :] = m_new

    @pl.when(j < last_j)
    def _():
        step(0, tk, False)

    for sub in range(tk // tq):
        @pl.when(jnp.logical_and(j == last_j, i % (tk // tq) == sub))
        def _(sub=sub):
            if sub > 0:
                step(0, sub * tq, False)
            step(sub * tq, tq, True)

    @pl.when(j == last_j)
    def _():
        for r in range(NSA_REP):
            rows = slice(r * tq, (r + 1) * tq)
            o_ref[:, r * HEAD_DIM:(r + 1) * HEAD_DIM] = (acc_ref[rows, 0:HEAD_DIM]
                                                         / acc_ref[rows, HEAD_DIM:2 * HEAD_DIM])


def _selection(pb, bias, B, S, k_blk, v_blk):
    G = NSA_KV_GROUPS
    T = B * S
    tq = min(S, SEL_TQ)
    tk = min(S, SEL_TK)
    assert tk % tq == 0
    nq = S // tq
    nk = S // tk
    gw = NSA_REP * HEAD_DIM
    pairs = [(i, j) for i in range(nq) for j in range((i * tq + tq - 1) // tk + 1)]
    qi = jnp.asarray([p[0] for p in pairs], jnp.int32)
    kj = jnp.asarray([p[1] for p in pairs], jnp.int32)

    def q_map(b, g, n, qi_ref, kj_ref):
        return (b * nq + qi_ref[n], g)

    def kv_spec(col0):
        return pl.BlockSpec((tk, HEAD_DIM),
                            lambda b, g, n, qi_ref, kj_ref: (b * nk + kj_ref[n], col0 + g))

    grid_spec = pltpu.PrefetchScalarGridSpec(
        num_scalar_prefetch=2,
        grid=(B, G, len(pairs)),
        in_specs=[pl.BlockSpec((tq, gw), q_map), pl.BlockSpec((tq, LANES), q_map),
                  kv_spec(k_blk), kv_spec(v_blk)],
        out_specs=pl.BlockSpec((tq, gw), q_map),
        scratch_shapes=[pltpu.VMEM((NSA_REP * tq, 2 * HEAD_DIM), BF16),
                        pltpu.VMEM((NSA_REP * tq, LANES), F32),
                        pltpu.VMEM((NSA_REP * tq, 2 * HEAD_DIM), F32)])
    return pl.pallas_call(
        functools.partial(_sel_kernel, tq=tq, tk=tk),
        grid_spec=grid_spec,
        out_shape=jax.ShapeDtypeStruct((T, NSA_HEADS * HEAD_DIM), F32),
        compiler_params=_cparams("parallel", "parallel", "arbitrary"),
        name="nsa_selection",
    )(qi, kj, pb, bias, pb, pb)


def _nsa_out_kernel(x_ref, oc_ref, os_ref, ow_ref, z_ref, gl_ref, w_ref, nw_ref, o_ref, hn_ref):
    sig = jax.nn.sigmoid(gl_ref[...])
    H = NSA_HEADS
    acc = x_ref[...]
    for hp in range(H // 2):
        parts = []
        for h in (2 * hp, 2 * hp + 1):
            sl = slice(h * HEAD_DIM, (h + 1) * HEAD_DIM)
            o = (sig[:, h:h + 1] * oc_ref[:, sl] + sig[:, H + h:H + h + 1] * os_ref[:, sl]
                 + sig[:, 2 * H + h:2 * H + h + 1] * ow_ref[:, sl])
            parts.append((o * _silu(z_ref[:, sl])).astype(BF16))
        rows = slice(2 * hp * HEAD_DIM, (2 * hp + 2) * HEAD_DIM)
        acc = acc + _dot(jnp.concatenate(parts, axis=1), w_ref[rows, :])
    o_ref[...] = acc
    hn_ref[...] = _rms_bf16(acc, nw_ref[...])


def _nsa_out(x, o_cmp, o_sel, o_win, pf, z_blk, gl_blk, w_out, layer, next_norm_w):
    T, D = x.shape
    W = o_cmp.shape[1]
    tm = min(T, OUT_TM)
    row = lambda i: (i, 0)
    return pl.pallas_call(
        _nsa_out_kernel,
        grid=(T // tm,),
        in_specs=[pl.BlockSpec((tm, D), row),
                  pl.BlockSpec((tm, W), row), pl.BlockSpec((tm, W), row), pl.BlockSpec((tm, W), row),
                  pl.BlockSpec((tm, W), lambda i: (i, z_blk)),
                  pl.BlockSpec((tm, LANES), lambda i: (i, gl_blk)),
                  pl.BlockSpec((None, W, D), lambda i: (layer, 0, 0)),
                  pl.BlockSpec((1, D), lambda i: (0, 0))],
        out_specs=[pl.BlockSpec((tm, D), row), pl.BlockSpec((tm, D), row)],
        out_shape=[jax.ShapeDtypeStruct((T, D), F32), jax.ShapeDtypeStruct((T, D), BF16)],
        compiler_params=_cparams("parallel"),
        name="nsa_out",
    )(x, o_cmp, o_sel, o_win, pf, pf, w_out, next_norm_w.reshape(1, D))


def _prepare_params(nsa_w_in, nsa_ck_pos, nsa_ck_w1, nsa_ck_w2, nsa_cv_pos, nsa_cv_w1, nsa_cv_w2,
                    nsa_w_out, hgrn_w_in, hgrn_w_out):
    H, G, d = NSA_HEADS, NSA_KV_GROUPS, HEAD_DIM
    kvw = G * d
    c = [0, H * d]
    for _ in range(6):
        c.append(c[-1] + kvw)
    c.append(c[-1] + 3 * H)
    c.append(c[-1] + H * d)
    seg = lambda n: nsa_w_in[:, :, c[n]:c[n + 1]]
    wq, wkc, wvc, wks, wvs, wkw, wvw, wgl, wz = [seg(n) for n in range(9)]
    pad = jnp.zeros(nsa_w_in.shape[:2] + (LANES - 3 * H,), nsa_w_in.dtype)
    return dict(
        nsa_wb=jnp.concatenate([wq, wks, wkw, wvs, wvw], axis=2).astype(BF16),
        nsa_wf=jnp.concatenate([wz, wkc, wvc, wgl, pad], axis=2).astype(BF16),
        ck_pos=nsa_ck_pos, ck_w1=nsa_ck_w1.astype(BF16), ck_w2=nsa_ck_w2.astype(BF16),
        cv_pos=nsa_cv_pos, cv_w1=nsa_cv_w1.astype(BF16), cv_w2=nsa_cv_w2.astype(BF16),
        nsa_wo=nsa_w_out.astype(BF16),
        hgrn_wi=hgrn_w_in, hgrn_wo=hgrn_w_out.astype(BF16))


def _nsa_layer(x, hn, B, S, tabs, prm, j, next_norm_w):
    H, G = NSA_HEADS, NSA_KV_GROUPS
    pb = _proj(hn, prm["nsa_wb"], j, BF16, 1024, ['rope_q'] * H + ['rope'] * (2 * G) + ['plain'] * (2 * G),
               tabs=tabs)
    pf = _proj(hn, prm["nsa_wf"], j, F32, 640, ['plain'] * H + ['rope'] * G + ['plain'] * (G + 1), tabs=tabs)
    ks_blk, kw_blk, vs_blk, vw_blk = H, H + G, H + 2 * G, H + 3 * G
    kc_blk, vc_blk, gl_blk = H, H + G, H + 2 * G

    kT, vcmp = _compress(pf, B, S, kc_blk, vc_blk, j, prm["ck_pos"], prm["ck_w1"], prm["ck_w2"],
                         prm["cv_pos"], prm["cv_w1"], prm["cv_w2"])
    o_cmp, bias = _cmp_topk(pb, kT, vcmp, B, S)
    o_win = _window(pb, B, S, kw_blk, vw_blk)
    o_sel = _selection(pb, bias, B, S, ks_blk, vs_blk)
    return _nsa_out(x, o_cmp, o_sel, o_win, pf, 0, gl_blk, prm["nsa_wo"], j, next_norm_w)


def _hgrn_kernel(q_ref, g_ref, i_ref, z_ref, gw_ref, o_ref, st_ref, *, C, sub, hb):
    n = pl.program_id(2)

    @pl.when(n == 0)
    def _():
        st_ref[...] = jnp.zeros(st_ref.shape, F32)

    r0 = lax.broadcasted_iota(jnp.int32, (C, C), 0)
    r1 = lax.broadcasted_iota(jnp.int32, (C, C), 1)
    tri = (r0 >= r1).astype(BF16)
    dsub = sub // 2
    drow = lax.broadcasted_iota(jnp.int32, (C // dsub, dsub, HEAD_DIM), 1)
    row_id = lax.broadcasted_iota(jnp.int32, (C, HEAD_DIM), 0)

    for h in range(hb):
        cs = slice(h * HEAD_DIM, (h + 1) * HEAD_DIM)
        g = g_ref[:, cs]
        k = 1.0 - jnp.exp(g)
        qs = q_ref[:, cs]
        iv = i_ref[:, cs]
        ib = iv.astype(BF16)

        g1 = g.astype(BF16)
        rem = g - g1.astype(F32)
        g2 = rem.astype(BF16)
        g3 = (rem - g2.astype(F32)).astype(BF16)
        b = _dot(tri, g1) + _dot(tri, g2) + _dot(tri, g3)

        stT = st_ref[h]
        o_inter = _dot_nt((qs * jnp.exp(b)).astype(BF16), stT.astype(BF16))

        nd = C // dsub
        b3 = b.reshape(nd, dsub, HEAD_DIM)
        q3 = qs.reshape(nd, dsub, HEAD_DIM)
        k3 = k.reshape(nd, dsub, HEAD_DIM)
        i3 = iv.reshape(nd, dsub, HEAD_DIM)
        o3 = jnp.zeros((nd, dsub, HEAD_DIM), F32)
        for s in range(dsub):
            w = jnp.where(drow >= s, jnp.exp(b3 - b3[:, s:s + 1, :]), 0.0)
            col = jnp.sum(q3 * w * k3[:, s:s + 1, :], axis=-1, keepdims=True)
            o3 = o3 + col * i3[:, s:s + 1, :]

        att = None
        bsz = 2 * dsub
        while bsz <= C:
            bl = b.reshape(C // bsz, bsz, HEAD_DIM)
            bmid = jnp.broadcast_to(bl[:, bsz // 2 - 1:bsz // 2, :], bl.shape).reshape(C, HEAD_DIM)
            second = (row_id % bsz) >= bsz // 2
            qk = jnp.where(second, qs, k) * jnp.exp(-jnp.abs(b - bmid))
            qh = jnp.where(second, qk, 0.0).astype(BF16)
            kh = jnp.where(second, 0.0, qk).astype(BF16)
            a = _dot_nt(qh, kh)
            if bsz < C:
                a = jnp.where(r0 // bsz == r1 // bsz, a, 0.0)
            att = a if att is None else att + a
            bsz *= 2
        o = o_inter + o3.reshape(C, HEAD_DIM) + _dot(att.astype(BF16), ib)

        b_last = b[C - 1:C, :]
        kd = (k * jnp.exp(b_last - b)).astype(BF16)
        st_ref[h] = stT * jnp.exp(b_last) + _dot(iv.T.astype(BF16), kd)

        o = o * lax.rsqrt(jnp.mean(o * o, axis=-1, keepdims=True) + RMS_EPS) * gw_ref[...]
        o_ref[:, cs] = (o * z_ref[:, cs]).astype(BF16)


def _hgrn_recurrence(ph, gnorm_w, B, S):
    H = HGRN_HEADS
    T = B * S
    C = min(S, HGRN_CHUNK)
    nc = S // C
    hb = HGRN_HEADS_PER_STEP
    hg = H // hb
    w = hb * HEAD_DIM

    def col(c0):
        return pl.BlockSpec((C, w), lambda b, h, n: (b * nc + n, c0 + h))

    return pl.pallas_call(
        functools.partial(_hgrn_kernel, C=C, sub=HGRN_SUB, hb=hb),
        grid=(B, hg, nc),
        in_specs=[col(0), col(hg), col(2 * hg), col(3 * hg),
                  pl.BlockSpec((1, HEAD_DIM), lambda b, h, n: (0, 0))],
        out_specs=pl.BlockSpec((C, w), lambda b, h, n: (b * nc + n, h)),
        out_shape=jax.ShapeDtypeStruct((T, H * HGRN_I_DIM), BF16),
        scratch_shapes=[pltpu.VMEM((hb, HGRN_I_DIM, HGRN_F_DIM), F32)],
        compiler_params=_cparams("parallel", "parallel", "arbitrary"),
        name="hgrn_recurrence",
    )(ph, ph, ph, ph, gnorm_w.reshape(1, HEAD_DIM))


def _hgrn_out_kernel(x_ref, og_ref, w_ref, nw_ref, *out_refs, final_norm):
    y = x_ref[...] + _dot(og_ref[...], w_ref[...])
    if final_norm:
        ms = jnp.mean(y * y, axis=-1, keepdims=True)
        out_refs[0][...] = y * lax.rsqrt(ms + RMS_EPS) * nw_ref[...]
    else:
        out_refs[0][...] = y
        out_refs[1][...] = _rms_bf16(y, nw_ref[...])


def _hgrn_out(x, og, w_out, layer, norm_w, final_norm):
    T, D = x.shape
    W = og.shape[1]
    tm = min(T, 2 * OUT_TM)
    row = lambda i: (i, 0)
    out_specs = [pl.BlockSpec((tm, D), row)]
    out_shape = [jax.ShapeDtypeStruct((T, D), F32)]
    if not final_norm:
        out_specs.append(pl.BlockSpec((tm, D), row))
        out_shape.append(jax.ShapeDtypeStruct((T, D), BF16))
    return pl.pallas_call(
        functools.partial(_hgrn_out_kernel, final_norm=final_norm),
        grid=(T // tm,),
        in_specs=[pl.BlockSpec((tm, D), row), pl.BlockSpec((tm, W), row),
                  pl.BlockSpec((None, W, D), lambda i: (layer, 0, 0)),
                  pl.BlockSpec((1, D), lambda i: (0, 0))],
        out_specs=out_specs,
        out_shape=out_shape,
        compiler_params=_cparams("parallel"),
        name="hgrn_out",
    )(x, og, w_out, norm_w.reshape(1, D))


def _hgrn_layer(x, hn, B, S, prm, lb_logits, gnorm_w, j, norm_w, final_norm):
    H = HGRN_HEADS
    kinds = ['silu'] * H + [('logf', h) for h in range(H)] + ['plain'] * H + ['silu'] * H
    ph = _proj(hn, prm["hgrn_wi"], j, F32, 512, kinds, lb_logits=lb_logits)
    og = _hgrn_recurrence(ph, gnorm_w, B, S)
    return _hgrn_out(x, og, prm["hgrn_wo"], j, norm_w, final_norm)


def kernel(x, positions, norm_w, final_norm_w, nsa_w_in, nsa_ck_pos, nsa_ck_w1, nsa_ck_w2,
           nsa_cv_pos, nsa_cv_w1, nsa_cv_w2, nsa_w_out, hgrn_w_in, hgrn_lb_logits,
           hgrn_gnorm_w, hgrn_w_out):
    B, S, D = x.shape
    depth = norm_w.shape[0]
    assert depth % 2 == 0
    xf = x.reshape(B * S, D)
    tabs = _rope_tables(positions)
    prm = _prepare_params(nsa_w_in, nsa_ck_pos, nsa_ck_w1, nsa_ck_w2, nsa_cv_pos, nsa_cv_w1,
                          nsa_cv_w2, nsa_w_out, hgrn_w_in, hgrn_w_out)
    hn = _rmsnorm(xf, norm_w[0])
    for layer in range(depth):
        j = layer // 2
        last = layer == depth - 1
        if layer % 2 == 0:
            xf, hn = _nsa_layer(xf, hn, B, S, tabs, prm, j, norm_w[layer + 1])
        elif last:
            (xf,) = _hgrn_layer(xf, hn, B, S, prm, hgrn_lb_logits, hgrn_gnorm_w[j], j, final_norm_w, True)
        else:
            xf, hn = _hgrn_layer(xf, hn, B, S, prm, hgrn_lb_logits, hgrn_gnorm_w[j], j,
                                 norm_w[layer + 1], False)
    return xf.reshape(B, S, D)
```

```python
import functools
import math

import jax
import jax.numpy as jnp
from jax import lax
from jax.experimental import pallas as pl
from jax.experimental.pallas import tpu as pltpu

F32 = jnp.float32
BF16 = jnp.bfloat16

RMS_EPS = 1e-6
NEG = -1e30
BELOW_NEG = -3e38
HEAD_DIM = 128
NSA_HEADS = 16
NSA_KV_GROUPS = 4
NSA_REP = NSA_HEADS // NSA_KV_GROUPS
CMP_BLOCK = 32
CMP_STRIDE = 16
SEL_BLOCK = 64
SEL_TOPK = 16
WINDOW = 512
ROPE_THETA = 500000.0
ROPE_DIM = HEAD_DIM // 4
HGRN_HEADS = 16
HGRN_F_DIM = 128
HGRN_I_DIM = 128

LANES = 128
VMEM_LIMIT_BYTES = 56 * 1024 * 1024

ATTN_SCALE = HEAD_DIM ** -0.5
EXP2_SCALE = ATTN_SCALE * math.log2(math.e)

PROJ_TM = 2048
PROJ_CHUNK_HEADS = 2
CMP_TQ = 512
SEL_TQ = 512
SEL_TK = 2048
OUT_TM = 256
HGRN_CHUNK = 128
HGRN_SUB = 16
HGRN_HEADS_PER_STEP = 4


def _cparams(*sem):
    return pltpu.CompilerParams(dimension_semantics=sem, vmem_limit_bytes=VMEM_LIMIT_BYTES)


def _dot(a, b):
    return jnp.dot(a, b, preferred_element_type=F32)


def _dot_nt(a, b):
    return lax.dot_general(a, b, (((1,), (1,)), ((), ())), preferred_element_type=F32)


def _silu(v):
    return v * jax.nn.sigmoid(v)


_HALF = ROPE_DIM // 2


def _rope_table_kernel(pos_ref, inv_ref, cos_ref, sin_lo_ref, sin_hi_ref):
    ang = pos_ref[...] * inv_ref[...]
    lane = lax.broadcasted_iota(jnp.int32, ang.shape, 1)
    s = jnp.sin(ang)
    cos_ref[...] = jnp.cos(ang)
    sin_lo_ref[...] = jnp.where(lane < _HALF, -s, 0.0)
    sin_hi_ref[...] = jnp.where(lane < _HALF, 0.0, s)


def _rope_tables(positions):
    T = positions.size
    tm = min(T, PROJ_TM)
    inv = ROPE_THETA ** (-jnp.arange(_HALF, dtype=F32) / _HALF)
    inv_pat = jnp.concatenate([inv, inv, jnp.zeros((LANES - ROPE_DIM,), F32)]).reshape(1, LANES)
    pos = positions.astype(F32).reshape(T, 1)
    return pl.pallas_call(
        _rope_table_kernel,
        grid=(T // tm,),
        in_specs=[pl.BlockSpec((tm, 1), lambda i: (i, 0)),
                  pl.BlockSpec((1, LANES), lambda i: (0, 0))],
        out_specs=[pl.BlockSpec((tm, LANES), lambda i: (i, 0))] * 3,
        out_shape=[jax.ShapeDtypeStruct((T, LANES), F32)] * 3,
        compiler_params=_cparams("parallel"),
        name="rope_tables",
    )(pos, inv_pat)


def _rms_bf16(y, w):
    ms = jnp.mean(y * y, axis=-1, keepdims=True)
    return (y * lax.rsqrt(ms + RMS_EPS) * w).astype(BF16)


def _rmsnorm_kernel(x_ref, w_ref, o_ref):
    o_ref[...] = _rms_bf16(x_ref[...], w_ref[...])


def _rmsnorm(x, w):
    T, D = x.shape
    tm = min(T, 2 * OUT_TM)
    return pl.pallas_call(
        _rmsnorm_kernel,
        grid=(T // tm,),
        in_specs=[pl.BlockSpec((tm, D), lambda i: (i, 0)), pl.BlockSpec((1, D), lambda i: (0, 0))],
        out_specs=pl.BlockSpec((tm, D), lambda i: (i, 0)),
        out_shape=jax.ShapeDtypeStruct((T, D), BF16),
        compiler_params=_cparams("parallel"),
        name="rmsnorm",
    )(x, w.reshape(1, D))


def _log_forget(t, lbl, layer):
    e = jnp.exp(lbl - jnp.max(lbl, axis=0, keepdims=True))
    p = e / jnp.sum(e, axis=0, keepdims=True)
    lb = jnp.sum(p[0:layer + 1, :], axis=0, keepdims=True) - p[0:1, :]
    u = jnp.exp(-jnp.abs(t))
    num = jnp.where(t >= 0.0, 1.0 + lb * u, lb + u)
    return jnp.maximum(jnp.log(num), jnp.minimum(t, 0.0)) - jnp.log(1.0 + u)


def _proj_kernel(*refs, kinds_of_tile, has_rope, layer):
    if has_rope:
        hn_ref, w_ref, cos_ref, sin_lo_ref, sin_hi_ref, o_ref = refs
    elif any(isinstance(k, tuple) for kinds in kinds_of_tile for k in kinds):
        hn_ref, w_ref, lbl_ref, o_ref = refs
    else:
        hn_ref, w_ref, o_ref = refs
    j = pl.program_id(1)
    heads_per_tile = len(kinds_of_tile[0])

    def tile_body(kinds):
        h = 0
        while h < heads_per_tile:
            nh = min(PROJ_CHUNK_HEADS, heads_per_tile - h)
            acc = _dot(hn_ref[...], w_ref[:, h * HEAD_DIM:(h + nh) * HEAD_DIM].astype(BF16))
            for hh in range(nh):
                sl = slice((h + hh) * HEAD_DIM, (h + hh + 1) * HEAD_DIM)
                t = acc[:, hh * HEAD_DIM:(hh + 1) * HEAD_DIM]
                kind = kinds[h + hh]
                if kind in ('rope', 'rope_q'):
                    t = (t * cos_ref[...] + pltpu.roll(t, HEAD_DIM - _HALF, axis=1) * sin_lo_ref[...]
                         + pltpu.roll(t, _HALF, axis=1) * sin_hi_ref[...])
                if kind == 'rope_q':
                    t = t * EXP2_SCALE
                elif kind == 'silu':
                    t = _silu(t)
                elif isinstance(kind, tuple):
                    t = _log_forget(t, lbl_ref[:, kind[1] * HEAD_DIM:(kind[1] + 1) * HEAD_DIM], layer)
                o_ref[:, sl] = t.astype(o_ref.dtype)
            h += nh

    patterns = {}
    for jj, kinds in enumerate(kinds_of_tile):
        patterns.setdefault(kinds, []).append(jj)
    for kinds, tiles in patterns.items():
        cond = functools.reduce(jnp.logical_or, [j == jj for jj in tiles])
        pl.when(cond)(functools.partial(tile_body, kinds))


def _proj(hn, w, layer, out_dtype, tn, kinds, tabs=None, lb_logits=None):
    T, D = hn.shape
    N = w.shape[2]
    tm = min(T, PROJ_TM)
    hpt = tn // HEAD_DIM
    kinds_of_tile = tuple(tuple(kinds[jj * hpt:(jj + 1) * hpt]) for jj in range(N // tn))
    in_specs = [pl.BlockSpec((tm, D), lambda i, j: (i, 0)),
                pl.BlockSpec((None, D, tn), lambda i, j: (layer, 0, j))]
    args = [hn, w]
    if tabs is not None:
        in_specs += [pl.BlockSpec((tm, LANES), lambda i, j: (i, 0))] * len(tabs)
        args += list(tabs)
    if lb_logits is not None:
        in_specs.append(pl.BlockSpec(lb_logits.shape, lambda i, j: (0, 0)))
        args.append(lb_logits)
    return pl.pallas_call(
        functools.partial(_proj_kernel, kinds_of_tile=kinds_of_tile, has_rope=tabs is not None, layer=layer),
        grid=(T // tm, N // tn),
        in_specs=in_specs,
        out_specs=pl.BlockSpec((tm, tn), lambda i, j: (i, j)),
        out_shape=jax.ShapeDtypeStruct((T, N), out_dtype),
        compiler_params=_cparams("parallel", "arbitrary"),
        name="in_proj",
    )(*args)


def _compress_kernel(kc_ref, vc_ref, kpos_ref, kw1_ref, kw2_ref, vpos_ref, vw1_ref, vw2_ref,
                     kT_ref, v_ref, shift_ref, *, ncp):
    half_blk = CMP_BLOCK // 2

    def comp(t_ref, pos_ref, w1_ref, w2_ref):
        a = jnp.zeros((ncp, HEAD_DIM), F32)
        bm = jnp.zeros((ncp, HEAD_DIM), F32)
        for l in range(half_blk):
            xl = t_ref[pl.ds(l, ncp, stride=CMP_STRIDE), :]
            a = a + _dot((xl + pos_ref[l:l + 1, :]).astype(BF16), w1_ref[l])
            bm = bm + _dot((xl + pos_ref[half_blk + l:half_blk + l + 1, :]).astype(BF16),
                           w1_ref[half_blk + l])
        shift_ref[pl.ds(0, ncp), :] = bm
        shift_ref[pl.ds(ncp, 8), :] = jnp.zeros((8, HEAD_DIM), F32)
        hid = _silu(a + shift_ref[pl.ds(1, ncp), :])
        out = _dot(hid.astype(BF16), w2_ref[...])
        row = lax.broadcasted_iota(jnp.int32, out.shape, 0)
        return jnp.where(row < ncp - 1, out, 0.0)

    kT_ref[0, 0] = comp(kc_ref, kpos_ref, kw1_ref, kw2_ref).T.astype(BF16)
    v_ref[0, 0] = comp(vc_ref, vpos_ref, vw1_ref, vw2_ref).astype(BF16)


def _compress(pf, B, S, kc_blk, vc_blk, layer, ck_pos, ck_w1, ck_w2, cv_pos, cv_w1, cv_w2):
    G = NSA_KV_GROUPS
    ncp = S // CMP_STRIDE
    full2 = lambda b, g: (layer, 0, 0)
    full3 = lambda b, g: (layer, 0, 0, 0)
    return pl.pallas_call(
        functools.partial(_compress_kernel, ncp=ncp),
        grid=(B, G),
        in_specs=[pl.BlockSpec((S, HEAD_DIM), lambda b, g: (b, kc_blk + g)),
                  pl.BlockSpec((S, HEAD_DIM), lambda b, g: (b, vc_blk + g)),
                  pl.BlockSpec((None, CMP_BLOCK, HEAD_DIM), full2),
                  pl.BlockSpec((None, CMP_BLOCK, HEAD_DIM, HEAD_DIM), full3),
                  pl.BlockSpec((None, HEAD_DIM, HEAD_DIM), full2),
                  pl.BlockSpec((None, CMP_BLOCK, HEAD_DIM), full2),
                  pl.BlockSpec((None, CMP_BLOCK, HEAD_DIM, HEAD_DIM), full3),
                  pl.BlockSpec((None, HEAD_DIM, HEAD_DIM), full2)],
        out_specs=[pl.BlockSpec((1, 1, HEAD_DIM, ncp), lambda b, g: (b, g, 0, 0)),
                   pl.BlockSpec((1, 1, ncp, HEAD_DIM), lambda b, g: (b, g, 0, 0))],
        out_shape=[jax.ShapeDtypeStruct((B, G, HEAD_DIM, ncp), BF16),
                   jax.ShapeDtypeStruct((B, G, ncp, HEAD_DIM), BF16)],
        scratch_shapes=[pltpu.VMEM((ncp + 8, HEAD_DIM), F32)],
        compiler_params=_cparams("parallel", "parallel"),
        name="nsa_compress",
    )(pf, pf, ck_pos, ck_w1, ck_w2, cv_pos, cv_w1, cv_w2)


def _cmp_topk_kernel(q_ref, kT_ref, v_ref, o_ref, bias_ref, imp_ref, *, tq, ncp):
    i = pl.program_id(2)
    s0 = i * tq
    t_col = s0 + lax.broadcasted_iota(jnp.int32, (tq, 1), 0)
    any_visible = (t_col >= CMP_BLOCK - 1).astype(F32)

    def attend(nb):
        nc = nb * LANES
        n_row = lax.broadcasted_iota(jnp.int32, (1, nc), 1)
        mc = jnp.logical_and(n_row * CMP_STRIDE + (CMP_BLOCK - 1) <= t_col, n_row < ncp - 1)
        kT = kT_ref[0, 0, :, 0:nc]
        v = v_ref[0, 0, 0:nc, :]
        psum = jnp.zeros((tq, nc), F32)
        for r in range(NSA_REP):
            sl = slice(r * HEAD_DIM, (r + 1) * HEAD_DIM)
            s = jnp.where(mc, _dot(q_ref[:, sl], kT), NEG)
            m = jnp.max(s, axis=-1, keepdims=True)
            e = jnp.exp2(s - m)
            l = jnp.sum(e, axis=-1, keepdims=True)
            pc = e * (any_visible / l)
            o_ref[:, sl] = _dot(pc.astype(BF16), v)
            psum = psum + pc
        n_col = lax.broadcasted_iota(jnp.int32, (nc, LANES), 0) * CMP_STRIDE
        k_row = lax.broadcasted_iota(jnp.int32, (nc, LANES), 1) * SEL_BLOCK
        ovl = jnp.logical_and(n_col < k_row + SEL_BLOCK, n_col + CMP_BLOCK > k_row).astype(BF16)
        hi = psum.astype(BF16)
        lo = (psum - hi.astype(F32)).astype(BF16)
        imp_ref[...] = _dot(hi, ovl) + _dot(lo, ovl)

    def topk(nvb):
        blk = lax.broadcasted_iota(jnp.int32, (nvb, LANES), 0)
        lane = lax.broadcasted_iota(jnp.int32, (nvb, LANES), 1)
        groups = []
        for c in range(0, tq, LANES):
            impT = imp_ref[c:c + LANES, :].T[0:nvb, :]
            cur = (s0 + c + lane) // SEL_BLOCK
            forced = jnp.logical_or(blk == 0, jnp.logical_or(blk == cur, blk == cur - 1))
            groups.append(jnp.where(forced, BELOW_NEG, jnp.where(blk <= cur, impT, NEG)))
        for _ in range(SEL_TOPK - 3):
            nxt = []
            for sc in groups:
                m = jnp.max(sc, axis=0, keepdims=True)
                idx = jnp.min(jnp.where(sc == m, blk, LANES), axis=0, keepdims=True)
                nxt.append(jnp.where(blk == idx, BELOW_NEG, sc))
            groups = nxt
        for g, sc in enumerate(groups):
            biasT = jnp.where(sc == BELOW_NEG, 0.0, NEG)
            if nvb < LANES:
                biasT = jnp.concatenate([biasT, jnp.full((LANES - nvb, LANES), NEG, F32)], axis=0)
            bias_ref[g * LANES:(g + 1) * LANES, :] = biasT.T.astype(BF16)

    nb_total = ncp // LANES
    sel_per_nb = LANES * CMP_STRIDE // SEL_BLOCK
    last_visible = (s0 + tq - CMP_BLOCK) // CMP_STRIDE
    nb_needed = jnp.minimum(last_visible // LANES + 1, nb_total)
    for nb in range(1, nb_total + 1):
        @pl.when(nb_needed == nb)
        def _(nb=nb):
            attend(nb)
            topk(min(LANES, nb * sel_per_nb))


def _cmp_topk(pb, kT, vcmp, B, S):
    G = NSA_KV_GROUPS
    T = B * S
    tq = min(S, CMP_TQ)
    nq = S // tq
    ncp = S // CMP_STRIDE
    gw = NSA_REP * HEAD_DIM
    return pl.pallas_call(
        functools.partial(_cmp_topk_kernel, tq=tq, ncp=ncp),
        grid=(B, G, nq),
        in_specs=[pl.BlockSpec((tq, gw), lambda b, g, i: (b * nq + i, g)),
                  pl.BlockSpec((1, 1, HEAD_DIM, ncp), lambda b, g, i: (b, g, 0, 0)),
                  pl.BlockSpec((1, 1, ncp, HEAD_DIM), lambda b, g, i: (b, g, 0, 0))],
        out_specs=[pl.BlockSpec((tq, gw), lambda b, g, i: (b * nq + i, g)),
                   pl.BlockSpec((tq, LANES), lambda b, g, i: (b * nq + i, g))],
        out_shape=[jax.ShapeDtypeStruct((T, NSA_HEADS * HEAD_DIM), F32),
                   jax.ShapeDtypeStruct((T, G * LANES), BF16)],
        scratch_shapes=[pltpu.VMEM((tq, LANES), F32)],
        compiler_params=_cparams("parallel", "parallel", "parallel"),
        name="nsa_cmp_topk",
    )(pb, kT, vcmp)


def _win_kernel(q_ref, kp_ref, kc_ref, vp_ref, vc_ref, o_ref, *, sub):
    i = pl.program_id(2)
    row = lax.broadcasted_iota(jnp.int32, (sub, sub), 0)
    col = lax.broadcasted_iota(jnp.int32, (sub, sub), 1)
    ones = jnp.ones((sub, LANES), BF16)

    def body(first):
        ks = [kp_ref[0:sub, :], kp_ref[sub:2 * sub, :], kc_ref[0:sub, :], kc_ref[sub:2 * sub, :]]
        vs = [jnp.concatenate([v, ones], axis=1) for v in
              (vp_ref[0:sub, :], vp_ref[sub:2 * sub, :], vc_ref[0:sub, :], vc_ref[sub:2 * sub, :])]
        for a in range(2):
            blocks = [(a, col > row), (a + 1, None), (a + 2, col <= row)]
            if first:
                blocks = blocks[2 - a:]
            rows = slice(a * sub, (a + 1) * sub)
            for r in range(NSA_REP):
                sl = slice(r * HEAD_DIM, (r + 1) * HEAD_DIM)
                q = q_ref[rows, sl]
                ss = []
                for kb, mask in blocks:
                    s = _dot_nt(q, ks[kb])
                    ss.append(s if mask is None else jnp.where(mask, s, NEG))
                m = functools.reduce(jnp.maximum, ss)
                m = jnp.max(m, axis=-1, keepdims=True)
                acc = None
                for s, (kb, _) in zip(ss, blocks):
                    part = _dot(jnp.exp2(s - m).astype(BF16), vs[kb])
                    acc = part if acc is None else acc + part
                o_ref[rows, sl] = acc[:, 0:HEAD_DIM] / acc[:, HEAD_DIM:2 * HEAD_DIM]

    pl.when(i > 0)(functools.partial(body, False))
    pl.when(i == 0)(functools.partial(body, True))


def _window(pb, B, S, k_blk, v_blk):
    G = NSA_KV_GROUPS
    T = B * S
    sub = WINDOW // 2
    tq = 2 * sub
    nq = S // tq
    gw = NSA_REP * HEAD_DIM

    def kv_spec(col0, back):
        return pl.BlockSpec((tq, HEAD_DIM),
                            lambda b, g, i: (b * nq + jnp.maximum(i - back, 0), col0 + g))

    return pl.pallas_call(
        functools.partial(_win_kernel, sub=sub),
        grid=(B, G, nq),
        in_specs=[pl.BlockSpec((tq, gw), lambda b, g, i: (b * nq + i, g)),
                  kv_spec(k_blk, 1), kv_spec(k_blk, 0), kv_spec(v_blk, 1), kv_spec(v_blk, 0)],
        out_specs=pl.BlockSpec((tq, gw), lambda b, g, i: (b * nq + i, g)),
        out_shape=jax.ShapeDtypeStruct((T, NSA_HEADS * HEAD_DIM), F32),
        compiler_params=_cparams("parallel", "parallel", "parallel"),
        name="nsa_window",
    )(pb, pb, pb, pb, pb)


def _sel_kernel(qi_ref, kj_ref, q_ref, bias_ref, k_ref, v_ref, o_ref, qa_ref, m_ref, acc_ref, *, tq, tk):
    n = pl.program_id(2)
    i = qi_ref[n]
    j = kj_ref[n]
    last_j = (i * tq + tq - 1) // tk

    @pl.when(j == 0)
    def _():
        for r in range(NSA_REP):
            rows = slice(r * tq, (r + 1) * tq)
            qa_ref[rows, 0:HEAD_DIM] = q_ref[:, r * HEAD_DIM:(r + 1) * HEAD_DIM]
            qa_ref[rows, HEAD_DIM:2 * HEAD_DIM] = bias_ref[...]
        m_ref[...] = jnp.full(m_ref.shape, NEG, F32)
        acc_ref[...] = jnp.zeros(acc_ref.shape, F32)

    def step(col0, ncols, diagonal):
        key = j * tk + col0 + lax.broadcasted_iota(jnp.int32, (ncols, LANES), 0)
        blk = lax.broadcasted_iota(jnp.int32, (ncols, LANES), 1)
        onehot = (key // SEL_BLOCK == blk).astype(BF16)
        ka = jnp.concatenate([k_ref[col0:col0 + ncols, :], onehot], axis=1)
        va = jnp.concatenate([v_ref[col0:col0 + ncols, :], jnp.ones((ncols, LANES), BF16)], axis=1)
        if diagonal:
            visible = (lax.broadcasted_iota(jnp.int32, (tq, tq), 1)
                       <= lax.broadcasted_iota(jnp.int32, (tq, tq), 0))
        def scores(r):
            s = _dot_nt(qa_ref[r * tq:(r + 1) * tq, :], ka)
            return jnp.where(visible, s, NEG) if diagonal else s

        s_next = scores(0)
        for r in range(NSA_REP):
            rows = slice(r * tq, (r + 1) * tq)
            s = s_next
            if r + 1 < NSA_REP:
                s_next = scores(r + 1)
            m_prev = m_ref[rows, :]
            m_new = jnp.maximum(m_prev, jnp.max(s, axis=-1, keepdims=True))
            alpha = jnp.exp2(m_prev - m_new)
            p = jnp.exp2(s - jnp.tile(m_new, (1, ncols // LANES)))
            acc_ref[rows, :] = jnp.tile(alpha, (1, 2)) * acc_ref[rows, :] + _dot(p.astype(BF16), va)
            m_ref[rows, :] = m_new

    @pl.when(j < last_j)
    def _():
        step(0, tk, False)

    for sub in range(tk // tq):
        @pl.when(jnp.logical_and(j == last_j, i % (tk // tq) == sub))
        def _(sub=sub):
            if sub > 0:
                step(0, sub * tq, False)
            step(sub * tq, tq, True)

    @pl.when(j == last_j)
    def _():
        for r in range(NSA_REP):
            rows = slice(r * tq, (r + 1) * tq)
            o_ref[:, r * HEAD_DIM:(r + 1) * HEAD_DIM] = (acc_ref[rows, 0:HEAD_DIM]
                                                         / acc_ref[rows, HEAD_DIM:2 * HEAD_DIM])


def _selection(pb, bias, B, S, k_blk, v_blk):
    G = NSA_KV_GROUPS
    T = B * S
    tq = min(S, SEL_TQ)
    tk = min(S, SEL_TK)
    assert tk % tq == 0
    nq = S // tq
    nk = S // tk
    gw = NSA_REP * HEAD_DIM
    pairs = [(i, j) for i in range(nq) for j in range((i * tq + tq - 1) // tk + 1)]
    qi = jnp.asarray([p[0] for p in pairs], jnp.int32)
    kj = jnp.asarray([p[1] for p in pairs], jnp.int32)

    def q_map(b, g, n, qi_ref, kj_ref):
        return (b * nq + qi_ref[n], g)

    def kv_spec(col0):
        return pl.BlockSpec((tk, HEAD_DIM),
                            lambda b, g, n, qi_ref, kj_ref: (b * nk + kj_ref[n], col0 + g))

    grid_spec = pltpu.PrefetchScalarGridSpec(
        num_scalar_prefetch=2,
        grid=(B, G, len(pairs)),
        in_specs=[pl.BlockSpec((tq, gw), q_map), pl.BlockSpec((tq, LANES), q_map),
                  kv_spec(k_blk), kv_spec(v_blk)],
        out_specs=pl.BlockSpec((tq, gw), q_map),
        scratch_shapes=[pltpu.VMEM((NSA_REP * tq, 2 * HEAD_DIM), BF16),
                        pltpu.VMEM((NSA_REP * tq, LANES), F32),
                        pltpu.VMEM((NSA_REP * tq, 2 * HEAD_DIM), F32)])
    return pl.pallas_call(
        functools.partial(_sel_kernel, tq=tq, tk=tk),
        grid_spec=grid_spec,
        out_shape=jax.ShapeDtypeStruct((T, NSA_HEADS * HEAD_DIM), F32),
        compiler_params=_cparams("parallel", "parallel", "arbitrary"),
        name="nsa_selection",
    )(qi, kj, pb, bias, pb, pb)


def _nsa_out_kernel(x_ref, oc_ref, os_ref, ow_ref, z_ref, gl_ref, w_ref, nw_ref, o_ref, hn_ref):
    sig = jax.nn.sigmoid(gl_ref[...])
    H = NSA_HEADS
    acc = x_ref[...]
    for hp in range(H // 2):
        parts = []
        for h in (2 * hp, 2 * hp + 1):
            sl = slice(h * HEAD_DIM, (h + 1) * HEAD_DIM)
            o = (sig[:, h:h + 1] * oc_ref[:, sl] + sig[:, H + h:H + h + 1] * os_ref[:, sl]
                 + sig[:, 2 * H + h:2 * H + h + 1] * ow_ref[:, sl])
            parts.append((o * _silu(z_ref[:, sl])).astype(BF16))
        rows = slice(2 * hp * HEAD_DIM, (2 * hp + 2) * HEAD_DIM)
        acc = acc + _dot(jnp.concatenate(parts, axis=1), w_ref[rows, :])
    o_ref[...] = acc
    hn_ref[...] = _rms_bf16(acc, nw_ref[...])


def _nsa_out(x, o_cmp, o_sel, o_win, pf, z_blk, gl_blk, w_out, layer, next_norm_w):
    T, D = x.shape
    W = o_cmp.shape[1]
    tm = min(T, OUT_TM)
    row = lambda i: (i, 0)
    return pl.pallas_call(
        _nsa_out_kernel,
        grid=(T // tm,),
        in_specs=[pl.BlockSpec((tm, D), row),
                  pl.BlockSpec((tm, W), row), pl.BlockSpec((tm, W), row), pl.BlockSpec((tm, W), row),
                  pl.BlockSpec((tm, W), lambda i: (i, z_blk)),
                  pl.BlockSpec((tm, LANES), lambda i: (i, gl_blk)),
                  pl.BlockSpec((None, W, D), lambda i: (layer, 0, 0)),
                  pl.BlockSpec((1, D), lambda i: (0, 0))],
        out_specs=[pl.BlockSpec((tm, D), row), pl.BlockSpec((tm, D), row)],
        out_shape=[jax.ShapeDtypeStruct((T, D), F32), jax.ShapeDtypeStruct((T, D), BF16)],
        compiler_params=_cparams("parallel"),
        name="nsa_out",
    )(x, o_cmp, o_sel, o_win, pf, pf, w_out, next_norm_w.reshape(1, D))


def _prepare_params(nsa_w_in, nsa_ck_pos, nsa_ck_w1, nsa_ck_w2, nsa_cv_pos, nsa_cv_w1, nsa_cv_w2,
                    nsa_w_out, hgrn_w_in, hgrn_w_out):
    H, G, d = NSA_HEADS, NSA_KV_GROUPS, HEAD_DIM
    kvw = G * d
    c = [0, H * d]
    for _ in range(6):
        c.append(c[-1] + kvw)
    c.append(c[-1] + 3 * H)
    c.append(c[-1] + H * d)
    seg = lambda n: nsa_w_in[:, :, c[n]:c[n + 1]]
    wq, wkc, wvc, wks, wvs, wkw, wvw, wgl, wz = [seg(n) for n in range(9)]
    pad = jnp.zeros(nsa_w_in.shape[:2] + (LANES - 3 * H,), nsa_w_in.dtype)
    return dict(
        nsa_wb=jnp.concatenate([wq, wks, wkw, wvs, wvw], axis=2).astype(BF16),
        nsa_wf=jnp.concatenate([wz, wkc, wvc, wgl, pad], axis=2).astype(BF16),
        ck_pos=nsa_ck_pos, ck_w1=nsa_ck_w1.astype(BF16), ck_w2=nsa_ck_w2.astype(BF16),
        cv_pos=nsa_cv_pos, cv_w1=nsa_cv_w1.astype(BF16), cv_w2=nsa_cv_w2.astype(BF16),
        nsa_wo=nsa_w_out.astype(BF16),
        hgrn_wi=hgrn_w_in, hgrn_wo=hgrn_w_out.astype(BF16))


def _nsa_layer(x, hn, B, S, tabs, prm, j, next_norm_w):
    H, G = NSA_HEADS, NSA_KV_GROUPS
    pb = _proj(hn, prm["nsa_wb"], j, BF16, 1024, ['rope_q'] * H + ['rope'] * (2 * G) + ['plain'] * (2 * G),
               tabs=tabs)
    pf = _proj(hn, prm["nsa_wf"], j, F32, 640, ['plain'] * H + ['rope'] * G + ['plain'] * (G + 1), tabs=tabs)
    ks_blk, kw_blk, vs_blk, vw_blk = H, H + G, H + 2 * G, H + 3 * G
    kc_blk, vc_blk, gl_blk = H, H + G, H + 2 * G

    kT, vcmp = _compress(pf, B, S, kc_blk, vc_blk, j, prm["ck_pos"], prm["ck_w1"], prm["ck_w2"],
                         prm["cv_pos"], prm["cv_w1"], prm["cv_w2"])
    o_cmp, bias = _cmp_topk(pb, kT, vcmp, B, S)
    o_win = _window(pb, B, S, kw_blk, vw_blk)
    o_sel = _selection(pb, bias, B, S, ks_blk, vs_blk)
    return _nsa_out(x, o_cmp, o_sel, o_win, pf, 0, gl_blk, prm["nsa_wo"], j, next_norm_w)


def _hgrn_kernel(q_ref, g_ref, i_ref, z_ref, gw_ref, o_ref, st_ref, *, C, sub, hb):
    n = pl.program_id(2)

    @pl.when(n == 0)
    def _():
        st_ref[...] = jnp.zeros(st_ref.shape, F32)

    r0 = lax.broadcasted_iota(jnp.int32, (C, C), 0)
    r1 = lax.broadcasted_iota(jnp.int32, (C, C), 1)
    tri = (r0 >= r1).astype(BF16)
    dsub = sub // 2
    drow = lax.broadcasted_iota(jnp.int32, (C // dsub, dsub, HEAD_DIM), 1)
    row_id = lax.broadcasted_iota(jnp.int32, (C, HEAD_DIM), 0)

    for h in range(hb):
        cs = slice(h * HEAD_DIM, (h + 1) * HEAD_DIM)
        g = g_ref[:, cs]
        k = 1.0 - jnp.exp(g)
        qs = q_ref[:, cs]
        iv = i_ref[:, cs]
        ib = iv.astype(BF16)

        g1 = g.astype(BF16)
        rem = g - g1.astype(F32)
        g2 = rem.astype(BF16)
        g3 = (rem - g2.astype(F32)).astype(BF16)
        b = _dot(tri, g1) + _dot(tri, g2) + _dot(tri, g3)

        stT = st_ref[h]
        o_inter = _dot_nt((qs * jnp.exp(b)).astype(BF16), stT.astype(BF16))

        nd = C // dsub
        b3 = b.reshape(nd, dsub, HEAD_DIM)
        q3 = qs.reshape(nd, dsub, HEAD_DIM)
        k3 = k.reshape(nd, dsub, HEAD_DIM)
        i3 = iv.reshape(nd, dsub, HEAD_DIM)
        o3 = jnp.zeros((nd, dsub, HEAD_DIM), F32)
        for s in range(dsub):
            w = jnp.where(drow >= s, jnp.exp(b3 - b3[:, s:s + 1, :]), 0.0)
            col = jnp.sum(q3 * w * k3[:, s:s + 1, :], axis=-1, keepdims=True)
            o3 = o3 + col * i3[:, s:s + 1, :]

        att = None
        bsz = 2 * dsub
        while bsz <= C:
            bl = b.reshape(C // bsz, bsz, HEAD_DIM)
            bmid = jnp.broadcast_to(bl[:, bsz // 2 - 1:bsz // 2, :], bl.shape).reshape(C, HEAD_DIM)
            second = (row_id % bsz) >= bsz // 2
            qk = jnp.where(second, qs, k) * jnp.exp(-jnp.abs(b - bmid))
            qh = jnp.where(second, qk, 0.0).astype(BF16)
            kh = jnp.where(second, 0.0, qk).astype(BF16)
            a = _dot_nt(qh, kh)
            if bsz < C:
                a = jnp.where(r0 // bsz == r1 // bsz, a, 0.0)
            att = a if att is None else att + a
            bsz *= 2
        o = o_inter + o3.reshape(C, HEAD_DIM) + _dot(att.astype(BF16), ib)

        b_last = b[C - 1:C, :]
        kd = (k * jnp.exp(b_last - b)).astype(BF16)
        st_ref[h] = stT * jnp.exp(b_last) + _dot(iv.T.astype(BF16), kd)

        o = o * lax.rsqrt(jnp.mean(o * o, axis=-1, keepdims=True) + RMS_EPS) * gw_ref[...]
        o_ref[:, cs] = (o * z_ref[:, cs]).astype(BF16)


def _hgrn_recurrence(ph, gnorm_w, B, S):
    H = HGRN_HEADS
    T = B * S
    C = min(S, HGRN_CHUNK)
    nc = S // C
    hb = HGRN_HEADS_PER_STEP
    hg = H // hb
    w = hb * HEAD_DIM

    def col(c0):
        return pl.BlockSpec((C, w), lambda b, h, n: (b * nc + n, c0 + h))

    return pl.pallas_call(
        functools.partial(_hgrn_kernel, C=C, sub=HGRN_SUB, hb=hb),
        grid=(B, hg, nc),
        in_specs=[col(0), col(hg), col(2 * hg), col(3 * hg),
                  pl.BlockSpec((1, HEAD_DIM), lambda b, h, n: (0, 0))],
        out_specs=pl.BlockSpec((C, w), lambda b, h, n: (b * nc + n, h)),
        out_shape=jax.ShapeDtypeStruct((T, H * HGRN_I_DIM), BF16),
        scratch_shapes=[pltpu.VMEM((hb, HGRN_I_DIM, HGRN_F_DIM), F32)],
        compiler_params=_cparams("parallel", "parallel", "arbitrary"),
        name="hgrn_recurrence",
    )(ph, ph, ph, ph, gnorm_w.reshape(1, HEAD_DIM))


def _hgrn_out_kernel(x_ref, og_ref, w_ref, nw_ref, *out_refs, final_norm):
    y = x_ref[...] + _dot(og_ref[...], w_ref[...])
    if final_norm:
        ms = jnp.mean(y * y, axis=-1, keepdims=True)
        out_refs[0][...] = y * lax.rsqrt(ms + RMS_EPS) * nw_ref[...]
    else:
        out_refs[0][...] = y
        out_refs[1][...] = _rms_bf16(y, nw_ref[...])


def _hgrn_out(x, og, w_out, layer, norm_w, final_norm):
    T, D = x.shape
    W = og.shape[1]
    tm = min(T, 2 * OUT_TM)
    row = lambda i: (i, 0)
    out_specs = [pl.BlockSpec((tm, D), row)]
    out_shape = [jax.ShapeDtypeStruct((T, D), F32)]
    if not final_norm:
        out_specs.append(pl.BlockSpec((tm, D), row))
        out_shape.append(jax.ShapeDtypeStruct((T, D), BF16))
    return pl.pallas_call(
        functools.partial(_hgrn_out_kernel, final_norm=final_norm),
        grid=(T // tm,),
        in_specs=[pl.BlockSpec((tm, D), row), pl.BlockSpec((tm, W), row),
                  pl.BlockSpec((None, W, D), lambda i: (layer, 0, 0)),
                  pl.BlockSpec((1, D), lambda i: (0, 0))],
        out_specs=out_specs,
        out_shape=out_shape,
        compiler_params=_cparams("parallel"),
        name="hgrn_out",
    )(x, og, w_out, norm_w.reshape(1, D))


def _hgrn_layer(x, hn, B, S, prm, lb_logits, gnorm_w, j, norm_w, final_norm):
    H = HGRN_HEADS
    kinds = ['silu'] * H + [('logf', h) for h in range(H)] + ['plain'] * H + ['silu'] * H
    ph = _proj(hn, prm["hgrn_wi"], j, F32, 512, kinds, lb_logits=lb_logits)
    og = _hgrn_recurrence(ph, gnorm_w, B, S)
    return _hgrn_out(x, og, prm["hgrn_wo"], j, norm_w, final_norm)


def kernel(x, positions, norm_w, final_norm_w, nsa_w_in, nsa_ck_pos, nsa_ck_w1, nsa_ck_w2,
           nsa_cv_pos, nsa_cv_w1, nsa_cv_w2, nsa_w_out, hgrn_w_in, hgrn_lb_logits,
           hgrn_gnorm_w, hgrn_w_out):
    B, S, D = x.shape
    depth = norm_w.shape[0]
    assert depth % 2 == 0
    xf = x.reshape(B * S, D)
    tabs = _rope_tables(positions)
    prm = _prepare_params(nsa_w_in, nsa_ck_pos, nsa_ck_w1, nsa_ck_w2, nsa_cv_pos, nsa_cv_w1,
                          nsa_cv_w2, nsa_w_out, hgrn_w_in, hgrn_w_out)
    hn = _rmsnorm(xf, norm_w[0])
    for layer in range(depth):
        j = layer // 2
        last = layer == depth - 1
        if layer % 2 == 0:
            xf, hn = _nsa_layer(xf, hn, B, S, tabs, prm, j, norm_w[layer + 1])
        elif last:
            (xf,) = _hgrn_layer(xf, hn, B, S, prm, hgrn_lb_logits, hgrn_gnorm_w[j], j, final_norm_w, True)
        else:
            xf, hn = _hgrn_layer(xf, hn, B, S, prm, hgrn_lb_logits, hgrn_gnorm_w[j], j,
                                 norm_w[layer + 1], False)
    return xf.reshape(B, S, D)
```

```python
import functools
import math

import jax
import jax.numpy as jnp
from jax import lax
from jax.experimental import pallas as pl
from jax.experimental.pallas import tpu as pltpu

F32 = jnp.float32
BF16 = jnp.bfloat16

RMS_EPS = 1e-6
NEG = -1e30
BELOW_NEG = -3e38
HEAD_DIM = 128
NSA_HEADS = 16
NSA_KV_GROUPS = 4
NSA_REP = NSA_HEADS // NSA_KV_GROUPS
CMP_BLOCK = 32
CMP_STRIDE = 16
SEL_BLOCK = 64
SEL_TOPK = 16
WINDOW = 512
ROPE_THETA = 500000.0
ROPE_DIM = HEAD_DIM // 4
HGRN_HEADS = 16
HGRN_F_DIM = 128
HGRN_I_DIM = 128

LANES = 128
VMEM_LIMIT_BYTES = 56 * 1024 * 1024

ATTN_SCALE = HEAD_DIM ** -0.5
EXP2_SCALE = ATTN_SCALE * math.log2(math.e)

PROJ_TM = 2048
PROJ_CHUNK_HEADS = 2
CMP_TQ = 512
SEL_TQ = 512
SEL_TK = 2048
OUT_TM = 256
HGRN_CHUNK = 128
HGRN_SUB = 16
HGRN_HEADS_PER_STEP = 16


def _cparams(*sem):
    return pltpu.CompilerParams(dimension_semantics=sem, vmem_limit_bytes=VMEM_LIMIT_BYTES)


def _dot(a, b):
    return jnp.dot(a, b, preferred_element_type=F32)


def _dot_nt(a, b):
    return lax.dot_general(a, b, (((1,), (1,)), ((), ())), preferred_element_type=F32)


def _silu(v):
    return v * jax.nn.sigmoid(v)


_HALF = ROPE_DIM // 2


def _rope_table_kernel(pos_ref, inv_ref, cos_ref, sin_lo_ref, sin_hi_ref):
    ang = pos_ref[...] * inv_ref[...]
    lane = lax.broadcasted_iota(jnp.int32, ang.shape, 1)
    s = jnp.sin(ang)
    cos_ref[...] = jnp.cos(ang)
    sin_lo_ref[...] = jnp.where(lane < _HALF, -s, 0.0)
    sin_hi_ref[...] = jnp.where(lane < _HALF, 0.0, s)


def _rope_tables(positions):
    T = positions.size
    tm = min(T, PROJ_TM)
    inv = ROPE_THETA ** (-jnp.arange(_HALF, dtype=F32) / _HALF)
    inv_pat = jnp.concatenate([inv, inv, jnp.zeros((LANES - ROPE_DIM,), F32)]).reshape(1, LANES)
    pos = positions.astype(F32).reshape(T, 1)
    return pl.pallas_call(
        _rope_table_kernel,
        grid=(T // tm,),
        in_specs=[pl.BlockSpec((tm, 1), lambda i: (i, 0)),
                  pl.BlockSpec((1, LANES), lambda i: (0, 0))],
        out_specs=[pl.BlockSpec((tm, LANES), lambda i: (i, 0))] * 3,
        out_shape=[jax.ShapeDtypeStruct((T, LANES), F32)] * 3,
        compiler_params=_cparams("parallel"),
        name="rope_tables",
    )(pos, inv_pat)


def _rms_bf16(y, w):
    ms = jnp.mean(y * y, axis=-1, keepdims=True)
    return (y * lax.rsqrt(ms + RMS_EPS) * w).astype(BF16)


def _rmsnorm_kernel(x_ref, w_ref, o_ref):
    o_ref[...] = _rms_bf16(x_ref[...], w_ref[...])


def _rmsnorm(x, w):
    T, D = x.shape
    tm = min(T, 2 * OUT_TM)
    return pl.pallas_call(
        _rmsnorm_kernel,
        grid=(T // tm,),
        in_specs=[pl.BlockSpec((tm, D), lambda i: (i, 0)), pl.BlockSpec((1, D), lambda i: (0, 0))],
        out_specs=pl.BlockSpec((tm, D), lambda i: (i, 0)),
        out_shape=jax.ShapeDtypeStruct((T, D), BF16),
        compiler_params=_cparams("parallel"),
        name="rmsnorm",
    )(x, w.reshape(1, D))


def _log_forget(t, lbl, layer):
    e = jnp.exp(lbl - jnp.max(lbl, axis=0, keepdims=True))
    p = e / jnp.sum(e, axis=0, keepdims=True)
    lb = jnp.sum(p[0:layer + 1, :], axis=0, keepdims=True) - p[0:1, :]
    u = jnp.exp(-jnp.abs(t))
    num = jnp.where(t >= 0.0, 1.0 + lb * u, lb + u)
    return jnp.maximum(jnp.log(num), jnp.minimum(t, 0.0)) - jnp.log(1.0 + u)


def _proj_kernel(*refs, kinds_of_tile, has_rope, layer):
    if has_rope:
        hn_ref, w_ref, cos_ref, sin_lo_ref, sin_hi_ref, o_ref = refs
    elif any(isinstance(k, tuple) for kinds in kinds_of_tile for k in kinds):
        hn_ref, w_ref, lbl_ref, o_ref = refs
    else:
        hn_ref, w_ref, o_ref = refs
    j = pl.program_id(1)
    heads_per_tile = len(kinds_of_tile[0])

    def tile_body(kinds):
        h = 0
        while h < heads_per_tile:
            nh = min(PROJ_CHUNK_HEADS, heads_per_tile - h)
            acc = _dot(hn_ref[...], w_ref[:, h * HEAD_DIM:(h + nh) * HEAD_DIM].astype(BF16))
            for hh in range(nh):
                sl = slice((h + hh) * HEAD_DIM, (h + hh + 1) * HEAD_DIM)
                t = acc[:, hh * HEAD_DIM:(hh + 1) * HEAD_DIM]
                kind = kinds[h + hh]
                if kind in ('rope', 'rope_q'):
                    t = (t * cos_ref[...] + pltpu.roll(t, HEAD_DIM - _HALF, axis=1) * sin_lo_ref[...]
                         + pltpu.roll(t, _HALF, axis=1) * sin_hi_ref[...])
                if kind == 'rope_q':
                    t = t * EXP2_SCALE
                elif kind == 'silu':
                    t = _silu(t)
                elif isinstance(kind, tuple):
                    t = _log_forget(t, lbl_ref[:, kind[1] * HEAD_DIM:(kind[1] + 1) * HEAD_DIM], layer)
                o_ref[:, sl] = t.astype(o_ref.dtype)
            h += nh

    patterns = {}
    for jj, kinds in enumerate(kinds_of_tile):
        patterns.setdefault(kinds, []).append(jj)
    for kinds, tiles in patterns.items():
        cond = functools.reduce(jnp.logical_or, [j == jj for jj in tiles])
        pl.when(cond)(functools.partial(tile_body, kinds))


def _proj(hn, w, layer, out_dtype, tn, kinds, tabs=None, lb_logits=None):
    T, D = hn.shape
    N = w.shape[2]
    tm = min(T, PROJ_TM)
    hpt = tn // HEAD_DIM
    kinds_of_tile = tuple(tuple(kinds[jj * hpt:(jj + 1) * hpt]) for jj in range(N // tn))
    in_specs = [pl.BlockSpec((tm, D), lambda i, j: (i, 0)),
                pl.BlockSpec((None, D, tn), lambda i, j: (layer, 0, j))]
    args = [hn, w]
    if tabs is not None:
        in_specs += [pl.BlockSpec((tm, LANES), lambda i, j: (i, 0))] * len(tabs)
        args += list(tabs)
    if lb_logits is not None:
        in_specs.append(pl.BlockSpec(lb_logits.shape, lambda i, j: (0, 0)))
        args.append(lb_logits)
    return pl.pallas_call(
        functools.partial(_proj_kernel, kinds_of_tile=kinds_of_tile, has_rope=tabs is not None, layer=layer),
        grid=(T // tm, N // tn),
        in_specs=in_specs,
        out_specs=pl.BlockSpec((tm, tn), lambda i, j: (i, j)),
        out_shape=jax.ShapeDtypeStruct((T, N), out_dtype),
        compiler_params=_cparams("parallel", "arbitrary"),
        name="in_proj",
    )(*args)


def _compress_kernel(kc_ref, vc_ref, kpos_ref, kw1_ref, kw2_ref, vpos_ref, vw1_ref, vw2_ref,
                     kT_ref, v_ref, shift_ref, *, ncp):
    half_blk = CMP_BLOCK // 2

    def comp(t_ref, pos_ref, w1_ref, w2_ref):
        a = jnp.zeros((ncp, HEAD_DIM), F32)
        bm = jnp.zeros((ncp, HEAD_DIM), F32)
        for l in range(half_blk):
            xl = t_ref[pl.ds(l, ncp, stride=CMP_STRIDE), :]
            a = a + _dot((xl + pos_ref[l:l + 1, :]).astype(BF16), w1_ref[l])
            bm = bm + _dot((xl + pos_ref[half_blk + l:half_blk + l + 1, :]).astype(BF16),
                           w1_ref[half_blk + l])
        shift_ref[pl.ds(0, ncp), :] = bm
        shift_ref[pl.ds(ncp, 8), :] = jnp.zeros((8, HEAD_DIM), F32)
        hid = _silu(a + shift_ref[pl.ds(1, ncp), :])
        out = _dot(hid.astype(BF16), w2_ref[...])
        row = lax.broadcasted_iota(jnp.int32, out.shape, 0)
        return jnp.where(row < ncp - 1, out, 0.0)

    kT_ref[0, 0] = comp(kc_ref, kpos_ref, kw1_ref, kw2_ref).T.astype(BF16)
    v_ref[0, 0] = comp(vc_ref, vpos_ref, vw1_ref, vw2_ref).astype(BF16)


def _compress(pf, B, S, kc_blk, vc_blk, layer, ck_pos, ck_w1, ck_w2, cv_pos, cv_w1, cv_w2):
    G = NSA_KV_GROUPS
    ncp = S // CMP_STRIDE
    full2 = lambda b, g: (layer, 0, 0)
    full3 = lambda b, g: (layer, 0, 0, 0)
    return pl.pallas_call(
        functools.partial(_compress_kernel, ncp=ncp),
        grid=(B, G),
        in_specs=[pl.BlockSpec((S, HEAD_DIM), lambda b, g: (b, kc_blk + g)),
                  pl.BlockSpec((S, HEAD_DIM), lambda b, g: (b, vc_blk + g)),
                  pl.BlockSpec((None, CMP_BLOCK, HEAD_DIM), full2),
                  pl.BlockSpec((None, CMP_BLOCK, HEAD_DIM, HEAD_DIM), full3),
                  pl.BlockSpec((None, HEAD_DIM, HEAD_DIM), full2),
                  pl.BlockSpec((None, CMP_BLOCK, HEAD_DIM), full2),
                  pl.BlockSpec((None, CMP_BLOCK, HEAD_DIM, HEAD_DIM), full3),
                  pl.BlockSpec((None, HEAD_DIM, HEAD_DIM), full2)],
        out_specs=[pl.BlockSpec((1, 1, HEAD_DIM, ncp), lambda b, g: (b, g, 0, 0)),
                   pl.BlockSpec((1, 1, ncp, HEAD_DIM), lambda b, g: (b, g, 0, 0))],
        out_shape=[jax.ShapeDtypeStruct((B, G, HEAD_DIM, ncp), BF16),
                   jax.ShapeDtypeStruct((B, G, ncp, HEAD_DIM), BF16)],
        scratch_shapes=[pltpu.VMEM((ncp + 8, HEAD_DIM), F32)],
        compiler_params=_cparams("parallel", "parallel"),
        name="nsa_compress",
    )(pf, pf, ck_pos, ck_w1, ck_w2, cv_pos, cv_w1, cv_w2)


def _cmp_topk_kernel(q_ref, kT_ref, v_ref, o_ref, bias_ref, imp_ref, *, tq, ncp):
    i = pl.program_id(2)
    s0 = i * tq
    t_col = s0 + lax.broadcasted_iota(jnp.int32, (tq, 1), 0)
    any_visible = (t_col >= CMP_BLOCK - 1).astype(F32)

    def attend(nb):
        nc = nb * LANES
        n_row = lax.broadcasted_iota(jnp.int32, (1, nc), 1)
        mc = jnp.logical_and(n_row * CMP_STRIDE + (CMP_BLOCK - 1) <= t_col, n_row < ncp - 1)
        kT = kT_ref[0, 0, :, 0:nc]
        v = v_ref[0, 0, 0:nc, :]
        psum = jnp.zeros((tq, nc), F32)
        for r in range(NSA_REP):
            sl = slice(r * HEAD_DIM, (r + 1) * HEAD_DIM)
            s = jnp.where(mc, _dot(q_ref[:, sl], kT), NEG)
            m = jnp.max(s, axis=-1, keepdims=True)
            e = jnp.exp2(s - m)
            l = jnp.sum(e, axis=-1, keepdims=True)
            pc = e * (any_visible / l)
            o_ref[:, sl] = _dot(pc.astype(BF16), v)
            psum = psum + pc
        n_col = lax.broadcasted_iota(jnp.int32, (nc, LANES), 0) * CMP_STRIDE
        k_row = lax.broadcasted_iota(jnp.int32, (nc, LANES), 1) * SEL_BLOCK
        ovl = jnp.logical_and(n_col < k_row + SEL_BLOCK, n_col + CMP_BLOCK > k_row).astype(BF16)
        hi = psum.astype(BF16)
        lo = (psum - hi.astype(F32)).astype(BF16)
        imp_ref[...] = _dot(hi, ovl) + _dot(lo, ovl)

    def topk(nvb):
        blk = lax.broadcasted_iota(jnp.int32, (nvb, LANES), 0)
        lane = lax.broadcasted_iota(jnp.int32, (nvb, LANES), 1)
        groups = []
        for c in range(0, tq, LANES):
            impT = imp_ref[c:c + LANES, :].T[0:nvb, :]
            cur = (s0 + c + lane) // SEL_BLOCK
            forced = jnp.logical_or(blk == 0, jnp.logical_or(blk == cur, blk == cur - 1))
            groups.append(jnp.where(forced, BELOW_NEG, jnp.where(blk <= cur, impT, NEG)))
        for _ in range(SEL_TOPK - 3):
            nxt = []
            for sc in groups:
                m = jnp.max(sc, axis=0, keepdims=True)
                idx = jnp.min(jnp.where(sc == m, blk, LANES), axis=0, keepdims=True)
                nxt.append(jnp.where(blk == idx, BELOW_NEG, sc))
            groups = nxt
        for g, sc in enumerate(groups):
            biasT = jnp.where(sc == BELOW_NEG, 0.0, NEG)
            if nvb < LANES:
                biasT = jnp.concatenate([biasT, jnp.full((LANES - nvb, LANES), NEG, F32)], axis=0)
            bias_ref[g * LANES:(g + 1) * LANES, :] = biasT.T.astype(BF16)

    nb_total = ncp // LANES
    sel_per_nb = LANES * CMP_STRIDE // SEL_BLOCK
    last_visible = (s0 + tq - CMP_BLOCK) // CMP_STRIDE
    nb_needed = jnp.minimum(last_visible // LANES + 1, nb_total)
    for nb in range(1, nb_total + 1):
        @pl.when(nb_needed == nb)
        def _(nb=nb):
            attend(nb)
            topk(min(LANES, nb * sel_per_nb))


def _cmp_topk(pb, kT, vcmp, B, S):
    G = NSA_KV_GROUPS
    T = B * S
    tq = min(S, CMP_TQ)
    nq = S // tq
    ncp = S // CMP_STRIDE
    gw = NSA_REP * HEAD_DIM
    return pl.pallas_call(
        functools.partial(_cmp_topk_kernel, tq=tq, ncp=ncp),
        grid=(B, G, nq),
        in_specs=[pl.BlockSpec((tq, gw), lambda b, g, i: (b * nq + i, g)),
                  pl.BlockSpec((1, 1, HEAD_DIM, ncp), lambda b, g, i: (b, g, 0, 0)),
                  pl.BlockSpec((1, 1, ncp, HEAD_DIM), lambda b, g, i: (b, g, 0, 0))],
        out_specs=[pl.BlockSpec((tq, gw), lambda b, g, i: (b * nq + i, g)),
                   pl.BlockSpec((tq, LANES), lambda b, g, i: (b * nq + i, g))],
        out_shape=[jax.ShapeDtypeStruct((T, NSA_HEADS * HEAD_DIM), F32),
                   jax.ShapeDtypeStruct((T, G * LANES), BF16)],
        scratch_shapes=[pltpu.VMEM((tq, LANES), F32)],
        compiler_params=_cparams("parallel", "parallel", "parallel"),
        name="nsa_cmp_topk",
    )(pb, kT, vcmp)


def _win_kernel(q_ref, kp_ref, kc_ref, vp_ref, vc_ref, o_ref, *, sub):
    i = pl.program_id(2)
    row = lax.broadcasted_iota(jnp.int32, (sub, sub), 0)
    col = lax.broadcasted_iota(jnp.int32, (sub, sub), 1)
    ones = jnp.ones((sub, LANES), BF16)

    def body(first):
        ks = [kp_ref[0:sub, :], kp_ref[sub:2 * sub, :], kc_ref[0:sub, :], kc_ref[sub:2 * sub, :]]
        vs = [jnp.concatenate([v, ones], axis=1) for v in
              (vp_ref[0:sub, :], vp_ref[sub:2 * sub, :], vc_ref[0:sub, :], vc_ref[sub:2 * sub, :])]
        for a in range(2):
            blocks = [(a, col > row), (a + 1, None), (a + 2, col <= row)]
            if first:
                blocks = blocks[2 - a:]
            rows = slice(a * sub, (a + 1) * sub)
            for r in range(NSA_REP):
                sl = slice(r * HEAD_DIM, (r + 1) * HEAD_DIM)
                q = q_ref[rows, sl]
                ss = []
                for kb, mask in blocks:
                    s = _dot_nt(q, ks[kb])
                    ss.append(s if mask is None else jnp.where(mask, s, NEG))
                m = functools.reduce(jnp.maximum, ss)
                m = jnp.max(m, axis=-1, keepdims=True)
                acc = None
                for s, (kb, _) in zip(ss, blocks):
                    part = _dot(jnp.exp2(s - m).astype(BF16), vs[kb])
                    acc = part if acc is None else acc + part
                o_ref[rows, sl] = acc[:, 0:HEAD_DIM] / acc[:, HEAD_DIM:2 * HEAD_DIM]

    pl.when(i > 0)(functools.partial(body, False))
    pl.when(i == 0)(functools.partial(body, True))


def _window(pb, B, S, k_blk, v_blk):
    G = NSA_KV_GROUPS
    T = B * S
    sub = WINDOW // 2
    tq = 2 * sub
    nq = S // tq
    gw = NSA_REP * HEAD_DIM

    def kv_spec(col0, back):
        return pl.BlockSpec((tq, HEAD_DIM),
                            lambda b, g, i: (b * nq + jnp.maximum(i - back, 0), col0 + g))

    return pl.pallas_call(
        functools.partial(_win_kernel, sub=sub),
        grid=(B, G, nq),
        in_specs=[pl.BlockSpec((tq, gw), lambda b, g, i: (b * nq + i, g)),
                  kv_spec(k_blk, 1), kv_spec(k_blk, 0), kv_spec(v_blk, 1), kv_spec(v_blk, 0)],
        out_specs=pl.BlockSpec((tq, gw), lambda b, g, i: (b * nq + i, g)),
        out_shape=jax.ShapeDtypeStruct((T, NSA_HEADS * HEAD_DIM), F32),
        compiler_params=_cparams("parallel", "parallel", "parallel"),
        name="nsa_window",
    )(pb, pb, pb, pb, pb)


def _sel_kernel(qi_ref, kj_ref, q_ref, bias_ref, k_ref, v_ref, o_ref, qa_ref, m_ref, acc_ref, *, tq, tk):
    n = pl.program_id(2)
    i = qi_ref[n]
    j = kj_ref[n]
    last_j = (i * tq + tq - 1) // tk

    @pl.when(j == 0)
    def _():
        for r in range(NSA_REP):
            rows = slice(r * tq, (r + 1) * tq)
            qa_ref[rows, 0:HEAD_DIM] = q_ref[:, r * HEAD_DIM:(r + 1) * HEAD_DIM]
            qa_ref[rows, HEAD_DIM:2 * HEAD_DIM] = bias_ref[...]
        m_ref[...] = jnp.full(m_ref.shape, NEG, F32)
        acc_ref[...] = jnp.zeros(acc_ref.shape, F32)

    def step(col0, ncols, diagonal):
        key = j * tk + col0 + lax.broadcasted_iota(jnp.int32, (ncols, LANES), 0)
        blk = lax.broadcasted_iota(jnp.int32, (ncols, LANES), 1)
        onehot = (key // SEL_BLOCK == blk).astype(BF16)
        ka = jnp.concatenate([k_ref[col0:col0 + ncols, :], onehot], axis=1)
        va = jnp.concatenate([v_ref[col0:col0 + ncols, :], jnp.ones((ncols, LANES), BF16)], axis=1)
        if diagonal:
            visible = (lax.broadcasted_iota(jnp.int32, (tq, tq), 1)
                       <= lax.broadcasted_iota(jnp.int32, (tq, tq), 0))
        def scores(r):
            s = _dot_nt(qa_ref[r * tq:(r + 1) * tq, :], ka)
            return jnp.where(visible, s, NEG) if diagonal else s

        s_next = scores(0)
        for r in range(NSA_REP):
            rows = slice(r * tq, (r + 1) * tq)
            s = s_next
            if r + 1 < NSA_REP:
                s_next = scores(r + 1)
            m_prev = m_ref[rows, :]
            m_new = jnp.maximum(m_prev, jnp.max(s, axis=-1, keepdims=True))
            alpha = jnp.exp2(m_prev - m_new)
            p = jnp.exp2(s - jnp.tile(m_new, (1, ncols // LANES)))
            acc_ref[rows, :] = jnp.tile(alpha, (1, 2)) * acc_ref[rows, :] + _dot(p.astype(BF16), va)
            m_ref[rows, :] = m_new

    @pl.when(j < last_j)
    def _():
        step(0, tk, False)

    for sub in range(tk // tq):
        @pl.when(jnp.logical_and(j == last_j, i % (tk // tq) == sub))
        def _(sub=sub):
            if sub > 0:
                step(0, sub * tq, False)
            step(sub * tq, tq, True)

    @pl.when(j == last_j)
    def _():
        for r in range(NSA_REP):
            rows = slice(r * tq, (r + 1) * tq)
            o_ref[:, r * HEAD_DIM:(r + 1) * HEAD_DIM] = (acc_ref[rows, 0:HEAD_DIM]
                                                         / acc_ref[rows, HEAD_DIM:2 * HEAD_DIM])


def _selection(pb, bias, B, S, k_blk, v_blk):
    G = NSA_KV_GROUPS
    T = B * S
    tq = min(S, SEL_TQ)
    tk = min(S, SEL_TK)
    assert tk % tq == 0
    nq = S // tq
    nk = S // tk
    gw = NSA_REP * HEAD_DIM
    pairs = [(i, j) for i in range(nq) for j in range((i * tq + tq - 1) // tk + 1)]
    qi = jnp.asarray([p[0] for p in pairs], jnp.int32)
    kj = jnp.asarray([p[1] for p in pairs], jnp.int32)

    def q_map(b, g, n, qi_ref, kj_ref):
        return (b * nq + qi_ref[n], g)

    def kv_spec(col0):
        return pl.BlockSpec((tk, HEAD_DIM),
                            lambda b, g, n, qi_ref, kj_ref: (b * nk + kj_ref[n], col0 + g))

    grid_spec = pltpu.PrefetchScalarGridSpec(
        num_scalar_prefetch=2,
        grid=(B, G, len(pairs)),
        in_specs=[pl.BlockSpec((tq, gw), q_map), pl.BlockSpec((tq, LANES), q_map),
                  kv_spec(k_blk), kv_spec(v_blk)],
        out_specs=pl.BlockSpec((tq, gw), q_map),
        scratch_shapes=[pltpu.VMEM((NSA_REP * tq, 2 * HEAD_DIM), BF16),
                        pltpu.VMEM((NSA_REP * tq, LANES), F32),
                        pltpu.VMEM((NSA_REP * tq, 2 * HEAD_DIM), F32)])
    return pl.pallas_call(
        functools.partial(_sel_kernel, tq=tq, tk=tk),
        grid_spec=grid_spec,
        out_shape=jax.ShapeDtypeStruct((T, NSA_HEADS * HEAD_DIM), F32),
        compiler_params=_cparams("parallel", "parallel", "arbitrary"),
        name="nsa_selection",
    )(qi, kj, pb, bias, pb, pb)


def _nsa_out_kernel(x_ref, oc_ref, os_ref, ow_ref, z_ref, gl_ref, w_ref, nw_ref, o_ref, hn_ref):
    sig = jax.nn.sigmoid(gl_ref[...])
    H = NSA_HEADS
    acc = x_ref[...]
    for hp in range(H // 2):
        parts = []
        for h in (2 * hp, 2 * hp + 1):
            sl = slice(h * HEAD_DIM, (h + 1) * HEAD_DIM)
            o = (sig[:, h:h + 1] * oc_ref[:, sl] + sig[:, H + h:H + h + 1] * os_ref[:, sl]
                 + sig[:, 2 * H + h:2 * H + h + 1] * ow_ref[:, sl])
            parts.append((o * _silu(z_ref[:, sl])).astype(BF16))
        rows = slice(2 * hp * HEAD_DIM, (2 * hp + 2) * HEAD_DIM)
        acc = acc + _dot(jnp.concatenate(parts, axis=1), w_ref[rows, :])
    o_ref[...] = acc
    hn_ref[...] = _rms_bf16(acc, nw_ref[...])


def _nsa_out(x, o_cmp, o_sel, o_win, pf, z_blk, gl_blk, w_out, layer, next_norm_w):
    T, D = x.shape
    W = o_cmp.shape[1]
    tm = min(T, OUT_TM)
    row = lambda i: (i, 0)
    return pl.pallas_call(
        _nsa_out_kernel,
        grid=(T // tm,),
        in_specs=[pl.BlockSpec((tm, D), row),
                  pl.BlockSpec((tm, W), row), pl.BlockSpec((tm, W), row), pl.BlockSpec((tm, W), row),
                  pl.BlockSpec((tm, W), lambda i: (i, z_blk)),
                  pl.BlockSpec((tm, LANES), lambda i: (i, gl_blk)),
                  pl.BlockSpec((None, W, D), lambda i: (layer, 0, 0)),
                  pl.BlockSpec((1, D), lambda i: (0, 0))],
        out_specs=[pl.BlockSpec((tm, D), row), pl.BlockSpec((tm, D), row)],
        out_shape=[jax.ShapeDtypeStruct((T, D), F32), jax.ShapeDtypeStruct((T, D), BF16)],
        compiler_params=_cparams("parallel"),
        name="nsa_out",
    )(x, o_cmp, o_sel, o_win, pf, pf, w_out, next_norm_w.reshape(1, D))


def _prepare_params(nsa_w_in, nsa_ck_pos, nsa_ck_w1, nsa_ck_w2, nsa_cv_pos, nsa_cv_w1, nsa_cv_w2,
                    nsa_w_out, hgrn_w_in, hgrn_w_out):
    H, G, d = NSA_HEADS, NSA_KV_GROUPS, HEAD_DIM
    kvw = G * d
    c = [0, H * d]
    for _ in range(6):
        c.append(c[-1] + kvw)
    c.append(c[-1] + 3 * H)
    c.append(c[-1] + H * d)
    seg = lambda n: nsa_w_in[:, :, c[n]:c[n + 1]]
    wq, wkc, wvc, wks, wvs, wkw, wvw, wgl, wz = [seg(n) for n in range(9)]
    pad = jnp.zeros(nsa_w_in.shape[:2] + (LANES - 3 * H,), nsa_w_in.dtype)
    return dict(
        nsa_wb=jnp.concatenate([wq, wks, wkw, wvs, wvw], axis=2).astype(BF16),
        nsa_wf=jnp.concatenate([wz, wkc, wvc, wgl, pad], axis=2).astype(BF16),
        ck_pos=nsa_ck_pos, ck_w1=nsa_ck_w1.astype(BF16), ck_w2=nsa_ck_w2.astype(BF16),
        cv_pos=nsa_cv_pos, cv_w1=nsa_cv_w1.astype(BF16), cv_w2=nsa_cv_w2.astype(BF16),
        nsa_wo=nsa_w_out.astype(BF16),
        hgrn_wi=hgrn_w_in, hgrn_wo=hgrn_w_out.astype(BF16))


def _nsa_layer(x, hn, B, S, tabs, prm, j, next_norm_w):
    H, G = NSA_HEADS, NSA_KV_GROUPS
    pb = _proj(hn, prm["nsa_wb"], j, BF16, 1024, ['rope_q'] * H + ['rope'] * (2 * G) + ['plain'] * (2 * G),
               tabs=tabs)
    pf = _proj(hn, prm["nsa_wf"], j, F32, 640, ['plain'] * H + ['rope'] * G + ['plain'] * (G + 1), tabs=tabs)
    ks_blk, kw_blk, vs_blk, vw_blk = H, H + G, H + 2 * G, H + 3 * G
    kc_blk, vc_blk, gl_blk = H, H + G, H + 2 * G

    kT, vcmp = _compress(pf, B, S, kc_blk, vc_blk, j, prm["ck_pos"], prm["ck_w1"], prm["ck_w2"],
                         prm["cv_pos"], prm["cv_w1"], prm["cv_w2"])
    o_cmp, bias = _cmp_topk(pb, kT, vcmp, B, S)
    o_win = _window(pb, B, S, kw_blk, vw_blk)
    o_sel = _selection(pb, bias, B, S, ks_blk, vs_blk)
    return _nsa_out(x, o_cmp, o_sel, o_win, pf, 0, gl_blk, prm["nsa_wo"], j, next_norm_w)


def _hgrn_kernel(q_ref, g_ref, i_ref, z_ref, gw_ref, o_ref, st_ref, *, C, sub, hb):
    n = pl.program_id(2)

    @pl.when(n == 0)
    def _():
        st_ref[...] = jnp.zeros(st_ref.shape, F32)

    r0 = lax.broadcasted_iota(jnp.int32, (C, C), 0)
    r1 = lax.broadcasted_iota(jnp.int32, (C, C), 1)
    tri = (r0 >= r1).astype(BF16)
    dsub = sub // 2
    drow = lax.broadcasted_iota(jnp.int32, (C // dsub, dsub, HEAD_DIM), 1)
    row_id = lax.broadcasted_iota(jnp.int32, (C, HEAD_DIM), 0)

    for h in range(hb):
        cs = slice(h * HEAD_DIM, (h + 1) * HEAD_DIM)
        g = g_ref[:, cs]
        k = 1.0 - jnp.exp(g)
        qs = q_ref[:, cs]
        iv = i_ref[:, cs]
        ib = iv.astype(BF16)

        g1 = g.astype(BF16)
        rem = g - g1.astype(F32)
        g2 = rem.astype(BF16)
        g3 = (rem - g2.astype(F32)).astype(BF16)
        b = _dot(tri, g1) + _dot(tri, g2) + _dot(tri, g3)

        stT = st_ref[h]
        o_inter = _dot_nt((qs * jnp.exp(b)).astype(BF16), stT.astype(BF16))

        nd = C // dsub
        b3 = b.reshape(nd, dsub, HEAD_DIM)
        q3 = qs.reshape(nd, dsub, HEAD_DIM)
        k3 = k.reshape(nd, dsub, HEAD_DIM)
        i3 = iv.reshape(nd, dsub, HEAD_DIM)
        o3 = jnp.zeros((nd, dsub, HEAD_DIM), F32)
        for s in range(dsub):
            w = jnp.where(drow >= s, jnp.exp(b3 - b3[:, s:s + 1, :]), 0.0)
            col = jnp.sum(q3 * w * k3[:, s:s + 1, :], axis=-1, keepdims=True)
            o3 = o3 + col * i3[:, s:s + 1, :]

        att = None
        bsz = 2 * dsub
        while bsz <= C:
            bl = b.reshape(C // bsz, bsz, HEAD_DIM)
            bmid = jnp.broadcast_to(bl[:, bsz // 2 - 1:bsz // 2, :], bl.shape).reshape(C, HEAD_DIM)
            second = (row_id % bsz) >= bsz // 2
            qk = jnp.where(second, qs, k) * jnp.exp(-jnp.abs(b - bmid))
            qh = jnp.where(second, qk, 0.0).astype(BF16)
            kh = jnp.where(second, 0.0, qk).astype(BF16)
            a = _dot_nt(qh, kh)
            if bsz < C:
                a = jnp.where(r0 // bsz == r1 // bsz, a, 0.0)
            att = a if att is None else att + a
            bsz *= 2
        o = o_inter + o3.reshape(C, HEAD_DIM) + _dot(att.astype(BF16), ib)

        b_last = b[C - 1:C, :]
        kd = (k * jnp.exp(b_last - b)).astype(BF16)
        st_ref[h] = stT * jnp.exp(b_last) + _dot(iv.T.astype(BF16), kd)

        o = o * lax.rsqrt(jnp.mean(o * o, axis=-1, keepdims=True) + RMS_EPS) * gw_ref[...]
        o_ref[:, cs] = (o * z_ref[:, cs]).astype(BF16)


def _hgrn_recurrence(ph, gnorm_w, B, S):
    H = HGRN_HEADS
    T = B * S
    C = min(S, HGRN_CHUNK)
    nc = S // C
    hb = HGRN_HEADS_PER_STEP
    hg = H // hb
    w = hb * HEAD_DIM

    def col(c0):
        return pl.BlockSpec((C, w), lambda b, h, n: (b * nc + n, c0 + h))

    return pl.pallas_call(
        functools.partial(_hgrn_kernel, C=C, sub=HGRN_SUB, hb=hb),
        grid=(B, hg, nc),
        in_specs=[col(0), col(hg), col(2 * hg), col(3 * hg),
                  pl.BlockSpec((1, HEAD_DIM), lambda b, h, n: (0, 0))],
        out_specs=pl.BlockSpec((C, w), lambda b, h, n: (b * nc + n, h)),
        out_shape=jax.ShapeDtypeStruct((T, H * HGRN_I_DIM), BF16),
        scratch_shapes=[pltpu.VMEM((hb, HGRN_I_DIM, HGRN_F_DIM), F32)],
        compiler_params=_cparams("parallel", "parallel", "arbitrary"),
        name="hgrn_recurrence",
    )(ph, ph, ph, ph, gnorm_w.reshape(1, HEAD_DIM))


def _hgrn_out_kernel(x_ref, og_ref, w_ref, nw_ref, *out_refs, final_norm):
    y = x_ref[...] + _dot(og_ref[...], w_ref[...])
    if final_norm:
        ms = jnp.mean(y * y, axis=-1, keepdims=True)
        out_refs[0][...] = y * lax.rsqrt(ms + RMS_EPS) * nw_ref[...]
    else:
        out_refs[0][...] = y
        out_refs[1][...] = _rms_bf16(y, nw_ref[...])


def _hgrn_out(x, og, w_out, layer, norm_w, final_norm):
    T, D = x.shape
    W = og.shape[1]
    tm = min(T, 2 * OUT_TM)
    row = lambda i: (i, 0)
    out_specs = [pl.BlockSpec((tm, D), row)]
    out_shape = [jax.ShapeDtypeStruct((T, D), F32)]
    if not final_norm:
        out_specs.append(pl.BlockSpec((tm, D), row))
        out_shape.append(jax.ShapeDtypeStruct((T, D), BF16))
    return pl.pallas_call(
        functools.partial(_hgrn_out_kernel, final_norm=final_norm),
        grid=(T // tm,),
        in_specs=[pl.BlockSpec((tm, D), row), pl.BlockSpec((tm, W), row),
                  pl.BlockSpec((None, W, D), lambda i: (layer, 0, 0)),
                  pl.BlockSpec((1, D), lambda i: (0, 0))],
        out_specs=out_specs,
        out_shape=out_shape,
        compiler_params=_cparams("parallel"),
        name="hgrn_out",
    )(x, og, w_out, norm_w.reshape(1, D))


def _hgrn_layer(x, hn, B, S, prm, lb_logits, gnorm_w, j, norm_w, final_norm):
    H = HGRN_HEADS
    kinds = ['silu'] * H + [('logf', h) for h in range(H)] + ['plain'] * H + ['silu'] * H
    ph = _proj(hn, prm["hgrn_wi"], j, F32, 512, kinds, lb_logits=lb_logits)
    og = _hgrn_recurrence(ph, gnorm_w, B, S)
    return _hgrn_out(x, og, prm["hgrn_wo"], j, norm_w, final_norm)


def kernel(x, positions, norm_w, final_norm_w, nsa_w_in, nsa_ck_pos, nsa_ck_w1, nsa_ck_w2,
           nsa_cv_pos, nsa_cv_w1, nsa_cv_w2, nsa_w_out, hgrn_w_in, hgrn_lb_logits,
           hgrn_gnorm_w, hgrn_w_out):
    B, S, D = x.shape
    depth = norm_w.shape[0]
    assert depth % 2 == 0
    xf = x.reshape(B * S, D)
    tabs = _rope_tables(positions)
    prm = _prepare_params(nsa_w_in, nsa_ck_pos, nsa_ck_w1, nsa_ck_w2, nsa_cv_pos, nsa_cv_w1,
                          nsa_cv_w2, nsa_w_out, hgrn_w_in, hgrn_w_out)
    hn = _rmsnorm(xf, norm_w[0])
    for layer in range(depth):
        j = layer // 2
        last = layer == depth - 1
        if layer % 2 == 0:
            xf, hn = _nsa_layer(xf, hn, B, S, tabs, prm, j, norm_w[layer + 1])
        elif last:
            (xf,) = _hgrn_layer(xf, hn, B, S, prm, hgrn_lb_logits, hgrn_gnorm_w[j], j, final_norm_w, True)
        else:
            xf, hn = _hgrn_layer(xf, hn, B, S, prm, hgrn_lb_logits, hgrn_gnorm_w[j], j,
                                 norm_w[layer + 1], False)
    return xf.reshape(B, S, D)
```

```python
import functools
import math

import jax
import jax.numpy as jnp
from jax import lax
from jax.experimental import pallas as pl
from jax.experimental.pallas import tpu as pltpu

F32 = jnp.float32
BF16 = jnp.bfloat16

RMS_EPS = 1e-6
NEG = -1e30
BELOW_NEG = -3e38
HEAD_DIM = 128
NSA_HEADS = 16
NSA_KV_GROUPS = 4
NSA_REP = NSA_HEADS // NSA_KV_GROUPS
CMP_BLOCK = 32
CMP_STRIDE = 16
SEL_BLOCK = 64
SEL_TOPK = 16
WINDOW = 512
ROPE_THETA = 500000.0
ROPE_DIM = HEAD_DIM // 4
HGRN_HEADS = 16
HGRN_F_DIM = 128
HGRN_I_DIM = 128

LANES = 128
VMEM_LIMIT_BYTES = 56 * 1024 * 1024

ATTN_SCALE = HEAD_DIM ** -0.5
EXP2_SCALE = ATTN_SCALE * math.log2(math.e)

PROJ_TM = 2048
PROJ_CHUNK_HEADS = 2
CMP_TQ = 1024
SEL_TQ = 512
SEL_TK = 2048
OUT_TM = 256
HGRN_CHUNK = 128
HGRN_SUB = 16
HGRN_HEADS_PER_STEP = 16


def _cparams(*sem):
    return pltpu.CompilerParams(dimension_semantics=sem, vmem_limit_bytes=VMEM_LIMIT_BYTES)


def _dot(a, b):
    return jnp.dot(a, b, preferred_element_type=F32)


def _dot_nt(a, b):
    return lax.dot_general(a, b, (((1,), (1,)), ((), ())), preferred_element_type=F32)


def _silu(v):
    return v * jax.nn.sigmoid(v)


_HALF = ROPE_DIM // 2


def _rope_table_kernel(pos_ref, inv_ref, cos_ref, sin_lo_ref, sin_hi_ref):
    ang = pos_ref[...] * inv_ref[...]
    lane = lax.broadcasted_iota(jnp.int32, ang.shape, 1)
    s = jnp.sin(ang)
    cos_ref[...] = jnp.cos(ang)
    sin_lo_ref[...] = jnp.where(lane < _HALF, -s, 0.0)
    sin_hi_ref[...] = jnp.where(lane < _HALF, 0.0, s)


def _rope_tables(positions):
    T = positions.size
    tm = min(T, PROJ_TM)
    inv = ROPE_THETA ** (-jnp.arange(_HALF, dtype=F32) / _HALF)
    inv_pat = jnp.concatenate([inv, inv, jnp.zeros((LANES - ROPE_DIM,), F32)]).reshape(1, LANES)
    pos = positions.astype(F32).reshape(T, 1)
    return pl.pallas_call(
        _rope_table_kernel,
        grid=(T // tm,),
        in_specs=[pl.BlockSpec((tm, 1), lambda i: (i, 0)),
                  pl.BlockSpec((1, LANES), lambda i: (0, 0))],
        out_specs=[pl.BlockSpec((tm, LANES), lambda i: (i, 0))] * 3,
        out_shape=[jax.ShapeDtypeStruct((T, LANES), F32)] * 3,
        compiler_params=_cparams("parallel"),
        name="rope_tables",
    )(pos, inv_pat)


def _rms_bf16(y, w):
    ms = jnp.mean(y * y, axis=-1, keepdims=True)
    return (y * lax.rsqrt(ms + RMS_EPS) * w).astype(BF16)


def _rmsnorm_kernel(x_ref, w_ref, o_ref):
    o_ref[...] = _rms_bf16(x_ref[...], w_ref[...])


def _rmsnorm(x, w):
    T, D = x.shape
    tm = min(T, 2 * OUT_TM)
    return pl.pallas_call(
        _rmsnorm_kernel,
        grid=(T // tm,),
        in_specs=[pl.BlockSpec((tm, D), lambda i: (i, 0)), pl.BlockSpec((1, D), lambda i: (0, 0))],
        out_specs=pl.BlockSpec((tm, D), lambda i: (i, 0)),
        out_shape=jax.ShapeDtypeStruct((T, D), BF16),
        compiler_params=_cparams("parallel"),
        name="rmsnorm",
    )(x, w.reshape(1, D))


def _log_forget(t, lbl, layer):
    e = jnp.exp(lbl - jnp.max(lbl, axis=0, keepdims=True))
    p = e / jnp.sum(e, axis=0, keepdims=True)
    lb = jnp.sum(p[0:layer + 1, :], axis=0, keepdims=True) - p[0:1, :]
    u = jnp.exp(-jnp.abs(t))
    num = jnp.where(t >= 0.0, 1.0 + lb * u, lb + u)
    return jnp.maximum(jnp.log(num), jnp.minimum(t, 0.0)) - jnp.log(1.0 + u)


def _proj_kernel(*refs, kinds_of_tile, has_rope, layer):
    if has_rope:
        hn_ref, w_ref, cos_ref, sin_lo_ref, sin_hi_ref, o_ref = refs
    elif any(isinstance(k, tuple) for kinds in kinds_of_tile for k in kinds):
        hn_ref, w_ref, lbl_ref, o_ref = refs
    else:
        hn_ref, w_ref, o_ref = refs
    j = pl.program_id(1)
    heads_per_tile = len(kinds_of_tile[0])

    def tile_body(kinds):
        h = 0
        while h < heads_per_tile:
            nh = min(PROJ_CHUNK_HEADS, heads_per_tile - h)
            acc = _dot(hn_ref[...], w_ref[:, h * HEAD_DIM:(h + nh) * HEAD_DIM].astype(BF16))
            for hh in range(nh):
                sl = slice((h + hh) * HEAD_DIM, (h + hh + 1) * HEAD_DIM)
                t = acc[:, hh * HEAD_DIM:(hh + 1) * HEAD_DIM]
                kind = kinds[h + hh]
                if kind in ('rope', 'rope_q'):
                    t = (t * cos_ref[...] + pltpu.roll(t, HEAD_DIM - _HALF, axis=1) * sin_lo_ref[...]
                         + pltpu.roll(t, _HALF, axis=1) * sin_hi_ref[...])
                if kind == 'rope_q':
                    t = t * EXP2_SCALE
                elif kind == 'silu':
                    t = _silu(t)
                elif isinstance(kind, tuple):
                    t = _log_forget(t, lbl_ref[:, kind[1] * HEAD_DIM:(kind[1] + 1) * HEAD_DIM], layer)
                o_ref[:, sl] = t.astype(o_ref.dtype)
            h += nh

    patterns = {}
    for jj, kinds in enumerate(kinds_of_tile):
        patterns.setdefault(kinds, []).append(jj)
    for kinds, tiles in patterns.items():
        cond = functools.reduce(jnp.logical_or, [j == jj for jj in tiles])
        pl.when(cond)(functools.partial(tile_body, kinds))


def _proj(hn, w, layer, out_dtype, tn, kinds, tabs=None, lb_logits=None):
    T, D = hn.shape
    N = w.shape[2]
    tm = min(T, PROJ_TM)
    hpt = tn // HEAD_DIM
    kinds_of_tile = tuple(tuple(kinds[jj * hpt:(jj + 1) * hpt]) for jj in range(N // tn))
    in_specs = [pl.BlockSpec((tm, D), lambda i, j: (i, 0)),
                pl.BlockSpec((None, D, tn), lambda i, j: (layer, 0, j))]
    args = [hn, w]
    if tabs is not None:
        in_specs += [pl.BlockSpec((tm, LANES), lambda i, j: (i, 0))] * len(tabs)
        args += list(tabs)
    if lb_logits is not None:
        in_specs.append(pl.BlockSpec(lb_logits.shape, lambda i, j: (0, 0)))
        args.append(lb_logits)
    return pl.pallas_call(
        functools.partial(_proj_kernel, kinds_of_tile=kinds_of_tile, has_rope=tabs is not None, layer=layer),
        grid=(T // tm, N // tn),
        in_specs=in_specs,
        out_specs=pl.BlockSpec((tm, tn), lambda i, j: (i, j)),
        out_shape=jax.ShapeDtypeStruct((T, N), out_dtype),
        compiler_params=_cparams("parallel", "arbitrary"),
        name="in_proj",
    )(*args)


def _compress_kernel(kc_ref, vc_ref, kpos_ref, kw1_ref, kw2_ref, vpos_ref, vw1_ref, vw2_ref,
                     kT_ref, v_ref, shift_ref, *, ncp):
    half_blk = CMP_BLOCK // 2

    def comp(t_ref, pos_ref, w1_ref, w2_ref):
        a = jnp.zeros((ncp, HEAD_DIM), F32)
        bm = jnp.zeros((ncp, HEAD_DIM), F32)
        for l in range(half_blk):
            xl = t_ref[pl.ds(l, ncp, stride=CMP_STRIDE), :]
            a = a + _dot((xl + pos_ref[l:l + 1, :]).astype(BF16), w1_ref[l])
            bm = bm + _dot((xl + pos_ref[half_blk + l:half_blk + l + 1, :]).astype(BF16),
                           w1_ref[half_blk + l])
        shift_ref[pl.ds(0, ncp), :] = bm
        shift_ref[pl.ds(ncp, 8), :] = jnp.zeros((8, HEAD_DIM), F32)
        hid = _silu(a + shift_ref[pl.ds(1, ncp), :])
        out = _dot(hid.astype(BF16), w2_ref[...])
        row = lax.broadcasted_iota(jnp.int32, out.shape, 0)
        return jnp.where(row < ncp - 1, out, 0.0)

    kT_ref[0, 0] = comp(kc_ref, kpos_ref, kw1_ref, kw2_ref).T.astype(BF16)
    v_ref[0, 0] = comp(vc_ref, vpos_ref, vw1_ref, vw2_ref).astype(BF16)


def _compress(pf, B, S, kc_blk, vc_blk, layer, ck_pos, ck_w1, ck_w2, cv_pos, cv_w1, cv_w2):
    G = NSA_KV_GROUPS
    ncp = S // CMP_STRIDE
    full2 = lambda b, g: (layer, 0, 0)
    full3 = lambda b, g: (layer, 0, 0, 0)
    return pl.pallas_call(
        functools.partial(_compress_kernel, ncp=ncp),
        grid=(B, G),
        in_specs=[pl.BlockSpec((S, HEAD_DIM), lambda b, g: (b, kc_blk + g)),
                  pl.BlockSpec((S, HEAD_DIM), lambda b, g: (b, vc_blk + g)),
                  pl.BlockSpec((None, CMP_BLOCK, HEAD_DIM), full2),
                  pl.BlockSpec((None, CMP_BLOCK, HEAD_DIM, HEAD_DIM), full3),
                  pl.BlockSpec((None, HEAD_DIM, HEAD_DIM), full2),
                  pl.BlockSpec((None, CMP_BLOCK, HEAD_DIM), full2),
                  pl.BlockSpec((None, CMP_BLOCK, HEAD_DIM, HEAD_DIM), full3),
                  pl.BlockSpec((None, HEAD_DIM, HEAD_DIM), full2)],
        out_specs=[pl.BlockSpec((1, 1, HEAD_DIM, ncp), lambda b, g: (b, g, 0, 0)),
                   pl.BlockSpec((1, 1, ncp, HEAD_DIM), lambda b, g: (b, g, 0, 0))],
        out_shape=[jax.ShapeDtypeStruct((B, G, HEAD_DIM, ncp), BF16),
                   jax.ShapeDtypeStruct((B, G, ncp, HEAD_DIM), BF16)],
        scratch_shapes=[pltpu.VMEM((ncp + 8, HEAD_DIM), F32)],
        compiler_params=_cparams("parallel", "parallel"),
        name="nsa_compress",
    )(pf, pf, ck_pos, ck_w1, ck_w2, cv_pos, cv_w1, cv_w2)


def _cmp_topk_kernel(q_ref, kT_ref, v_ref, o_ref, bias_ref, imp_ref, *, tq, ncp):
    i = pl.program_id(2)
    s0 = i * tq
    t_col = s0 + lax.broadcasted_iota(jnp.int32, (tq, 1), 0)
    any_visible = (t_col >= CMP_BLOCK - 1).astype(F32)

    def attend(nb):
        nc = nb * LANES
        n_row = lax.broadcasted_iota(jnp.int32, (1, nc), 1)
        mc = jnp.logical_and(n_row * CMP_STRIDE + (CMP_BLOCK - 1) <= t_col, n_row < ncp - 1)
        kT = kT_ref[0, 0, :, 0:nc]
        v = v_ref[0, 0, 0:nc, :]
        psum = jnp.zeros((tq, nc), F32)
        for r in range(NSA_REP):
            sl = slice(r * HEAD_DIM, (r + 1) * HEAD_DIM)
            s = jnp.where(mc, _dot(q_ref[:, sl], kT), NEG)
            m = jnp.max(s, axis=-1, keepdims=True)
            e = jnp.exp2(s - m)
            l = jnp.sum(e, axis=-1, keepdims=True)
            pc = e * (any_visible / l)
            o_ref[:, sl] = _dot(pc.astype(BF16), v)
            psum = psum + pc
        n_col = lax.broadcasted_iota(jnp.int32, (nc, LANES), 0) * CMP_STRIDE
        k_row = lax.broadcasted_iota(jnp.int32, (nc, LANES), 1) * SEL_BLOCK
        ovl = jnp.logical_and(n_col < k_row + SEL_BLOCK, n_col + CMP_BLOCK > k_row).astype(BF16)
        hi = psum.astype(BF16)
        lo = (psum - hi.astype(F32)).astype(BF16)
        imp_ref[...] = _dot(hi, ovl) + _dot(lo, ovl)

    def topk(nvb):
        blk = lax.broadcasted_iota(jnp.int32, (nvb, LANES), 0)
        lane = lax.broadcasted_iota(jnp.int32, (nvb, LANES), 1)
        groups = []
        for c in range(0, tq, LANES):
            impT = imp_ref[c:c + LANES, :].T[0:nvb, :]
            cur = (s0 + c + lane) // SEL_BLOCK
            forced = jnp.logical_or(blk == 0, jnp.logical_or(blk == cur, blk == cur - 1))
            groups.append(jnp.where(forced, BELOW_NEG, jnp.where(blk <= cur, impT, NEG)))
        for _ in range(SEL_TOPK - 3):
            nxt = []
            for sc in groups:
                m = jnp.max(sc, axis=0, keepdims=True)
                idx = jnp.min(jnp.where(sc == m, blk, LANES), axis=0, keepdims=True)
                nxt.append(jnp.where(blk == idx, BELOW_NEG, sc))
            groups = nxt
        for g, sc in enumerate(groups):
            biasT = jnp.where(sc == BELOW_NEG, 0.0, NEG)
            if nvb < LANES:
                biasT = jnp.concatenate([biasT, jnp.full((LANES - nvb, LANES), NEG, F32)], axis=0)
            bias_ref[g * LANES:(g + 1) * LANES, :] = biasT.T.astype(BF16)

    nb_total = ncp // LANES
    sel_per_nb = LANES * CMP_STRIDE // SEL_BLOCK
    last_visible = (s0 + tq - CMP_BLOCK) // CMP_STRIDE
    nb_needed = jnp.minimum(last_visible // LANES + 1, nb_total)
    for nb in range(1, nb_total + 1):
        @pl.when(nb_needed == nb)
        def _(nb=nb):
            attend(nb)
            topk(min(LANES, nb * sel_per_nb))


def _cmp_topk(pb, kT, vcmp, B, S):
    G = NSA_KV_GROUPS
    T = B * S
    tq = min(S, CMP_TQ)
    nq = S // tq
    ncp = S // CMP_STRIDE
    gw = NSA_REP * HEAD_DIM
    return pl.pallas_call(
        functools.partial(_cmp_topk_kernel, tq=tq, ncp=ncp),
        grid=(B, G, nq),
        in_specs=[pl.BlockSpec((tq, gw), lambda b, g, i: (b * nq + i, g)),
                  pl.BlockSpec((1, 1, HEAD_DIM, ncp), lambda b, g, i: (b, g, 0, 0)),
                  pl.BlockSpec((1, 1, ncp, HEAD_DIM), lambda b, g, i: (b, g, 0, 0))],
        out_specs=[pl.BlockSpec((tq, gw), lambda b, g, i: (b * nq + i, g)),
                   pl.BlockSpec((tq, LANES), lambda b, g, i: (b * nq + i, g))],
        out_shape=[jax.ShapeDtypeStruct((T, NSA_HEADS * HEAD_DIM), F32),
                   jax.ShapeDtypeStruct((T, G * LANES), BF16)],
        scratch_shapes=[pltpu.VMEM((tq, LANES), F32)],
        compiler_params=_cparams("parallel", "parallel", "parallel"),
        name="nsa_cmp_topk",
    )(pb, kT, vcmp)


def _win_kernel(q_ref, kp_ref, kc_ref, vp_ref, vc_ref, o_ref, *, sub):
    i = pl.program_id(2)
    row = lax.broadcasted_iota(jnp.int32, (sub, sub), 0)
    col = lax.broadcasted_iota(jnp.int32, (sub, sub), 1)
    ones = jnp.ones((sub, LANES), BF16)

    def body(first):
        ks = [kp_ref[0:sub, :], kp_ref[sub:2 * sub, :], kc_ref[0:sub, :], kc_ref[sub:2 * sub, :]]
        vs = [jnp.concatenate([v, ones], axis=1) for v in
              (vp_ref[0:sub, :], vp_ref[sub:2 * sub, :], vc_ref[0:sub, :], vc_ref[sub:2 * sub, :])]
        for a in range(2):
            blocks = [(a, col > row), (a + 1, None), (a + 2, col <= row)]
            if first:
                blocks = blocks[2 - a:]
            rows = slice(a * sub, (a + 1) * sub)
            for r in range(NSA_REP):
                sl = slice(r * HEAD_DIM, (r + 1) * HEAD_DIM)
                q = q_ref[rows, sl]
                ss = []
                for kb, mask in blocks:
                    s = _dot_nt(q, ks[kb])
                    ss.append(s if mask is None else jnp.where(mask, s, NEG))
                m = functools.reduce(jnp.maximum, ss)
                m = jnp.max(m, axis=-1, keepdims=True)
                acc = None
                for s, (kb, _) in zip(ss, blocks):
                    part = _dot(jnp.exp2(s - m).astype(BF16), vs[kb])
                    acc = part if acc is None else acc + part
                o_ref[rows, sl] = acc[:, 0:HEAD_DIM] / acc[:, HEAD_DIM:2 * HEAD_DIM]

    pl.when(i > 0)(functools.partial(body, False))
    pl.when(i == 0)(functools.partial(body, True))


def _window(pb, B, S, k_blk, v_blk):
    G = NSA_KV_GROUPS
    T = B * S
    sub = WINDOW // 2
    tq = 2 * sub
    nq = S // tq
    gw = NSA_REP * HEAD_DIM

    def kv_spec(col0, back):
        return pl.BlockSpec((tq, HEAD_DIM),
                            lambda b, g, i: (b * nq + jnp.maximum(i - back, 0), col0 + g))

    return pl.pallas_call(
        functools.partial(_win_kernel, sub=sub),
        grid=(B, G, nq),
        in_specs=[pl.BlockSpec((tq, gw), lambda b, g, i: (b * nq + i, g)),
                  kv_spec(k_blk, 1), kv_spec(k_blk, 0), kv_spec(v_blk, 1), kv_spec(v_blk, 0)],
        out_specs=pl.BlockSpec((tq, gw), lambda b, g, i: (b * nq + i, g)),
        out_shape=jax.ShapeDtypeStruct((T, NSA_HEADS * HEAD_DIM), F32),
        compiler_params=_cparams("parallel", "parallel", "parallel"),
        name="nsa_window",
    )(pb, pb, pb, pb, pb)


def _sel_kernel(qi_ref, kj_ref, q_ref, bias_ref, k_ref, v_ref, o_ref, qa_ref, m_ref, acc_ref, *, tq, tk):
    n = pl.program_id(2)
    i = qi_ref[n]
    j = kj_ref[n]
    last_j = (i * tq + tq - 1) // tk

    @pl.when(j == 0)
    def _():
        for r in range(NSA_REP):
            rows = slice(r * tq, (r + 1) * tq)
            qa_ref[rows, 0:HEAD_DIM] = q_ref[:, r * HEAD_DIM:(r + 1) * HEAD_DIM]
            qa_ref[rows, HEAD_DIM:2 * HEAD_DIM] = bias_ref[...]
        m_ref[...] = jnp.full(m_ref.shape, NEG, F32)
        acc_ref[...] = jnp.zeros(acc_ref.shape, F32)

    def step(col0, ncols, diagonal):
        key = j * tk + col0 + lax.broadcasted_iota(jnp.int32, (ncols, LANES), 0)
        blk = lax.broadcasted_iota(jnp.int32, (ncols, LANES), 1)
        onehot = (key // SEL_BLOCK == blk).astype(BF16)
        ka = jnp.concatenate([k_ref[col0:col0 + ncols, :], onehot], axis=1)
        va = jnp.concatenate([v_ref[col0:col0 + ncols, :], jnp.ones((ncols, LANES), BF16)], axis=1)
        if diagonal:
            visible = (lax.broadcasted_iota(jnp.int32, (tq, tq), 1)
                       <= lax.broadcasted_iota(jnp.int32, (tq, tq), 0))
        def scores(r):
            s = _dot_nt(qa_ref[r * tq:(r + 1) * tq, :], ka)
            return jnp.where(visible, s, NEG) if diagonal else s

        s_next = scores(0)
        for r in range(NSA_REP):
            rows = slice(r * tq, (r + 1) * tq)
            s = s_next
            if r + 1 < NSA_REP:
                s_next = scores(r + 1)
            m_prev = m_ref[rows, :]
            m_new = jnp.maximum(m_prev, jnp.max(s, axis=-1, keepdims=True))
            alpha = jnp.exp2(m_prev - m_new)
            p = jnp.exp2(s - jnp.tile(m_new, (1, ncols // LANES)))
            acc_ref[rows, :] = jnp.tile(alpha, (1, 2)) * acc_ref[rows, :] + _dot(p.astype(BF16), va)
            m_ref[rows, :] = m_new

    @pl.when(j < last_j)
    def _():
        step(0, tk, False)

    for sub in range(tk // tq):
        @pl.when(jnp.logical_and(j == last_j, i % (tk // tq) == sub))
        def _(sub=sub):
            if sub > 0:
                step(0, sub * tq, False)
            step(sub * tq, tq, True)

    @pl.when(j == last_j)
    def _():
        for r in range(NSA_REP):
            rows = slice(r * tq, (r + 1) * tq)
            o_ref[:, r * HEAD_DIM:(r + 1) * HEAD_DIM] = (acc_ref[rows, 0:HEAD_DIM]
                                                         / acc_ref[rows, HEAD_DIM:2 * HEAD_DIM])


def _selection(pb, bias, B, S, k_blk, v_blk):
    G = NSA_KV_GROUPS
    T = B * S
    tq = min(S, SEL_TQ)
    tk = min(S, SEL_TK)
    assert tk % tq == 0
    nq = S // tq
    nk = S // tk
    gw = NSA_REP * HEAD_DIM
    pairs = [(i, j) for i in range(nq) for j in range((i * tq + tq - 1) // tk + 1)]
    qi = jnp.asarray([p[0] for p in pairs], jnp.int32)
    kj = jnp.asarray([p[1] for p in pairs], jnp.int32)

    def q_map(b, g, n, qi_ref, kj_ref):
        return (b * nq + qi_ref[n], g)

    def kv_spec(col0):
        return pl.BlockSpec((tk, HEAD_DIM),
                            lambda b, g, n, qi_ref, kj_ref: (b * nk + kj_ref[n], col0 + g))

    grid_spec = pltpu.PrefetchScalarGridSpec(
        num_scalar_prefetch=2,
        grid=(B, G, len(pairs)),
        in_specs=[pl.BlockSpec((tq, gw), q_map), pl.BlockSpec((tq, LANES), q_map),
                  kv_spec(k_blk), kv_spec(v_blk)],
        out_specs=pl.BlockSpec((tq, gw), q_map),
        scratch_shapes=[pltpu.VMEM((NSA_REP * tq, 2 * HEAD_DIM), BF16),
                        pltpu.VMEM((NSA_REP * tq, LANES), F32),
                        pltpu.VMEM((NSA_REP * tq, 2 * HEAD_DIM), F32)])
    return pl.pallas_call(
        functools.partial(_sel_kernel, tq=tq, tk=tk),
        grid_spec=grid_spec,
        out_shape=jax.ShapeDtypeStruct((T, NSA_HEADS * HEAD_DIM), F32),
        compiler_params=_cparams("parallel", "parallel", "arbitrary"),
        name="nsa_selection",
    )(qi, kj, pb, bias, pb, pb)


def _nsa_out_kernel(x_ref, oc_ref, os_ref, ow_ref, z_ref, gl_ref, w_ref, nw_ref, o_ref, hn_ref):
    sig = jax.nn.sigmoid(gl_ref[...])
    H = NSA_HEADS
    acc = x_ref[...]
    for hp in range(H // 2):
        parts = []
        for h in (2 * hp, 2 * hp + 1):
            sl = slice(h * HEAD_DIM, (h + 1) * HEAD_DIM)
            o = (sig[:, h:h + 1] * oc_ref[:, sl] + sig[:, H + h:H + h + 1] * os_ref[:, sl]
                 + sig[:, 2 * H + h:2 * H + h + 1] * ow_ref[:, sl])
            parts.append((o * _silu(z_ref[:, sl])).astype(BF16))
        rows = slice(2 * hp * HEAD_DIM, (2 * hp + 2) * HEAD_DIM)
        acc = acc + _dot(jnp.concatenate(parts, axis=1), w_ref[rows, :])
    o_ref[...] = acc
    hn_ref[...] = _rms_bf16(acc, nw_ref[...])


def _nsa_out(x, o_cmp, o_sel, o_win, pf, z_blk, gl_blk, w_out, layer, next_norm_w):
    T, D = x.shape
    W = o_cmp.shape[1]
    tm = min(T, OUT_TM)
    row = lambda i: (i, 0)
    return pl.pallas_call(
        _nsa_out_kernel,
        grid=(T // tm,),
        in_specs=[pl.BlockSpec((tm, D), row),
                  pl.BlockSpec((tm, W), row), pl.BlockSpec((tm, W), row), pl.BlockSpec((tm, W), row),
                  pl.BlockSpec((tm, W), lambda i: (i, z_blk)),
                  pl.BlockSpec((tm, LANES), lambda i: (i, gl_blk)),
                  pl.BlockSpec((None, W, D), lambda i: (layer, 0, 0)),
                  pl.BlockSpec((1, D), lambda i: (0, 0))],
        out_specs=[pl.BlockSpec((tm, D), row), pl.BlockSpec((tm, D), row)],
        out_shape=[jax.ShapeDtypeStruct((T, D), F32), jax.ShapeDtypeStruct((T, D), BF16)],
        compiler_params=_cparams("parallel"),
        name="nsa_out",
    )(x, o_cmp, o_sel, o_win, pf, pf, w_out, next_norm_w.reshape(1, D))


def _prepare_params(nsa_w_in, nsa_ck_pos, nsa_ck_w1, nsa_ck_w2, nsa_cv_pos, nsa_cv_w1, nsa_cv_w2,
                    nsa_w_out, hgrn_w_in, hgrn_w_out):
    H, G, d = NSA_HEADS, NSA_KV_GROUPS, HEAD_DIM
    kvw = G * d
    c = [0, H * d]
    for _ in range(6):
        c.append(c[-1] + kvw)
    c.append(c[-1] + 3 * H)
    c.append(c[-1] + H * d)
    seg = lambda n: nsa_w_in[:, :, c[n]:c[n + 1]]
    wq, wkc, wvc, wks, wvs, wkw, wvw, wgl, wz = [seg(n) for n in range(9)]
    pad = jnp.zeros(nsa_w_in.shape[:2] + (LANES - 3 * H,), nsa_w_in.dtype)
    return dict(
        nsa_wb=jnp.concatenate([wq, wks, wkw, wvs, wvw], axis=2).astype(BF16),
        nsa_wf=jnp.concatenate([wz, wkc, wvc, wgl, pad], axis=2).astype(BF16),
        ck_pos=nsa_ck_pos, ck_w1=nsa_ck_w1.astype(BF16), ck_w2=nsa_ck_w2.astype(BF16),
        cv_pos=nsa_cv_pos, cv_w1=nsa_cv_w1.astype(BF16), cv_w2=nsa_cv_w2.astype(BF16),
        nsa_wo=nsa_w_out.astype(BF16),
        hgrn_wi=hgrn_w_in, hgrn_wo=hgrn_w_out.astype(BF16))


def _nsa_layer(x, hn, B, S, tabs, prm, j, next_norm_w):
    H, G = NSA_HEADS, NSA_KV_GROUPS
    pb = _proj(hn, prm["nsa_wb"], j, BF16, 1024, ['rope_q'] * H + ['rope'] * (2 * G) + ['plain'] * (2 * G),
               tabs=tabs)
    pf = _proj(hn, prm["nsa_wf"], j, F32, 640, ['plain'] * H + ['rope'] * G + ['plain'] * (G + 1), tabs=tabs)
    ks_blk, kw_blk, vs_blk, vw_blk = H, H + G, H + 2 * G, H + 3 * G
    kc_blk, vc_blk, gl_blk = H, H + G, H + 2 * G

    kT, vcmp = _compress(pf, B, S, kc_blk, vc_blk, j, prm["ck_pos"], prm["ck_w1"], prm["ck_w2"],
                         prm["cv_pos"], prm["cv_w1"], prm["cv_w2"])
    o_cmp, bias = _cmp_topk(pb, kT, vcmp, B, S)
    o_win = _window(pb, B, S, kw_blk, vw_blk)
    o_sel = _selection(pb, bias, B, S, ks_blk, vs_blk)
    return _nsa_out(x, o_cmp, o_sel, o_win, pf, 0, gl_blk, prm["nsa_wo"], j, next_norm_w)


def _hgrn_kernel(q_ref, g_ref, i_ref, z_ref, gw_ref, o_ref, st_ref, *, C, sub, hb):
    n = pl.program_id(2)

    @pl.when(n == 0)
    def _():
        st_ref[...] = jnp.zeros(st_ref.shape, F32)

    r0 = lax.broadcasted_iota(jnp.int32, (C, C), 0)
    r1 = lax.broadcasted_iota(jnp.int32, (C, C), 1)
    tri = (r0 >= r1).astype(BF16)
    dsub = sub // 2
    drow = lax.broadcasted_iota(jnp.int32, (C // dsub, dsub, HEAD_DIM), 1)
    row_id = lax.broadcasted_iota(jnp.int32, (C, HEAD_DIM), 0)

    for h in range(hb):
        cs = slice(h * HEAD_DIM, (h + 1) * HEAD_DIM)
        g = g_ref[:, cs]
        k = 1.0 - jnp.exp(g)
        qs = q_ref[:, cs]
        iv = i_ref[:, cs]
        ib = iv.astype(BF16)

        g1 = g.astype(BF16)
        rem = g - g1.astype(F32)
        g2 = rem.astype(BF16)
        g3 = (rem - g2.astype(F32)).astype(BF16)
        b = _dot(tri, g1) + _dot(tri, g2) + _dot(tri, g3)

        stT = st_ref[h]
        o_inter = _dot_nt((qs * jnp.exp(b)).astype(BF16), stT.astype(BF16))

        nd = C // dsub
        b3 = b.reshape(nd, dsub, HEAD_DIM)
        q3 = qs.reshape(nd, dsub, HEAD_DIM)
        k3 = k.reshape(nd, dsub, HEAD_DIM)
        i3 = iv.reshape(nd, dsub, HEAD_DIM)
        o3 = jnp.zeros((nd, dsub, HEAD_DIM), F32)
        for s in range(dsub):
            w = jnp.where(drow >= s, jnp.exp(b3 - b3[:, s:s + 1, :]), 0.0)
            col = jnp.sum(q3 * w * k3[:, s:s + 1, :], axis=-1, keepdims=True)
            o3 = o3 + col * i3[:, s:s + 1, :]

        att = None
        bsz = 2 * dsub
        while bsz <= C:
            bl = b.reshape(C // bsz, bsz, HEAD_DIM)
            bmid = jnp.broadcast_to(bl[:, bsz // 2 - 1:bsz // 2, :], bl.shape).reshape(C, HEAD_DIM)
            second = (row_id % bsz) >= bsz // 2
            qk = jnp.where(second, qs, k) * jnp.exp(-jnp.abs(b - bmid))
            qh = jnp.where(second, qk, 0.0).astype(BF16)
            kh = jnp.where(second, 0.0, qk).astype(BF16)
            a = _dot_nt(qh, kh)
            if bsz < C:
                a = jnp.where(r0 // bsz == r1 // bsz, a, 0.0)
            att = a if att is None else att + a
            bsz *= 2
        o = o_inter + o3.reshape(C, HEAD_DIM) + _dot(att.astype(BF16), ib)

        b_last = b[C - 1:C, :]
        kd = (k * jnp.exp(b_last - b)).astype(BF16)
        st_ref[h] = stT * jnp.exp(b_last) + _dot(iv.T.astype(BF16), kd)

        o = o * lax.rsqrt(jnp.mean(o * o, axis=-1, keepdims=True) + RMS_EPS) * gw_ref[...]
        o_ref[:, cs] = (o * z_ref[:, cs]).astype(BF16)


def _hgrn_recurrence(ph, gnorm_w, B, S):
    H = HGRN_HEADS
    T = B * S
    C = min(S, HGRN_CHUNK)
    nc = S // C
    hb = HGRN_HEADS_PER_STEP
    hg = H // hb
    w = hb * HEAD_DIM

    def col(c0):
        return pl.BlockSpec((C, w), lambda b, h, n: (b * nc + n, c0 + h))

    return pl.pallas_call(
        functools.partial(_hgrn_kernel, C=C, sub=HGRN_SUB, hb=hb),
        grid=(B, hg, nc),
        in_specs=[col(0), col(hg), col(2 * hg), col(3 * hg),
                  pl.BlockSpec((1, HEAD_DIM), lambda b, h, n: (0, 0))],
        out_specs=pl.BlockSpec((C, w), lambda b, h, n: (b * nc + n, h)),
        out_shape=jax.ShapeDtypeStruct((T, H * HGRN_I_DIM), BF16),
        scratch_shapes=[pltpu.VMEM((hb, HGRN_I_DIM, HGRN_F_DIM), F32)],
        compiler_params=_cparams("parallel", "parallel", "arbitrary"),
        name="hgrn_recurrence",
    )(ph, ph, ph, ph, gnorm_w.reshape(1, HEAD_DIM))


def _hgrn_out_kernel(x_ref, og_ref, w_ref, nw_ref, *out_refs, final_norm):
    y = x_ref[...] + _dot(og_ref[...], w_ref[...])
    if final_norm:
        ms = jnp.mean(y * y, axis=-1, keepdims=True)
        out_refs[0][...] = y * lax.rsqrt(ms + RMS_EPS) * nw_ref[...]
    else:
        out_refs[0][...] = y
        out_refs[1][...] = _rms_bf16(y, nw_ref[...])


def _hgrn_out(x, og, w_out, layer, norm_w, final_norm):
    T, D = x.shape
    W = og.shape[1]
    tm = min(T, 2 * OUT_TM)
    row = lambda i: (i, 0)
    out_specs = [pl.BlockSpec((tm, D), row)]
    out_shape = [jax.ShapeDtypeStruct((T, D), F32)]
    if not final_norm:
        out_specs.append(pl.BlockSpec((tm, D), row))
        out_shape.append(jax.ShapeDtypeStruct((T, D), BF16))
    return pl.pallas_call(
        functools.partial(_hgrn_out_kernel, final_norm=final_norm),
        grid=(T // tm,),
        in_specs=[pl.BlockSpec((tm, D), row), pl.BlockSpec((tm, W), row),
                  pl.BlockSpec((None, W, D), lambda i: (layer, 0, 0)),
                  pl.BlockSpec((1, D), lambda i: (0, 0))],
        out_specs=out_specs,
        out_shape=out_shape,
        compiler_params=_cparams("parallel"),
        name="hgrn_out",
    )(x, og, w_out, norm_w.reshape(1, D))


def _hgrn_layer(x, hn, B, S, prm, lb_logits, gnorm_w, j, norm_w, final_norm):
    H = HGRN_HEADS
    kinds = ['silu'] * H + [('logf', h) for h in range(H)] + ['plain'] * H + ['silu'] * H
    ph = _proj(hn, prm["hgrn_wi"], j, F32, 512, kinds, lb_logits=lb_logits)
    og = _hgrn_recurrence(ph, gnorm_w, B, S)
    return _hgrn_out(x, og, prm["hgrn_wo"], j, norm_w, final_norm)


def kernel(x, positions, norm_w, final_norm_w, nsa_w_in, nsa_ck_pos, nsa_ck_w1, nsa_ck_w2,
           nsa_cv_pos, nsa_cv_w1, nsa_cv_w2, nsa_w_out, hgrn_w_in, hgrn_lb_logits,
           hgrn_gnorm_w, hgrn_w_out):
    B, S, D = x.shape
    depth = norm_w.shape[0]
    assert depth % 2 == 0
    xf = x.reshape(B * S, D)
    tabs = _rope_tables(positions)
    prm = _prepare_params(nsa_w_in, nsa_ck_pos, nsa_ck_w1, nsa_ck_w2, nsa_cv_pos, nsa_cv_w1,
                          nsa_cv_w2, nsa_w_out, hgrn_w_in, hgrn_w_out)
    hn = _rmsnorm(xf, norm_w[0])
    for layer in range(depth):
        j = layer // 2
        last = layer == depth - 1
        if layer % 2 == 0:
            xf, hn = _nsa_layer(xf, hn, B, S, tabs, prm, j, norm_w[layer + 1])
        elif last:
            (xf,) = _hgrn_layer(xf, hn, B, S, prm, hgrn_lb_logits, hgrn_gnorm_w[j], j, final_norm_w, True)
        else:
            xf, hn = _hgrn_layer(xf, hn, B, S, prm, hgrn_lb_logits, hgrn_gnorm_w[j], j,
                                 norm_w[layer + 1], False)
    return xf.reshape(B, S, D)
```

```python
import functools
import math

import jax
import jax.numpy as jnp
from jax import lax
from jax.experimental import pallas as pl
from jax.experimental.pallas import tpu as pltpu

F32 = jnp.float32
BF16 = jnp.bfloat16

RMS_EPS = 1e-6
NEG = -1e30
BELOW_NEG = -3e38
HEAD_DIM = 128
NSA_HEADS = 16
NSA_KV_GROUPS = 4
NSA_REP = NSA_HEADS // NSA_KV_GROUPS
CMP_BLOCK = 32
CMP_STRIDE = 16
SEL_BLOCK = 64
SEL_TOPK = 16
WINDOW = 512
ROPE_THETA = 500000.0
ROPE_DIM = HEAD_DIM // 4
HGRN_HEADS = 16
HGRN_F_DIM = 128
HGRN_I_DIM = 128

LANES = 128
VMEM_LIMIT_BYTES = 56 * 1024 * 1024

ATTN_SCALE = HEAD_DIM ** -0.5
EXP2_SCALE = ATTN_SCALE * math.log2(math.e)

PROJ_TM = 2048
PROJ_CHUNK_HEADS = 2
CMP_TQ = 1024
WIN_SUBTILES = 4
SEL_TQ = 512
SEL_TK = 2048
OUT_TM = 256
HGRN_CHUNK = 128
HGRN_SUB = 16
HGRN_HEADS_PER_STEP = 16


def _cparams(*sem):
    return pltpu.CompilerParams(dimension_semantics=sem, vmem_limit_bytes=VMEM_LIMIT_BYTES)


def _dot(a, b):
    return jnp.dot(a, b, preferred_element_type=F32)


def _dot_nt(a, b):
    return lax.dot_general(a, b, (((1,), (1,)), ((), ())), preferred_element_type=F32)


def _silu(v):
    return v * jax.nn.sigmoid(v)


_HALF = ROPE_DIM // 2


def _rope_table_kernel(pos_ref, inv_ref, cos_ref, sin_lo_ref, sin_hi_ref):
    ang = pos_ref[...] * inv_ref[...]
    lane = lax.broadcasted_iota(jnp.int32, ang.shape, 1)
    s = jnp.sin(ang)
    cos_ref[...] = jnp.cos(ang)
    sin_lo_ref[...] = jnp.where(lane < _HALF, -s, 0.0)
    sin_hi_ref[...] = jnp.where(lane < _HALF, 0.0, s)


def _rope_tables(positions):
    T = positions.size
    tm = min(T, PROJ_TM)
    inv = ROPE_THETA ** (-jnp.arange(_HALF, dtype=F32) / _HALF)
    inv_pat = jnp.concatenate([inv, inv, jnp.zeros((LANES - ROPE_DIM,), F32)]).reshape(1, LANES)
    pos = positions.astype(F32).reshape(T, 1)
    return pl.pallas_call(
        _rope_table_kernel,
        grid=(T // tm,),
        in_specs=[pl.BlockSpec((tm, 1), lambda i: (i, 0)),
                  pl.BlockSpec((1, LANES), lambda i: (0, 0))],
        out_specs=[pl.BlockSpec((tm, LANES), lambda i: (i, 0))] * 3,
        out_shape=[jax.ShapeDtypeStruct((T, LANES), F32)] * 3,
        compiler_params=_cparams("parallel"),
        name="rope_tables",
    )(pos, inv_pat)


def _rms_bf16(y, w):
    ms = jnp.mean(y * y, axis=-1, keepdims=True)
    return (y * lax.rsqrt(ms + RMS_EPS) * w).astype(BF16)


def _rmsnorm_kernel(x_ref, w_ref, o_ref):
    o_ref[...] = _rms_bf16(x_ref[...], w_ref[...])


def _rmsnorm(x, w):
    T, D = x.shape
    tm = min(T, 2 * OUT_TM)
    return pl.pallas_call(
        _rmsnorm_kernel,
        grid=(T // tm,),
        in_specs=[pl.BlockSpec((tm, D), lambda i: (i, 0)), pl.BlockSpec((1, D), lambda i: (0, 0))],
        out_specs=pl.BlockSpec((tm, D), lambda i: (i, 0)),
        out_shape=jax.ShapeDtypeStruct((T, D), BF16),
        compiler_params=_cparams("parallel"),
        name="rmsnorm",
    )(x, w.reshape(1, D))


def _log_forget(t, lbl, layer):
    e = jnp.exp(lbl - jnp.max(lbl, axis=0, keepdims=True))
    p = e / jnp.sum(e, axis=0, keepdims=True)
    lb = jnp.sum(p[0:layer + 1, :], axis=0, keepdims=True) - p[0:1, :]
    u = jnp.exp(-jnp.abs(t))
    num = jnp.where(t >= 0.0, 1.0 + lb * u, lb + u)
    return jnp.maximum(jnp.log(num), jnp.minimum(t, 0.0)) - jnp.log(1.0 + u)


def _proj_kernel(*refs, kinds_of_tile, has_rope, layer):
    if has_rope:
        hn_ref, w_ref, cos_ref, sin_lo_ref, sin_hi_ref, o_ref = refs
    elif any(isinstance(k, tuple) for kinds in kinds_of_tile for k in kinds):
        hn_ref, w_ref, lbl_ref, o_ref = refs
    else:
        hn_ref, w_ref, o_ref = refs
    j = pl.program_id(1)
    heads_per_tile = len(kinds_of_tile[0])

    def tile_body(kinds):
        h = 0
        while h < heads_per_tile:
            nh = min(PROJ_CHUNK_HEADS, heads_per_tile - h)
            acc = _dot(hn_ref[...], w_ref[:, h * HEAD_DIM:(h + nh) * HEAD_DIM].astype(BF16))
            for hh in range(nh):
                sl = slice((h + hh) * HEAD_DIM, (h + hh + 1) * HEAD_DIM)
                t = acc[:, hh * HEAD_DIM:(hh + 1) * HEAD_DIM]
                kind = kinds[h + hh]
                if kind in ('rope', 'rope_q'):
                    t = (t * cos_ref[...] + pltpu.roll(t, HEAD_DIM - _HALF, axis=1) * sin_lo_ref[...]
                         + pltpu.roll(t, _HALF, axis=1) * sin_hi_ref[...])
                if kind == 'rope_q':
                    t = t * EXP2_SCALE
                elif kind == 'silu':
                    t = _silu(t)
                elif isinstance(kind, tuple):
                    t = _log_forget(t, lbl_ref[:, kind[1] * HEAD_DIM:(kind[1] + 1) * HEAD_DIM], layer)
                o_ref[:, sl] = t.astype(o_ref.dtype)
            h += nh

    patterns = {}
    for jj, kinds in enumerate(kinds_of_tile):
        patterns.setdefault(kinds, []).append(jj)
    for kinds, tiles in patterns.items():
        cond = functools.reduce(jnp.logical_or, [j == jj for jj in tiles])
        pl.when(cond)(functools.partial(tile_body, kinds))


def _proj(hn, w, layer, out_dtype, tn, kinds, tabs=None, lb_logits=None):
    T, D = hn.shape
    N = w.shape[2]
    tm = min(T, PROJ_TM)
    hpt = tn // HEAD_DIM
    kinds_of_tile = tuple(tuple(kinds[jj * hpt:(jj + 1) * hpt]) for jj in range(N // tn))
    in_specs = [pl.BlockSpec((tm, D), lambda i, j: (i, 0)),
                pl.BlockSpec((None, D, tn), lambda i, j: (layer, 0, j))]
    args = [hn, w]
    if tabs is not None:
        in_specs += [pl.BlockSpec((tm, LANES), lambda i, j: (i, 0))] * len(tabs)
        args += list(tabs)
    if lb_logits is not None:
        in_specs.append(pl.BlockSpec(lb_logits.shape, lambda i, j: (0, 0)))
        args.append(lb_logits)
    return pl.pallas_call(
        functools.partial(_proj_kernel, kinds_of_tile=kinds_of_tile, has_rope=tabs is not None, layer=layer),
        grid=(T // tm, N // tn),
        in_specs=in_specs,
        out_specs=pl.BlockSpec((tm, tn), lambda i, j: (i, j)),
        out_shape=jax.ShapeDtypeStruct((T, N), out_dtype),
        compiler_params=_cparams("parallel", "arbitrary"),
        name="in_proj",
    )(*args)


def _compress_kernel(kc_ref, vc_ref, kpos_ref, kw1_ref, kw2_ref, vpos_ref, vw1_ref, vw2_ref,
                     kT_ref, v_ref, shift_ref, *, ncp):
    half_blk = CMP_BLOCK // 2

    def comp(t_ref, pos_ref, w1_ref, w2_ref):
        a = jnp.zeros((ncp, HEAD_DIM), F32)
        bm = jnp.zeros((ncp, HEAD_DIM), F32)
        for l in range(half_blk):
            xl = t_ref[pl.ds(l, ncp, stride=CMP_STRIDE), :]
            a = a + _dot((xl + pos_ref[l:l + 1, :]).astype(BF16), w1_ref[l])
            bm = bm + _dot((xl + pos_ref[half_blk + l:half_blk + l + 1, :]).astype(BF16),
                           w1_ref[half_blk + l])
        shift_ref[pl.ds(0, ncp), :] = bm
        shift_ref[pl.ds(ncp, 8), :] = jnp.zeros((8, HEAD_DIM), F32)
        hid = _silu(a + shift_ref[pl.ds(1, ncp), :])
        out = _dot(hid.astype(BF16), w2_ref[...])
        row = lax.broadcasted_iota(jnp.int32, out.shape, 0)
        return jnp.where(row < ncp - 1, out, 0.0)

    kT_ref[0, 0] = comp(kc_ref, kpos_ref, kw1_ref, kw2_ref).T.astype(BF16)
    v_ref[0, 0] = comp(vc_ref, vpos_ref, vw1_ref, vw2_ref).astype(BF16)


def _compress(pf, B, S, kc_blk, vc_blk, layer, ck_pos, ck_w1, ck_w2, cv_pos, cv_w1, cv_w2):
    G = NSA_KV_GROUPS
    ncp = S // CMP_STRIDE
    full2 = lambda b, g: (layer, 0, 0)
    full3 = lambda b, g: (layer, 0, 0, 0)
    return pl.pallas_call(
        functools.partial(_compress_kernel, ncp=ncp),
        grid=(B, G),
        in_specs=[pl.BlockSpec((S, HEAD_DIM), lambda b, g: (b, kc_blk + g)),
                  pl.BlockSpec((S, HEAD_DIM), lambda b, g: (b, vc_blk + g)),
                  pl.BlockSpec((None, CMP_BLOCK, HEAD_DIM), full2),
                  pl.BlockSpec((None, CMP_BLOCK, HEAD_DIM, HEAD_DIM), full3),
                  pl.BlockSpec((None, HEAD_DIM, HEAD_DIM), full2),
                  pl.BlockSpec((None, CMP_BLOCK, HEAD_DIM), full2),
                  pl.BlockSpec((None, CMP_BLOCK, HEAD_DIM, HEAD_DIM), full3),
                  pl.BlockSpec((None, HEAD_DIM, HEAD_DIM), full2)],
        out_specs=[pl.BlockSpec((1, 1, HEAD_DIM, ncp), lambda b, g: (b, g, 0, 0)),
                   pl.BlockSpec((1, 1, ncp, HEAD_DIM), lambda b, g: (b, g, 0, 0))],
        out_shape=[jax.ShapeDtypeStruct((B, G, HEAD_DIM, ncp), BF16),
                   jax.ShapeDtypeStruct((B, G, ncp, HEAD_DIM), BF16)],
        scratch_shapes=[pltpu.VMEM((ncp + 8, HEAD_DIM), F32)],
        compiler_params=_cparams("parallel", "parallel"),
        name="nsa_compress",
    )(pf, pf, ck_pos, ck_w1, ck_w2, cv_pos, cv_w1, cv_w2)


def _cmp_topk_kernel(q_ref, kT_ref, v_ref, o_ref, bias_ref, imp_ref, *, tq, ncp):
    i = pl.program_id(2)
    s0 = i * tq
    t_col = s0 + lax.broadcasted_iota(jnp.int32, (tq, 1), 0)
    any_visible = (t_col >= CMP_BLOCK - 1).astype(F32)

    def attend(nb):
        nc = nb * LANES
        n_row = lax.broadcasted_iota(jnp.int32, (1, nc), 1)
        mc = jnp.logical_and(n_row * CMP_STRIDE + (CMP_BLOCK - 1) <= t_col, n_row < ncp - 1)
        kT = kT_ref[0, 0, :, 0:nc]
        v = v_ref[0, 0, 0:nc, :]
        psum = jnp.zeros((tq, nc), F32)
        for r in range(NSA_REP):
            sl = slice(r * HEAD_DIM, (r + 1) * HEAD_DIM)
            s = jnp.where(mc, _dot(q_ref[:, sl], kT), NEG)
            m = jnp.max(s, axis=-1, keepdims=True)
            e = jnp.exp2(s - m)
            l = jnp.sum(e, axis=-1, keepdims=True)
            pc = e * (any_visible / l)
            o_ref[:, sl] = _dot(pc.astype(BF16), v)
            psum = psum + pc
        n_col = lax.broadcasted_iota(jnp.int32, (nc, LANES), 0) * CMP_STRIDE
        k_row = lax.broadcasted_iota(jnp.int32, (nc, LANES), 1) * SEL_BLOCK
        ovl = jnp.logical_and(n_col < k_row + SEL_BLOCK, n_col + CMP_BLOCK > k_row).astype(BF16)
        hi = psum.astype(BF16)
        lo = (psum - hi.astype(F32)).astype(BF16)
        imp_ref[...] = _dot(hi, ovl) + _dot(lo, ovl)

    def topk(nvb):
        blk = lax.broadcasted_iota(jnp.int32, (nvb, LANES), 0)
        lane = lax.broadcasted_iota(jnp.int32, (nvb, LANES), 1)
        groups = []
        for c in range(0, tq, LANES):
            impT = imp_ref[c:c + LANES, :].T[0:nvb, :]
            cur = (s0 + c + lane) // SEL_BLOCK
            forced = jnp.logical_or(blk == 0, jnp.logical_or(blk == cur, blk == cur - 1))
            groups.append(jnp.where(forced, BELOW_NEG, jnp.where(blk <= cur, impT, NEG)))
        for _ in range(SEL_TOPK - 3):
            nxt = []
            for sc in groups:
                m = jnp.max(sc, axis=0, keepdims=True)
                idx = jnp.min(jnp.where(sc == m, blk, LANES), axis=0, keepdims=True)
                nxt.append(jnp.where(blk == idx, BELOW_NEG, sc))
            groups = nxt
        for g, sc in enumerate(groups):
            biasT = jnp.where(sc == BELOW_NEG, 0.0, NEG)
            if nvb < LANES:
                biasT = jnp.concatenate([biasT, jnp.full((LANES - nvb, LANES), NEG, F32)], axis=0)
            bias_ref[g * LANES:(g + 1) * LANES, :] = biasT.T.astype(BF16)

    nb_total = ncp // LANES
    sel_per_nb = LANES * CMP_STRIDE // SEL_BLOCK
    last_visible = (s0 + tq - CMP_BLOCK) // CMP_STRIDE
    nb_needed = jnp.minimum(last_visible // LANES + 1, nb_total)
    for nb in range(1, nb_total + 1):
        @pl.when(nb_needed == nb)
        def _(nb=nb):
            attend(nb)
            topk(min(LANES, nb * sel_per_nb))


def _cmp_topk(pb, kT, vcmp, B, S):
    G = NSA_KV_GROUPS
    T = B * S
    tq = min(S, CMP_TQ)
    nq = S // tq
    ncp = S // CMP_STRIDE
    gw = NSA_REP * HEAD_DIM
    return pl.pallas_call(
        functools.partial(_cmp_topk_kernel, tq=tq, ncp=ncp),
        grid=(B, G, nq),
        in_specs=[pl.BlockSpec((tq, gw), lambda b, g, i: (b * nq + i, g)),
                  pl.BlockSpec((1, 1, HEAD_DIM, ncp), lambda b, g, i: (b, g, 0, 0)),
                  pl.BlockSpec((1, 1, ncp, HEAD_DIM), lambda b, g, i: (b, g, 0, 0))],
        out_specs=[pl.BlockSpec((tq, gw), lambda b, g, i: (b * nq + i, g)),
                   pl.BlockSpec((tq, LANES), lambda b, g, i: (b * nq + i, g))],
        out_shape=[jax.ShapeDtypeStruct((T, NSA_HEADS * HEAD_DIM), F32),
                   jax.ShapeDtypeStruct((T, G * LANES), BF16)],
        scratch_shapes=[pltpu.VMEM((tq, LANES), F32)],
        compiler_params=_cparams("parallel", "parallel", "parallel"),
        name="nsa_cmp_topk",
    )(pb, kT, vcmp)


def _win_kernel(q_ref, kp_ref, kc_ref, vp_ref, vc_ref, o_ref, *, sub, nsub):
    i = pl.program_id(2)
    row = lax.broadcasted_iota(jnp.int32, (sub, sub), 0)
    col = lax.broadcasted_iota(jnp.int32, (sub, sub), 1)
    ones = jnp.ones((sub, LANES), BF16)

    def body(first):
        def blk(ref, n):
            src, m = (ref[0], n) if n < nsub else (ref[1], n - nsub)
            return src[m * sub:(m + 1) * sub, :]
        for a in range(nsub):
            blocks = [(nsub + a - 2, col > row), (nsub + a - 1, None), (nsub + a, col <= row)]
            if first:
                blocks = [b for b in blocks if b[0] >= nsub]
            ks = {n: blk((kp_ref, kc_ref), n) for n, _ in blocks}
            vs = {n: jnp.concatenate([blk((vp_ref, vc_ref), n), ones], axis=1) for n, _ in blocks}
            rows = slice(a * sub, (a + 1) * sub)
            for r in range(NSA_REP):
                sl = slice(r * HEAD_DIM, (r + 1) * HEAD_DIM)
                q = q_ref[rows, sl]
                ss = []
                for kb, mask in blocks:
                    s = _dot_nt(q, ks[kb])
                    ss.append(s if mask is None else jnp.where(mask, s, NEG))
                m = functools.reduce(jnp.maximum, ss)
                m = jnp.max(m, axis=-1, keepdims=True)
                acc = None
                for s, (kb, _) in zip(ss, blocks):
                    part = _dot(jnp.exp2(s - m).astype(BF16), vs[kb])
                    acc = part if acc is None else acc + part
                o_ref[rows, sl] = acc[:, 0:HEAD_DIM] / acc[:, HEAD_DIM:2 * HEAD_DIM]

    pl.when(i > 0)(functools.partial(body, False))
    pl.when(i == 0)(functools.partial(body, True))


def _window(pb, B, S, k_blk, v_blk):
    G = NSA_KV_GROUPS
    T = B * S
    sub = WINDOW // 2
    nsub = WIN_SUBTILES
    tq = nsub * sub
    nq = S // tq
    gw = NSA_REP * HEAD_DIM

    def kv_spec(col0, back):
        return pl.BlockSpec((tq, HEAD_DIM),
                            lambda b, g, i: (b * nq + jnp.maximum(i - back, 0), col0 + g))

    return pl.pallas_call(
        functools.partial(_win_kernel, sub=sub, nsub=nsub),
        grid=(B, G, nq),
        in_specs=[pl.BlockSpec((tq, gw), lambda b, g, i: (b * nq + i, g)),
                  kv_spec(k_blk, 1), kv_spec(k_blk, 0), kv_spec(v_blk, 1), kv_spec(v_blk, 0)],
        out_specs=pl.BlockSpec((tq, gw), lambda b, g, i: (b * nq + i, g)),
        out_shape=jax.ShapeDtypeStruct((T, NSA_HEADS * HEAD_DIM), F32),
        compiler_params=_cparams("parallel", "parallel", "parallel"),
        name="nsa_window",
    )(pb, pb, pb, pb, pb)


def _sel_kernel(qi_ref, kj_ref, q_ref, bias_ref, k_ref, v_ref, o_ref, qa_ref, m_ref, acc_ref, *, tq, tk):
    n = pl.program_id(2)
    i = qi_ref[n]
    j = kj_ref[n]
    last_j = (i * tq + tq - 1) // tk

    @pl.when(j == 0)
    def _():
        for r in range(NSA_REP):
            rows = slice(r * tq, (r + 1) * tq)
            qa_ref[rows, 0:HEAD_DIM] = q_ref[:, r * HEAD_DIM:(r + 1) * HEAD_DIM]
            qa_ref[rows, HEAD_DIM:2 * HEAD_DIM] = bias_ref[...]
        m_ref[...] = jnp.full(m_ref.shape, NEG, F32)
        acc_ref[...] = jnp.zeros(acc_ref.shape, F32)

    def step(col0, ncols, diagonal):
        key = j * tk + col0 + lax.broadcasted_iota(jnp.int32, (ncols, LANES), 0)
        blk = lax.broadcasted_iota(jnp.int32, (ncols, LANES), 1)
        onehot = (key // SEL_BLOCK == blk).astype(BF16)
        ka = jnp.concatenate([k_ref[col0:col0 + ncols, :], onehot], axis=1)
        va = jnp.concatenate([v_ref[col0:col0 + ncols, :], jnp.ones((ncols, LANES), BF16)], axis=1)
        if diagonal:
            visible = (lax.broadcasted_iota(jnp.int32, (tq, tq), 1)
                       <= lax.broadcasted_iota(jnp.int32, (tq, tq), 0))
        def scores(r):
            s = _dot_nt(qa_ref[r * tq:(r + 1) * tq, :], ka)
            return jnp.where(visible, s, NEG) if diagonal else s

        s_next = scores(0)
        for r in range(NSA_REP):
            rows = slice(r * tq, (r + 1) * tq)
            s = s_next
            if r + 1 < NSA_REP:
                s_next = scores(r + 1)
            m_prev = m_ref[rows, :]
            m_new = jnp.maximum(m_prev, jnp.max(s, axis=-1, keepdims=True))
            alpha = jnp.exp2(m_prev - m_new)
            p = jnp.exp2(s - jnp.tile(m_new, (1, ncols // LANES)))
            acc_ref[rows, :] = jnp.tile(alpha, (1, 2)) * acc_ref[rows, :] + _dot(p.astype(BF16), va)
            m_ref[rows, :] = m_new

    @pl.when(j < last_j)
    def _():
        step(0, tk, False)

    for sub in range(tk // tq):
        @pl.when(jnp.logical_and(j == last_j, i % (tk // tq) == sub))
        def _(sub=sub):
            if sub > 0:
                step(0, sub * tq, False)
            step(sub * tq, tq, True)

    @pl.when(j == last_j)
    def _():
        for r in range(NSA_REP):
            rows = slice(r * tq, (r + 1) * tq)
            o_ref[:, r * HEAD_DIM:(r + 1) * HEAD_DIM] = (acc_ref[rows, 0:HEAD_DIM]
                                                         / acc_ref[rows, HEAD_DIM:2 * HEAD_DIM])


def _selection(pb, bias, B, S, k_blk, v_blk):
    G = NSA_KV_GROUPS
    T = B * S
    tq = min(S, SEL_TQ)
    tk = min(S, SEL_TK)
    assert tk % tq == 0
    nq = S // tq
    nk = S // tk
    gw = NSA_REP * HEAD_DIM
    pairs = [(i, j) for i in range(nq) for j in range((i * tq + tq - 1) // tk + 1)]
    qi = jnp.asarray([p[0] for p in pairs], jnp.int32)
    kj = jnp.asarray([p[1] for p in pairs], jnp.int32)

    def q_map(b, g, n, qi_ref, kj_ref):
        return (b * nq + qi_ref[n], g)

    def kv_spec(col0):
        return pl.BlockSpec((tk, HEAD_DIM),
                            lambda b, g, n, qi_ref, kj_ref: (b * nk + kj_ref[n], col0 + g))

    grid_spec = pltpu.PrefetchScalarGridSpec(
        num_scalar_prefetch=2,
        grid=(B, G, len(pairs)),
        in_specs=[pl.BlockSpec((tq, gw), q_map), pl.BlockSpec((tq, LANES), q_map),
                  kv_spec(k_blk), kv_spec(v_blk)],
        out_specs=pl.BlockSpec((tq, gw), q_map),
        scratch_shapes=[pltpu.VMEM((NSA_REP * tq, 2 * HEAD_DIM), BF16),
                        pltpu.VMEM((NSA_REP * tq, LANES), F32),
                        pltpu.VMEM((NSA_REP * tq, 2 * HEAD_DIM), F32)])
    return pl.pallas_call(
        functools.partial(_sel_kernel, tq=tq, tk=tk),
        grid_spec=grid_spec,
        out_shape=jax.ShapeDtypeStruct((T, NSA_HEADS * HEAD_DIM), F32),
        compiler_params=_cparams("parallel", "parallel", "arbitrary"),
        name="nsa_selection",
    )(qi, kj, pb, bias, pb, pb)


def _nsa_out_kernel(x_ref, oc_ref, os_ref, ow_ref, z_ref, gl_ref, w_ref, nw_ref, o_ref, hn_ref):
    sig = jax.nn.sigmoid(gl_ref[...])
    H = NSA_HEADS
    acc = x_ref[...]
    for hp in range(H // 2):
        parts = []
        for h in (2 * hp, 2 * hp + 1):
            sl = slice(h * HEAD_DIM, (h + 1) * HEAD_DIM)
            o = (sig[:, h:h + 1] * oc_ref[:, sl] + sig[:, H + h:H + h + 1] * os_ref[:, sl]
                 + sig[:, 2 * H + h:2 * H + h + 1] * ow_ref[:, sl])
            parts.append((o * _silu(z_ref[:, sl])).astype(BF16))
        rows = slice(2 * hp * HEAD_DIM, (2 * hp + 2) * HEAD_DIM)
        acc = acc + _dot(jnp.concatenate(parts, axis=1), w_ref[rows, :])
    o_ref[...] = acc
    hn_ref[...] = _rms_bf16(acc, nw_ref[...])


def _nsa_out(x, o_cmp, o_sel, o_win, pf, z_blk, gl_blk, w_out, layer, next_norm_w):
    T, D = x.shape
    W = o_cmp.shape[1]
    tm = min(T, OUT_TM)
    row = lambda i: (i, 0)
    return pl.pallas_call(
        _nsa_out_kernel,
        grid=(T // tm,),
        in_specs=[pl.BlockSpec((tm, D), row),
                  pl.BlockSpec((tm, W), row), pl.BlockSpec((tm, W), row), pl.BlockSpec((tm, W), row),
                  pl.BlockSpec((tm, W), lambda i: (i, z_blk)),
                  pl.BlockSpec((tm, LANES), lambda i: (i, gl_blk)),
                  pl.BlockSpec((None, W, D), lambda i: (layer, 0, 0)),
                  pl.BlockSpec((1, D), lambda i: (0, 0))],
        out_specs=[pl.BlockSpec((tm, D), row), pl.BlockSpec((tm, D), row)],
        out_shape=[jax.ShapeDtypeStruct((T, D), F32), jax.ShapeDtypeStruct((T, D), BF16)],
        compiler_params=_cparams("parallel"),
        name="nsa_out",
    )(x, o_cmp, o_sel, o_win, pf, pf, w_out, next_norm_w.reshape(1, D))


def _prepare_params(nsa_w_in, nsa_ck_pos, nsa_ck_w1, nsa_ck_w2, nsa_cv_pos, nsa_cv_w1, nsa_cv_w2,
                    nsa_w_out, hgrn_w_in, hgrn_w_out):
    H, G, d = NSA_HEADS, NSA_KV_GROUPS, HEAD_DIM
    kvw = G * d
    c = [0, H * d]
    for _ in range(6):
        c.append(c[-1] + kvw)
    c.append(c[-1] + 3 * H)
    c.append(c[-1] + H * d)
    seg = lambda n: nsa_w_in[:, :, c[n]:c[n + 1]]
    wq, wkc, wvc, wks, wvs, wkw, wvw, wgl, wz = [seg(n) for n in range(9)]
    pad = jnp.zeros(nsa_w_in.shape[:2] + (LANES - 3 * H,), nsa_w_in.dtype)
    return dict(
        nsa_wb=jnp.concatenate([wq, wks, wkw, wvs, wvw], axis=2).astype(BF16),
        nsa_wf=jnp.concatenate([wz, wkc, wvc, wgl, pad], axis=2).astype(BF16),
        ck_pos=nsa_ck_pos, ck_w1=nsa_ck_w1.astype(BF16), ck_w2=nsa_ck_w2.astype(BF16),
        cv_pos=nsa_cv_pos, cv_w1=nsa_cv_w1.astype(BF16), cv_w2=nsa_cv_w2.astype(BF16),
        nsa_wo=nsa_w_out.astype(BF16),
        hgrn_wi=hgrn_w_in, hgrn_wo=hgrn_w_out.astype(BF16))


def _nsa_layer(x, hn, B, S, tabs, prm, j, next_norm_w):
    H, G = NSA_HEADS, NSA_KV_GROUPS
    pb = _proj(hn, prm["nsa_wb"], j, BF16, 1024, ['rope_q'] * H + ['rope'] * (2 * G) + ['plain'] * (2 * G),
               tabs=tabs)
    pf = _proj(hn, prm["nsa_wf"], j, F32, 640, ['plain'] * H + ['rope'] * G + ['plain'] * (G + 1), tabs=tabs)
    ks_blk, kw_blk, vs_blk, vw_blk = H, H + G, H + 2 * G, H + 3 * G
    kc_blk, vc_blk, gl_blk = H, H + G, H + 2 * G

    kT, vcmp = _compress(pf, B, S, kc_blk, vc_blk, j, prm["ck_pos"], prm["ck_w1"], prm["ck_w2"],
                         prm["cv_pos"], prm["cv_w1"], prm["cv_w2"])
    o_cmp, bias = _cmp_topk(pb, kT, vcmp, B, S)
    o_win = _window(pb, B, S, kw_blk, vw_blk)
    o_sel = _selection(pb, bias, B, S, ks_blk, vs_blk)
    return _nsa_out(x, o_cmp, o_sel, o_win, pf, 0, gl_blk, prm["nsa_wo"], j, next_norm_w)


def _hgrn_kernel(q_ref, g_ref, i_ref, z_ref, gw_ref, o_ref, st_ref, *, C, sub, hb):
    n = pl.program_id(2)

    @pl.when(n == 0)
    def _():
        st_ref[...] = jnp.zeros(st_ref.shape, F32)

    r0 = lax.broadcasted_iota(jnp.int32, (C, C), 0)
    r1 = lax.broadcasted_iota(jnp.int32, (C, C), 1)
    tri = (r0 >= r1).astype(BF16)
    dsub = sub // 2
    drow = lax.broadcasted_iota(jnp.int32, (C // dsub, dsub, HEAD_DIM), 1)
    row_id = lax.broadcasted_iota(jnp.int32, (C, HEAD_DIM), 0)

    for h in range(hb):
        cs = slice(h * HEAD_DIM, (h + 1) * HEAD_DIM)
        g = g_ref[:, cs]
        k = 1.0 - jnp.exp(g)
        qs = q_ref[:, cs]
        iv = i_ref[:, cs]
        ib = iv.astype(BF16)

        g1 = g.astype(BF16)
        rem = g - g1.astype(F32)
        g2 = rem.astype(BF16)
        g3 = (rem - g2.astype(F32)).astype(BF16)
        b = _dot(tri, g1) + _dot(tri, g2) + _dot(tri, g3)

        stT = st_ref[h]
        o_inter = _dot_nt((qs * jnp.exp(b)).astype(BF16), stT.astype(BF16))

        nd = C // dsub
        b3 = b.reshape(nd, dsub, HEAD_DIM)
        q3 = qs.reshape(nd, dsub, HEAD_DIM)
        k3 = k.reshape(nd, dsub, HEAD_DIM)
        i3 = iv.reshape(nd, dsub, HEAD_DIM)
        o3 = jnp.zeros((nd, dsub, HEAD_DIM), F32)
        for s in range(dsub):
            w = jnp.where(drow >= s, jnp.exp(b3 - b3[:, s:s + 1, :]), 0.0)
            col = jnp.sum(q3 * w * k3[:, s:s + 1, :], axis=-1, keepdims=True)
            o3 = o3 + col * i3[:, s:s + 1, :]

        att = None
        bsz = 2 * dsub
        while bsz <= C:
            bl = b.reshape(C // bsz, bsz, HEAD_DIM)
            bmid = jnp.broadcast_to(bl[:, bsz // 2 - 1:bsz // 2, :], bl.shape).reshape(C, HEAD_DIM)
            second = (row_id % bsz) >= bsz // 2
            qk = jnp.where(second, qs, k) * jnp.exp(-jnp.abs(b - bmid))
            qh = jnp.where(second, qk, 0.0).astype(BF16)
            kh = jnp.where(second, 0.0, qk).astype(BF16)
            a = _dot_nt(qh, kh)
            if bsz < C:
                a = jnp.where(r0 // bsz == r1 // bsz, a, 0.0)
            att = a if att is None else att + a
            bsz *= 2
        o = o_inter + o3.reshape(C, HEAD_DIM) + _dot(att.astype(BF16), ib)

        b_last = b[C - 1:C, :]
        kd = (k * jnp.exp(b_last - b)).astype(BF16)
        st_ref[h] = stT * jnp.exp(b_last) + _dot(iv.T.astype(BF16), kd)

        o = o * lax.rsqrt(jnp.mean(o * o, axis=-1, keepdims=True) + RMS_EPS) * gw_ref[...]
        o_ref[:, cs] = (o * z_ref[:, cs]).astype(BF16)


def _hgrn_recurrence(ph, gnorm_w, B, S):
    H = HGRN_HEADS
    T = B * S
    C = min(S, HGRN_CHUNK)
    nc = S // C
    hb = HGRN_HEADS_PER_STEP
    hg = H // hb
    w = hb * HEAD_DIM

    def col(c0):
        return pl.BlockSpec((C, w), lambda b, h, n: (b * nc + n, c0 + h))

    return pl.pallas_call(
        functools.partial(_hgrn_kernel, C=C, sub=HGRN_SUB, hb=hb),
        grid=(B, hg, nc),
        in_specs=[col(0), col(hg), col(2 * hg), col(3 * hg),
                  pl.BlockSpec((1, HEAD_DIM), lambda b, h, n: (0, 0))],
        out_specs=pl.BlockSpec((C, w), lambda b, h, n: (b * nc + n, h)),
        out_shape=jax.ShapeDtypeStruct((T, H * HGRN_I_DIM), BF16),
        scratch_shapes=[pltpu.VMEM((hb, HGRN_I_DIM, HGRN_F_DIM), F32)],
        compiler_params=_cparams("parallel", "parallel", "arbitrary"),
        name="hgrn_recurrence",
    )(ph, ph, ph, ph, gnorm_w.reshape(1, HEAD_DIM))


def _hgrn_out_kernel(x_ref, og_ref, w_ref, nw_ref, *out_refs, final_norm):
    y = x_ref[...] + _dot(og_ref[...], w_ref[...])
    if final_norm:
        ms = jnp.mean(y * y, axis=-1, keepdims=True)
        out_refs[0][...] = y * lax.rsqrt(ms + RMS_EPS) * nw_ref[...]
    else:
        out_refs[0][...] = y
        out_refs[1][...] = _rms_bf16(y, nw_ref[...])


def _hgrn_out(x, og, w_out, layer, norm_w, final_norm):
    T, D = x.shape
    W = og.shape[1]
    tm = min(T, 2 * OUT_TM)
    row = lambda i: (i, 0)
    out_specs = [pl.BlockSpec((tm, D), row)]
    out_shape = [jax.ShapeDtypeStruct((T, D), F32)]
    if not final_norm:
        out_specs.append(pl.BlockSpec((tm, D), row))
        out_shape.append(jax.ShapeDtypeStruct((T, D), BF16))
    return pl.pallas_call(
        functools.partial(_hgrn_out_kernel, final_norm=final_norm),
        grid=(T // tm,),
        in_specs=[pl.BlockSpec((tm, D), row), pl.BlockSpec((tm, W), row),
                  pl.BlockSpec((None, W, D), lambda i: (layer, 0, 0)),
                  pl.BlockSpec((1, D), lambda i: (0, 0))],
        out_specs=out_specs,
        out_shape=out_shape,
        compiler_params=_cparams("parallel"),
        name="hgrn_out",
    )(x, og, w_out, norm_w.reshape(1, D))


def _hgrn_layer(x, hn, B, S, prm, lb_logits, gnorm_w, j, norm_w, final_norm):
    H = HGRN_HEADS
    kinds = ['silu'] * H + [('logf', h) for h in range(H)] + ['plain'] * H + ['silu'] * H
    ph = _proj(hn, prm["hgrn_wi"], j, F32, 512, kinds, lb_logits=lb_logits)
    og = _hgrn_recurrence(ph, gnorm_w, B, S)
    return _hgrn_out(x, og, prm["hgrn_wo"], j, norm_w, final_norm)


def kernel(x, positions, norm_w, final_norm_w, nsa_w_in, nsa_ck_pos, nsa_ck_w1, nsa_ck_w2,
           nsa_cv_pos, nsa_cv_w1, nsa_cv_w2, nsa_w_out, hgrn_w_in, hgrn_lb_logits,
           hgrn_gnorm_w, hgrn_w_out):
    B, S, D = x.shape
    depth = norm_w.shape[0]
    assert depth % 2 == 0
    xf = x.reshape(B * S, D)
    tabs = _rope_tables(positions)
    prm = _prepare_params(nsa_w_in, nsa_ck_pos, nsa_ck_w1, nsa_ck_w2, nsa_cv_pos, nsa_cv_w1,
                          nsa_cv_w2, nsa_w_out, hgrn_w_in, hgrn_w_out)
    hn = _rmsnorm(xf, norm_w[0])
    for layer in range(depth):
        j = layer // 2
        last = layer == depth - 1
        if layer % 2 == 0:
            xf, hn = _nsa_layer(xf, hn, B, S, tabs, prm, j, norm_w[layer + 1])
        elif last:
            (xf,) = _hgrn_layer(xf, hn, B, S, prm, hgrn_lb_logits, hgrn_gnorm_w[j], j, final_norm_w, True)
        else:
            xf, hn = _hgrn_layer(xf, hn, B, S, prm, hgrn_lb_logits, hgrn_gnorm_w[j], j,
                                 norm_w[layer + 1], False)
    return xf.reshape(B, S, D)
```
